```python
import jax, jax.numpy as jnp
from jax import lax
import numpy as np

D_MODEL = 2048
BATCH = 8
SEQ = 2048
DEPTH = 1

CONV_CHANNELS = D_MODEL // 2
DW_TAPS = 31
N_HEADS = 16
HEAD_DIM = 128
N_KV_GROUPS = 4
GROUP_SIZE = N_HEADS // N_KV_GROUPS
ROT_DIM = HEAD_DIM // 4
ROPE_THETA = 500000.0
CMP_BLOCK = 32
CMP_STRIDE = 16
CMP_HIDDEN = 256
SEL_BLOCK = 64
N_SELECT = 16
N_FORCED_LOCAL = 2
WINDOW = 512
WIN_QBLOCK = 128
SEL_QCHUNK = 16
N_GROUPS = 8
EXPERTS_PER_GROUP = 8
N_EXPERTS = N_GROUPS * EXPERTS_PER_GROUP
TOP_K = 2
EXPERT_FF = 512
MOE_ROWBLOCK = 128

Q_COLS = N_HEADS * HEAD_DIM
KV_COLS = N_KV_GROUPS * HEAD_DIM
IN_COLS = 2 * CONV_CHANNELS + Q_COLS + 6 * KV_COLS + 3 * N_HEADS
NORM_EPS = 1e-6
NEG_INF = -1e30
FORCE_BONUS = 1e4

kernel_name = "hybrid_conv_nsa_hiermoe"


def rms_norm(x, g):
    xf = x.astype(jnp.float32)
    y = xf * lax.rsqrt(jnp.mean(xf * xf, axis=-1, keepdims=True) + NORM_EPS)
    return y.astype(x.dtype) * g


def layer_norm(x, g, b):
    xf = x.astype(jnp.float32)
    mu = jnp.mean(xf, axis=-1, keepdims=True)
    var = jnp.mean(jnp.square(xf - mu), axis=-1, keepdims=True)
    return ((xf - mu) * lax.rsqrt(var + NORM_EPS)).astype(x.dtype) * g + b


def rope_tables(seq):
    pos = jnp.arange(seq, dtype=jnp.float32)
    inv = ROPE_THETA ** (-jnp.arange(0, ROT_DIM, 2, dtype=jnp.float32) / ROT_DIM)
    ang = pos[:, None] * inv[None, :]
    return jnp.cos(ang), jnp.sin(ang)


def partial_rope(x, cos, sin):
    half = ROT_DIM // 2
    c = cos[:, None, :].astype(x.dtype)
    s = sin[:, None, :].astype(x.dtype)
    x1, x2, rest = x[..., :half], x[..., half:ROT_DIM], x[..., ROT_DIM:]
    return jnp.concatenate([x1 * c - x2 * s, x1 * s + x2 * c, rest], axis=-1)


def masked_softmax(s, mask):
    s = jnp.where(mask, s.astype(jnp.float32), NEG_INF)
    return jax.nn.softmax(s, axis=-1) * mask


def conformer_conv(u_a, u_b, dw, dw_b, ln_g, ln_b):
    glu = u_a * jax.nn.sigmoid(u_b)
    padded = jnp.pad(glu, ((0, 0), (DW_TAPS - 1, 0), (0, 0)))
    y = lax.conv_general_dilated(padded, dw[:, None, :], window_strides=(1,), padding='VALID',
                                 dimension_numbers=('NWC', 'WIO', 'NWC'),
                                 feature_group_count=CONV_CHANNELS) + dw_b
    return jax.nn.silu(layer_norm(y, ln_g, ln_b))


def compress(kv, pos_emb, w1, w2):
    seq = kv.shape[2]
    n_cmp = (seq - CMP_BLOCK) // CMP_STRIDE + 1
    idx = np.arange(n_cmp)[:, None] * CMP_STRIDE + np.arange(CMP_BLOCK)[None, :]
    blocks = kv[:, :, idx] + pos_emb
    flat = blocks.reshape(blocks.shape[0], blocks.shape[1], n_cmp, CMP_BLOCK * HEAD_DIM)
    return jax.nn.gelu(flat @ w1) @ w2


def block_importance_matrix(n_cmp, n_blk):
    r_sel, r_cmp = SEL_BLOCK // CMP_STRIDE, CMP_BLOCK // CMP_STRIDE
    a = np.zeros((n_cmp, n_blk), np.float32)
    j = np.arange(n_blk)
    for m in range(r_sel):
        for n in range(r_cmp):
            i = r_sel * j + m - n
            ok = (i >= 0) & (i < n_cmp)
            np.add.at(a, (i[ok], j[ok]), 1.0)
    return jnp.asarray(a)


def nsa_attention(q, k_cmp, v_cmp, k_slc, v_slc, k_win, v_win, gates,
                  pos_k, k_w1, k_w2, pos_v, v_w1, v_w2):
    B, G, R, S, dh = q.shape
    t = jnp.arange(S)

    kc = compress(k_cmp, pos_k, k_w1, k_w2)
    vc = compress(v_cmp, pos_v, v_w1, v_w2)
    n_cmp = kc.shape[2]
    cmp_end = jnp.arange(n_cmp) * CMP_STRIDE + CMP_BLOCK - 1
    s_cmp = jnp.einsum('bgrsd,bgcd->bgrsc', q, kc)
    p_cmp = masked_softmax(s_cmp, cmp_end[None, :] <= t[:, None])
    o_cmp = jnp.einsum('bgrsc,bgcd->bgrsd', p_cmp.astype(vc.dtype), vc)

    n_blk = S // SEL_BLOCK
    imp = jnp.einsum('bgrsc,cj->bgsj', p_cmp, block_importance_matrix(n_cmp, n_blk))
    blk = jnp.arange(n_blk)[None, :]
    cur = (t // SEL_BLOCK)[:, None]
    forced = (blk == 0) | ((blk <= cur) & (blk > cur - N_FORCED_LOCAL))
    imp = jnp.where(forced, imp + FORCE_BONUS, imp)
    imp = jnp.where(blk * SEL_BLOCK <= t[:, None], imp, NEG_INF)
    n_sel = min(N_SELECT, n_blk)
    _, sel_idx = lax.top_k(imp, n_sel)

    n_chunk = S // SEL_QCHUNK
    q_chunks = q.reshape(B, G, R, n_chunk, SEL_QCHUNK, dh).transpose(3, 0, 1, 2, 4, 5)
    idx_chunks = sel_idx.reshape(B, G, n_chunk, SEL_QCHUNK, n_sel).transpose(2, 0, 1, 3, 4)
    k_blk = k_slc.reshape(B, G, n_blk, SEL_BLOCK, dh)
    v_blk = v_slc.reshape(B, G, n_blk, SEL_BLOCK, dh)
    b_ix = jnp.arange(B)[:, None, None, None]
    g_ix = jnp.arange(G)[None, :, None, None]
    n_keys = n_sel * SEL_BLOCK

    def sel_chunk(args):
        qc, ic, start = args
        kg = k_blk[b_ix, g_ix, ic]
        vg = v_blk[b_ix, g_ix, ic].reshape(B, G, SEL_QCHUNK, n_keys, dh)
        tq = start + jnp.arange(SEL_QCHUNK)
        kpos = ic[..., None] * SEL_BLOCK + jnp.arange(SEL_BLOCK)
        mask = (kpos <= tq[None, None, :, None, None]).reshape(B, G, 1, SEL_QCHUNK, n_keys)
        s = jnp.einsum('bgrqd,bgqnkd->bgrqnk', qc, kg).reshape(B, G, R, SEL_QCHUNK, n_keys)
        p = masked_softmax(s, mask)
        return jnp.einsum('bgrqj,bgqjd->bgrqd', p.astype(vg.dtype), vg)

    o_slc = lax.map(sel_chunk, (q_chunks, idx_chunks, jnp.arange(n_chunk) * SEL_QCHUNK))
    o_slc = o_slc.transpose(1, 2, 3, 0, 4, 5).reshape(B, G, R, S, dh)

    n_wb = S // WIN_QBLOCK
    span = WIN_QBLOCK + WINDOW
    q_blocks = q.reshape(B, G, R, n_wb, WIN_QBLOCK, dh).transpose(3, 0, 1, 2, 4, 5)
    k_pad = jnp.pad(k_win, ((0, 0), (0, 0), (WINDOW, 0), (0, 0)))
    v_pad = jnp.pad(v_win, ((0, 0), (0, 0), (WINDOW, 0), (0, 0)))

    def win_block(args):
        qb, start = args
        kb = lax.dynamic_slice_in_dim(k_pad, start, span, axis=2)
        vb = lax.dynamic_slice_in_dim(v_pad, start, span, axis=2)
        tq = start + jnp.arange(WIN_QBLOCK)
        kpos = start - WINDOW + jnp.arange(span)
        diff = tq[:, None] - kpos[None, :]
        mask = (kpos[None, :] >= 0) & (diff >= 0) & (diff < WINDOW)
        s = jnp.einsum('bgrqd,bgkd->bgrqk', qb, kb)
        p = masked_softmax(s, mask)
        return jnp.einsum('bgrqk,bgkd->bgrqd', p.astype(vb.dtype), vb)

    o_win = lax.map(win_block, (q_blocks, jnp.arange(n_wb) * WIN_QBLOCK))
    o_win = o_win.transpose(1, 2, 3, 0, 4, 5).reshape(B, G, R, S, dh)

    o = gates[0] * o_cmp + gates[1] * o_slc + gates[2] * o_win
    return o.transpose(0, 3, 1, 2, 4).reshape(B, S, N_HEADS * HEAD_DIM)


def hier_moe(h, w_grp, b_grp, w_exp, b_exp, w1, w3, w2):
    B, S, D = h.shape
    T = B * S
    hf = h.reshape(T, D)
    grp_logits = (hf @ w_grp).astype(jnp.float32) + b_grp.astype(jnp.float32)
    p_grp = jax.nn.softmax(grp_logits, axis=-1)
    g_sel = jnp.argmax(grp_logits, axis=-1)
    p_g = jnp.take_along_axis(p_grp, g_sel[:, None], axis=-1)
    exp_logits = ((hf @ w_exp).astype(jnp.float32) + b_exp.astype(jnp.float32)).reshape(T, N_GROUPS, EXPERTS_PER_GROUP)
    in_grp = jnp.take_along_axis(exp_logits, g_sel[:, None, None], axis=1)[:, 0]
    top_val, top_loc = lax.top_k(in_grp, TOP_K)
    gate = p_g * jax.nn.softmax(top_val, axis=-1)
    expert = g_sel[:, None] * EXPERTS_PER_GROUP + top_loc

    n_assign = T * TOP_K
    flat_e = expert.reshape(-1)
    flat_tok = jnp.repeat(jnp.arange(T, dtype=jnp.int32), TOP_K)
    flat_w = gate.reshape(-1)
    order = jnp.argsort(flat_e)
    e_s, tok_s, w_s = flat_e[order], flat_tok[order], flat_w[order]
    counts = jnp.zeros((N_EXPERTS,), jnp.int32).at[flat_e].add(1)
    padded = (counts + MOE_ROWBLOCK - 1) // MOE_ROWBLOCK * MOE_ROWBLOCK
    start = jnp.cumsum(counts) - counts
    pend = jnp.cumsum(padded)
    pstart = pend - padded
    dest = pstart[e_s] + (jnp.arange(n_assign, dtype=jnp.int32) - start[e_s])
    n_blocks = -(-(n_assign + N_EXPERTS * (MOE_ROWBLOCK - 1)) // MOE_ROWBLOCK)
    P = n_blocks * MOE_ROWBLOCK
    row_tok = jnp.full((P,), T, jnp.int32).at[dest].set(tok_s)
    row_w = jnp.zeros((P,), jnp.float32).at[dest].set(w_s)
    block_e = jnp.minimum(jnp.searchsorted(pend, jnp.arange(n_blocks) * MOE_ROWBLOCK, side='right'), N_EXPERTS - 1)
    x_rows = jnp.concatenate([hf, jnp.zeros((1, D), hf.dtype)], axis=0)[row_tok].reshape(n_blocks, MOE_ROWBLOCK, D)

    def expert_block(args):
        xb, e = args
        return (jax.nn.silu(xb @ w1[e]) * (xb @ w3[e])) @ w2[e]

    y_rows = lax.map(expert_block, (x_rows, block_e)).reshape(P, D)
    y = jax.ops.segment_sum(y_rows * row_w[:, None].astype(y_rows.dtype), row_tok, num_segments=T + 1)[:T]
    return y.reshape(B, S, D)


def setup_inputs(seed: int = 0) -> dict:
    key = jax.random.key(seed)
    ks = jax.random.split(key, 27)
    f32 = jnp.float32

    def nrm(k, shape, scale):
        return jax.random.normal(k, shape, f32) * scale

    D, C, L = D_MODEL, CONV_CHANNELS, DEPTH
    return {
        "x": nrm(ks[0], (BATCH, SEQ, D), 1.0),
        "attn_norm": 1.0 + nrm(ks[1], (L, D), 0.01),
        "w_in": nrm(ks[2], (L, D, IN_COLS), D ** -0.5),
        "conv_dw": nrm(ks[3], (L, DW_TAPS, C), DW_TAPS ** -0.5),
        "conv_dw_b": nrm(ks[4], (L, C), 0.01),
        "conv_ln_g": 1.0 + nrm(ks[5], (L, C), 0.01),
        "conv_ln_b": nrm(ks[6], (L, C), 0.01),
        "cmp_pos_k": nrm(ks[7], (L, CMP_BLOCK, HEAD_DIM), 0.02),
        "cmp_k_w1": nrm(ks[8], (L, CMP_BLOCK * HEAD_DIM, CMP_HIDDEN), (CMP_BLOCK * HEAD_DIM) ** -0.5),
        "cmp_k_w2": nrm(ks[9], (L, CMP_HIDDEN, HEAD_DIM), CMP_HIDDEN ** -0.5),
        "cmp_pos_v": nrm(ks[10], (L, CMP_BLOCK, HEAD_DIM), 0.02),
        "cmp_v_w1": nrm(ks[11], (L, CMP_BLOCK * HEAD_DIM, CMP_HIDDEN), (CMP_BLOCK * HEAD_DIM) ** -0.5),
        "cmp_v_w2": nrm(ks[12], (L, CMP_HIDDEN, HEAD_DIM), CMP_HIDDEN ** -0.5),
        "w_proj_conv": nrm(ks[13], (L, C, D), C ** -0.5),
        "w_proj_nsa": nrm(ks[14], (L, Q_COLS, D), Q_COLS ** -0.5),
        "w_merge": nrm(ks[15], (L, D, 2 * D), D ** -0.5),
        "b_merge": nrm(ks[16], (L, 2 * D), 0.01),
        "w_out": nrm(ks[17], (L, D, D), D ** -0.5),
        "ffn_norm": 1.0 + nrm(ks[18], (L, D), 0.01),
        "w_grp": nrm(ks[19], (L, D, N_GROUPS), D ** -0.5),
        "b_grp": nrm(ks[20], (L, N_GROUPS), 0.01),
        "w_exp": nrm(ks[21], (L, D, N_EXPERTS), D ** -0.5),
        "b_exp": nrm(ks[22], (L, N_EXPERTS), 0.01),
        "exp_w1": nrm(ks[23], (L, N_EXPERTS, D, EXPERT_FF), D ** -0.5),
        "exp_w3": nrm(ks[24], (L, N_EXPERTS, D, EXPERT_FF), D ** -0.5),
        "exp_w2": nrm(ks[25], (L, N_EXPERTS, EXPERT_FF, D), EXPERT_FF ** -0.5),
        "final_norm": 1.0 + nrm(ks[26], (D,), 0.01),
    }


def reference(x, attn_norm, w_in, conv_dw, conv_dw_b, conv_ln_g, conv_ln_b,
              cmp_pos_k, cmp_k_w1, cmp_k_w2, cmp_pos_v, cmp_v_w1, cmp_v_w2,
              w_proj_conv, w_proj_nsa, w_merge, b_merge, w_out, ffn_norm,
              w_grp, b_grp, w_exp, b_exp, exp_w1, exp_w3, exp_w2, final_norm):
    B, S, _ = x.shape
    cos, sin = rope_tables(S)
    scale = HEAD_DIM ** -0.5
    sizes = [CONV_CHANNELS, CONV_CHANNELS, Q_COLS] + [KV_COLS] * 6
    split_points = [int(v) for v in np.cumsum(sizes)]

    def kv_heads(u, rotate):
        u = u.reshape(B, S, N_KV_GROUPS, HEAD_DIM)
        if rotate:
            u = partial_rope(u, cos, sin)
        return u.transpose(0, 2, 1, 3)

    for l in range(DEPTH):
        h = rms_norm(x, attn_norm[l])
        proj = h @ w_in[l]
        conv_a, conv_b, q, kc, vc, ksl, vsl, kw, vw, gl = jnp.split(proj, split_points, axis=-1)

        y_conv = conformer_conv(conv_a, conv_b, conv_dw[l], conv_dw_b[l], conv_ln_g[l], conv_ln_b[l]) @ w_proj_conv[l]

        qh = partial_rope(q.reshape(B, S, N_HEADS, HEAD_DIM), cos, sin) * scale
        qh = qh.reshape(B, S, N_KV_GROUPS, GROUP_SIZE, HEAD_DIM).transpose(0, 2, 3, 1, 4)
        gates = jax.nn.sigmoid(gl).reshape(B, S, 3, N_KV_GROUPS, GROUP_SIZE).transpose(2, 0, 3, 4, 1)[..., None]
        o_nsa = nsa_attention(qh, kv_heads(kc, True), kv_heads(vc, False), kv_heads(ksl, True), kv_heads(vsl, False),
                              kv_heads(kw, True), kv_heads(vw, False), gates,
                              cmp_pos_k[l], cmp_k_w1[l], cmp_k_w2[l], cmp_pos_v[l], cmp_v_w1[l], cmp_v_w2[l])
        y_nsa = o_nsa @ w_proj_nsa[l]

        g_conv, g_nsa = jnp.split(jax.nn.sigmoid(h @ w_merge[l] + b_merge[l]), 2, axis=-1)
        x = x + (g_conv * y_conv + g_nsa * y_nsa) @ w_out[l]

        x = x + hier_moe(rms_norm(x, ffn_norm[l]), w_grp[l], b_grp[l], w_exp[l], b_exp[l],
                         exp_w1[l], exp_w3[l], exp_w2[l])
    return rms_norm(x, final_norm)
```

```python
import functools

import numpy as np
import jax
import jax.numpy as jnp
from jax import lax
from jax.experimental import pallas as pl
from jax.experimental.pallas import tpu as pltpu

D_MODEL = 2048
CONV_CHANNELS = D_MODEL // 2
DW_TAPS = 31
N_HEADS = 16
HEAD_DIM = 128
N_KV_GROUPS = 4
GROUP_SIZE = N_HEADS // N_KV_GROUPS
ROT_DIM = HEAD_DIM // 4
ROPE_THETA = 500000.0
CMP_BLOCK = 32
CMP_STRIDE = 16
CMP_HIDDEN = 256
SEL_BLOCK = 64
N_SELECT = 16
N_FORCED_LOCAL = 2
WINDOW = 512
N_GROUPS = 8
EXPERTS_PER_GROUP = 8
N_EXPERTS = N_GROUPS * EXPERTS_PER_GROUP
TOP_K = 2
EXPERT_FF = 512
Q_COLS = N_HEADS * HEAD_DIM
KV_COLS = N_KV_GROUPS * HEAD_DIM
HEAD_COLS = Q_COLS + 6 * KV_COLS
GATE_COLS = 3 * N_HEADS
NORM_EPS = 1e-6
NEG_INF = -1e30
FORCE_BONUS = 1e4

LANES = 128
VMEM_LIMIT_BYTES = 56 * 1024 * 1024

ROW_TILE = 512
HEAD_TILE = GROUP_SIZE * HEAD_DIM
Q_TILE = 128
SEL_CHUNK = 512
CONV_TILE = 256
CONV_HALO = 32
OUT_TILE = 256
MOE_ROWS = 128
FINAL_TILE = 128

_BF16 = jnp.bfloat16
_F32 = jnp.float32


def _params(*sem):
    return pltpu.CompilerParams(dimension_semantics=sem, vmem_limit_bytes=VMEM_LIMIT_BYTES)


def _dot(a, b):
    return jnp.dot(a, b, preferred_element_type=_F32)


def _dot_nt(a, b):
    return lax.dot_general(a, b, (((1,), (1,)), ((), ())), preferred_element_type=_F32)


def _sigmoid(x):
    return 1.0 / (1.0 + jnp.exp(-x))


def _rmsnorm_kernel(x_ref, g_ref, o_ref):
    x = x_ref[...]
    ms = jnp.mean(x * x, axis=-1, keepdims=True)
    o_ref[...] = (x * lax.rsqrt(ms + NORM_EPS) * g_ref[...]).astype(o_ref.dtype)


def _rmsnorm(x, g, out_dtype):
    t, d = x.shape
    return pl.pallas_call(
        _rmsnorm_kernel,
        out_shape=jax.ShapeDtypeStruct((t, d), out_dtype),
        grid=(t // ROW_TILE,),
        in_specs=[pl.BlockSpec((ROW_TILE, d), lambda i: (i, 0)),
                  pl.BlockSpec((1, d), lambda i: (0, 0))],
        out_specs=pl.BlockSpec((ROW_TILE, d), lambda i: (i, 0)),
        compiler_params=_params("parallel"),
        name="rmsnorm",
    )(x, g.reshape(1, d))


def _glu_kernel(h_ref, wa_ref, wb_ref, o_ref):
    h = h_ref[...]
    a = _dot(h, wa_ref[...])
    b = _dot(h, wb_ref[...])
    o_ref[...] = a * _sigmoid(b)


def _glu_proj(h, w_in_bf16):
    t, d = h.shape
    tn = HEAD_TILE
    nb = CONV_CHANNELS // tn
    return pl.pallas_call(
        _glu_kernel,
        out_shape=jax.ShapeDtypeStruct((t, CONV_CHANNELS), _F32),
        grid=(t // ROW_TILE, nb),
        in_specs=[pl.BlockSpec((ROW_TILE, d), lambda i, j: (i, 0)),
                  pl.BlockSpec((d, tn), lambda i, j: (0, j)),
                  pl.BlockSpec((d, tn), lambda i, j: (0, j + nb))],
        out_specs=pl.BlockSpec((ROW_TILE, tn), lambda i, j: (i, j)),
        compiler_params=_params("parallel", "arbitrary"),
        name="glu_proj",
    )(h, w_in_bf16, w_in_bf16)


N_Q_SLABS = Q_COLS // HEAD_TILE
N_SLABS = HEAD_COLS // HEAD_TILE


def _rope_tables(seq):
    pos = jnp.arange(seq, dtype=_F32)
    inv = ROPE_THETA ** (-jnp.arange(0, ROT_DIM, 2, dtype=_F32) / ROT_DIM)
    ang = pos[:, None] * inv[None, :]
    cos, sin = jnp.cos(ang), jnp.sin(ang)
    half = ROT_DIM // 2
    ones = jnp.ones((seq, HEAD_DIM - ROT_DIM), _F32)
    zeros_h = jnp.zeros((seq, half), _F32)
    zeros_r = jnp.zeros((seq, HEAD_DIM - ROT_DIM), _F32)
    c = jnp.concatenate([cos, cos, ones], axis=-1)
    s_lo = jnp.concatenate([zeros_h, sin, zeros_r], axis=-1)
    s_hi = jnp.concatenate([-sin, zeros_h, zeros_r], axis=-1)
    return c, s_lo, s_hi


def _heads_kernel(h_ref, w_ref, c_ref, slo_ref, shi_ref, o_ref):
    j = pl.program_id(1)
    acc = _dot(h_ref[...], w_ref[...])
    is_q = j < N_Q_SLABS
    rotate = is_q | (j % 2 == 0)
    half = ROT_DIM // 2

    @pl.when(rotate)
    def _():
        scale = jnp.where(is_q, HEAD_DIM ** -0.5, 1.0).astype(_F32)
        c, slo, shi = c_ref[...], slo_ref[...], shi_ref[...]
        for hd in range(GROUP_SIZE):
            xs = acc[:, hd * HEAD_DIM:(hd + 1) * HEAD_DIM]
            r = (xs * c + pltpu.roll(xs, half, 1) * slo
                 + pltpu.roll(xs, HEAD_DIM - half, 1) * shi)
            o_ref[0, hd] = (r * scale).astype(o_ref.dtype)

    @pl.when(jnp.logical_not(rotate))
    def _():
        for hd in range(GROUP_SIZE):
            o_ref[0, hd] = acc[:, hd * HEAD_DIM:(hd + 1) * HEAD_DIM].astype(o_ref.dtype)


def _heads_proj(h, w_in_bf16, seq):
    t, d = h.shape
    tm = ROW_TILE
    c, slo, shi = _rope_tables(seq)
    col0 = 2 * CONV_CHANNELS // HEAD_TILE
    spt = seq // tm
    tab = pl.BlockSpec((tm, HEAD_DIM), lambda i, j: (i % spt, 0))
    return pl.pallas_call(
        _heads_kernel,
        out_shape=jax.ShapeDtypeStruct((N_SLABS, GROUP_SIZE, t, HEAD_DIM), _BF16),
        grid=(t // tm, N_SLABS),
        in_specs=[pl.BlockSpec((tm, d), lambda i, j: (i, 0)),
                  pl.BlockSpec((d, HEAD_TILE), lambda i, j: (0, j + col0)),
                  tab, tab, tab],
        out_specs=pl.BlockSpec((1, GROUP_SIZE, tm, HEAD_DIM), lambda i, j: (j, 0, i, 0)),
        compiler_params=_params("parallel", "arbitrary"),
        name="heads_proj",
    )(h, w_in_bf16, c, slo, shi)


def _gate_kernel(h_ref, w_ref, o_ref):
    o_ref[...] = _sigmoid(_dot(h_ref[...], w_ref[...]))


def _gate_proj(h, w_gate_bf16):
    t, d = h.shape
    n = w_gate_bf16.shape[1]
    return pl.pallas_call(
        _gate_kernel,
        out_shape=jax.ShapeDtypeStruct((t, n), _F32),
        grid=(t // ROW_TILE,),
        in_specs=[pl.BlockSpec((ROW_TILE, d), lambda i: (i, 0)),
                  pl.BlockSpec((d, n), lambda i: (0, 0))],
        out_specs=pl.BlockSpec((ROW_TILE, n), lambda i: (i, 0)),
        compiler_params=_params("parallel"),
        name="gate_proj",
    )(h, w_gate_bf16)


def _compress_kernel(kv_ref, pos_ref, w1_ref, w2_ref, o_ref):
    half = CMP_STRIDE * HEAD_DIM
    c = kv_ref[0, 0].astype(_F32)
    n_chunk = c.shape[0]
    top = (c + pos_ref[0, 0]).astype(_BF16)
    bot = (c + pos_ref[0, 1]).astype(_BF16)
    u = _dot(top, w1_ref[0, :half, :])
    v = _dot(bot, w1_ref[0, half:, :])
    hidden = u + pltpu.roll(v, n_chunk - 1, 0)
    act = jax.nn.gelu(hidden).astype(_BF16)
    o_ref[0, 0, 0] = _dot(act, w2_ref[0]).astype(o_ref.dtype)


def _compress(heads, pos, w1, w2, batch, seq):
    t = heads.shape[2]
    n_chunk = seq // CMP_STRIDE
    flat = heads.reshape(N_SLABS, N_KV_GROUPS, t // CMP_STRIDE, CMP_STRIDE * HEAD_DIM)
    return pl.pallas_call(
        _compress_kernel,
        out_shape=jax.ShapeDtypeStruct((2, batch, N_KV_GROUPS, n_chunk, HEAD_DIM), _BF16),
        grid=(2, batch, N_KV_GROUPS),
        in_specs=[pl.BlockSpec((1, 1, n_chunk, CMP_STRIDE * HEAD_DIM),
                               lambda s, b, g: (N_Q_SLABS + s, g, b, 0)),
                  pl.BlockSpec((1, 2, 1, CMP_STRIDE * HEAD_DIM), lambda s, b, g: (s, 0, 0, 0)),
                  pl.BlockSpec((1, CMP_BLOCK * HEAD_DIM, CMP_HIDDEN), lambda s, b, g: (s, 0, 0)),
                  pl.BlockSpec((1, CMP_HIDDEN, HEAD_DIM), lambda s, b, g: (s, 0, 0))],
        out_specs=pl.BlockSpec((1, 1, 1, n_chunk, HEAD_DIM), lambda s, b, g: (s, b, g, 0, 0)),
        compiler_params=_params("parallel", "parallel", "parallel"),
        name="compress",
    )(flat, pos, w1, w2)


def _conv_kernel(cur_ref, halo_ref, dw_ref, dwb_ref, g_ref, b_ref, o_ref, ext_ref):
    i = pl.program_id(1)
    ts = cur_ref.shape[0]
    ext_ref[:CONV_HALO, :] = jnp.where(i > 0, halo_ref[...], 0.0)
    ext_ref[CONV_HALO:, :] = cur_ref[...]
    off = CONV_HALO - (DW_TAPS - 1)
    y = ext_ref[off:off + ts, :] * dw_ref[0:1, :]
    for k in range(1, DW_TAPS):
        y = y + ext_ref[off + k:off + k + ts, :] * dw_ref[k:k + 1, :]
    y = y + dwb_ref[...]
    mu = jnp.mean(y, axis=-1, keepdims=True)
    yc = y - mu
    var = jnp.mean(yc * yc, axis=-1, keepdims=True)
    z = yc * lax.rsqrt(var + NORM_EPS) * g_ref[...] + b_ref[...]
    o_ref[...] = (z * _sigmoid(z)).astype(o_ref.dtype)


def _conv(glu, dw, dw_b, ln_g, ln_b, batch, seq):
    t, c = glu.shape
    ts = CONV_TILE
    nt = seq // ts
    hb = ts // CONV_HALO
    dw_pad = jnp.concatenate([dw, jnp.zeros((CONV_HALO - DW_TAPS, c), _F32)], axis=0)
    vec = pl.BlockSpec((1, c), lambda b, i: (0, 0))
    return pl.pallas_call(
        _conv_kernel,
        out_shape=jax.ShapeDtypeStruct((t, c), _BF16),
        grid=(batch, nt),
        in_specs=[pl.BlockSpec((ts, c), lambda b, i: (b * nt + i, 0)),
                  pl.BlockSpec((CONV_HALO, c), lambda b, i: (jnp.maximum((b * nt + i) * hb - 1, 0), 0)),
                  pl.BlockSpec((CONV_HALO, c), lambda b, i: (0, 0)),
                  vec, vec, vec],
        out_specs=pl.BlockSpec((ts, c), lambda b, i: (b * nt + i, 0)),
        scratch_shapes=[pltpu.VMEM((ts + CONV_HALO, c), _F32)],
        compiler_params=_params("parallel", "arbitrary"),
        name="conv",
    )(glu, glu, dw_pad, dw_b.reshape(1, c), ln_g.reshape(1, c), ln_b.reshape(1, c))


def _importance_matrix(n_cmp_pad, n_blk):
    r_sel, r_cmp = SEL_BLOCK // CMP_STRIDE, CMP_BLOCK // CMP_STRIDE
    a = np.zeros((n_cmp_pad, LANES), np.float32)
    for j in range(n_blk):
        for m in range(r_sel):
            for n in range(r_cmp):
                c = r_sel * j + m - n
                if 0 <= c < n_cmp_pad - 1:
                    a[c, j] += 1.0
    return a


def _expand_matrix(n_blk, seq):
    e = np.zeros((LANES, seq), np.float32)
    for j in range(n_blk):
        e[j, j * SEL_BLOCK:(j + 1) * SEL_BLOCK] = 1.0
    return e


def _masked_softmax(s, mask):
    m = jnp.max(jnp.where(mask, s, NEG_INF), axis=-1, keepdims=True)
    e = jnp.where(mask, jnp.exp(s - m), 0.0)
    l = jnp.sum(e, axis=-1, keepdims=True)
    return e / jnp.where(l > 0.0, l, 1.0)


def _nsa_kernel(q_ref, kc_ref, vc_ref, ks_ref, vs_ref, kw_ref, vw_ref, gate_ref, amat_ref, emat_ref,
                o_ref, m_scr, l_scr, acc_scr):
    tq_n = q_ref.shape[2]
    seq = ks_ref.shape[2]
    n_cmp_pad = kc_ref.shape[3]
    n_blk = seq // SEL_BLOCK
    rows = GROUP_SIZE * tq_n
    start = pl.program_id(2) * tq_n
    q = q_ref[0].reshape(rows, HEAD_DIM)
    tq = start + lax.broadcasted_iota(jnp.int32, (tq_n, 1), 0)

    cidx = lax.broadcasted_iota(jnp.int32, (1, n_cmp_pad), 1)
    cmask = (cidx * CMP_STRIDE + (CMP_BLOCK - 1) <= tq) & (cidx < n_cmp_pad - 1)
    s = _dot_nt(q, kc_ref[0, 0, 0]).reshape(GROUP_SIZE, tq_n, n_cmp_pad)
    p = _masked_softmax(s, cmask[None])
    o_cmp = _dot(p.reshape(rows, n_cmp_pad).astype(_BF16), vc_ref[0, 0, 0])

    imp = jnp.dot(jnp.sum(p, axis=0), amat_ref[...], preferred_element_type=_F32,
                  precision=lax.Precision.HIGHEST)
    blk = lax.broadcasted_iota(jnp.int32, (1, LANES), 1)
    cur = tq // SEL_BLOCK
    forced = (blk == 0) | ((blk <= cur) & (blk > cur - N_FORCED_LOCAL))
    imp = jnp.where(forced, imp + FORCE_BONUS, imp)
    imp = jnp.where(blk * SEL_BLOCK <= tq, imp, NEG_INF)
    imp = jnp.where(blk < n_blk, imp, -3e38)
    rank = jnp.zeros((tq_n, LANES), _F32)
    for i in range(n_blk):
        col = imp[:, i:i + 1]
        before = (col > imp) | ((col == imp) & (blk > i))
        rank = rank + jnp.where(before, 1.0, 0.0)
    sel = jnp.where(rank < float(min(N_SELECT, n_blk)), 1.0, 0.0).astype(_BF16)

    m_scr[...] = jnp.full(m_scr.shape, NEG_INF, _F32)
    l_scr[...] = jnp.zeros(l_scr.shape, _F32)
    acc_scr[...] = jnp.zeros(acc_scr.shape, _F32)
    for c in range(seq // SEL_CHUNK):
        @pl.when(c * SEL_CHUNK <= start + tq_n - 1)
        def _(c=c):
            lo = c * SEL_CHUNK
            kpos = lo + lax.broadcasted_iota(jnp.int32, (1, SEL_CHUNK), 1)
            chosen = _dot(sel, emat_ref[:, lo:lo + SEL_CHUNK]) > 0.5
            mask = (chosen & (kpos <= tq))[None]
            sc = _dot_nt(q, ks_ref[0, 0, lo:lo + SEL_CHUNK, :]).reshape(GROUP_SIZE, tq_n, SEL_CHUNK)
            m_old = m_scr[...]
            m_new = jnp.maximum(m_old, jnp.max(jnp.where(mask, sc, NEG_INF), axis=-1, keepdims=True))
            e = jnp.where(mask, jnp.exp(sc - m_new), 0.0)
            alpha = jnp.exp(m_old - m_new)
            l_scr[...] = alpha * l_scr[...] + jnp.sum(e, axis=-1, keepdims=True)
            pv = _dot(e.reshape(rows, SEL_CHUNK).astype(_BF16), vs_ref[0, 0, lo:lo + SEL_CHUNK, :])
            acc_scr[...] = alpha * acc_scr[...] + pv.reshape(GROUP_SIZE, tq_n, HEAD_DIM)
            m_scr[...] = m_new
    l_sel = l_scr[...]
    o_slc = acc_scr[...] / jnp.where(l_sel > 0.0, l_sel, 1.0)

    span = WINDOW + tq_n
    kstart = pl.multiple_of(jnp.maximum(start - WINDOW, 0), tq_n)
    kpos = kstart + lax.broadcasted_iota(jnp.int32, (1, span), 1)
    wmask = ((kpos <= tq) & (tq - kpos < WINDOW))[None]
    sw = _dot_nt(q, kw_ref[0, 0, pl.ds(kstart, span), :]).reshape(GROUP_SIZE, tq_n, span)
    pw = _masked_softmax(sw, wmask)
    o_win = _dot(pw.reshape(rows, span).astype(_BF16), vw_ref[0, 0, pl.ds(kstart, span), :])

    o_cmp = o_cmp.reshape(GROUP_SIZE, tq_n, HEAD_DIM)
    o_win = o_win.reshape(GROUP_SIZE, tq_n, HEAD_DIM)
    gates = gate_ref[0]
    for r in range(GROUP_SIZE):
        g_cmp = gates[:, r:r + 1]
        g_slc = gates[:, GROUP_SIZE + r:GROUP_SIZE + r + 1]
        g_win = gates[:, 2 * GROUP_SIZE + r:2 * GROUP_SIZE + r + 1]
        o = g_cmp * o_cmp[r] + g_slc * o_slc[r] + g_win * o_win[r]
        o_ref[:, r * HEAD_DIM:(r + 1) * HEAD_DIM] = o.astype(o_ref.dtype)


def _nsa(heads, cmp_kv, gates, batch, seq):
    t = heads.shape[2]
    tq = Q_TILE
    nq = seq // tq
    n_chunk = cmp_kv.shape[3]
    n_blk = seq // SEL_BLOCK
    assert n_chunk == LANES and n_blk <= LANES and seq % SEL_CHUNK == 0 and WINDOW % tq == 0
    amat = jnp.asarray(_importance_matrix(n_chunk, n_blk))
    emat = jnp.asarray(_expand_matrix(n_blk, seq), dtype=_BF16)

    def kv_spec(slab):
        return pl.BlockSpec((1, 1, seq, HEAD_DIM), lambda b, g, i: (slab, g, b, 0))

    def cmp_spec(s):
        return pl.BlockSpec((1, 1, 1, n_chunk, HEAD_DIM), lambda b, g, i: (s, b, g, 0, 0))

    return pl.pallas_call(
        _nsa_kernel,
        out_shape=jax.ShapeDtypeStruct((t, Q_COLS), _BF16),
        grid=(batch, N_KV_GROUPS, nq),
        in_specs=[pl.BlockSpec((1, GROUP_SIZE, tq, HEAD_DIM), lambda b, g, i: (g, 0, b * nq + i, 0)),
                  cmp_spec(0), cmp_spec(1),
                  kv_spec(N_Q_SLABS + 2), kv_spec(N_Q_SLABS + 3),
                  kv_spec(N_Q_SLABS + 4), kv_spec(N_Q_SLABS + 5),
                  pl.BlockSpec((1, tq, 3 * GROUP_SIZE), lambda b, g, i: (g, b * nq + i, 0)),
                  pl.BlockSpec((n_chunk, LANES), lambda b, g, i: (0, 0)),
                  pl.BlockSpec((LANES, seq), lambda b, g, i: (0, 0))],
        out_specs=pl.BlockSpec((tq, HEAD_TILE), lambda b, g, i: (b * nq + i, g)),
        scratch_shapes=[pltpu.VMEM((GROUP_SIZE, tq, 1), _F32),
                        pltpu.VMEM((GROUP_SIZE, tq, 1), _F32),
                        pltpu.VMEM((GROUP_SIZE, tq, HEAD_DIM), _F32)],
        compiler_params=_params("parallel", "parallel", "arbitrary"),
        name="nsa",
    )(heads, cmp_kv, cmp_kv, heads, heads, heads, heads, gates, amat, emat)


def _mix_kernel(c_ref, o_ref, h_ref, wpc_ref, wpn_ref, wmc_ref, wmn_ref, bc_ref, bn_ref, out_ref):
    h = h_ref[...]
    g_conv = _sigmoid(_dot(h, wmc_ref[...]) + bc_ref[...])
    g_nsa = _sigmoid(_dot(h, wmn_ref[...]) + bn_ref[...])
    y_conv = _dot(c_ref[...], wpc_ref[...])
    y_nsa = _dot(o_ref[...], wpn_ref[...])
    out_ref[...] = (g_conv * y_conv + g_nsa * y_nsa).astype(out_ref.dtype)


def _mix(c, o, h, wpc, wpn, wm, bm):
    t, d = h.shape
    tm, tn = ROW_TILE, HEAD_TILE
    nb = d // tn
    bm2 = bm.reshape(1, 2 * d)
    return pl.pallas_call(
        _mix_kernel,
        out_shape=jax.ShapeDtypeStruct((t, d), _BF16),
        grid=(t // tm, nb),
        in_specs=[pl.BlockSpec((tm, CONV_CHANNELS), lambda i, j: (i, 0)),
                  pl.BlockSpec((tm, Q_COLS), lambda i, j: (i, 0)),
                  pl.BlockSpec((tm, d), lambda i, j: (i, 0)),
                  pl.BlockSpec((CONV_CHANNELS, tn), lambda i, j: (0, j)),
                  pl.BlockSpec((Q_COLS, tn), lambda i, j: (0, j)),
                  pl.BlockSpec((d, tn), lambda i, j: (0, j)),
                  pl.BlockSpec((d, tn), lambda i, j: (0, j + nb)),
                  pl.BlockSpec((1, tn), lambda i, j: (0, j)),
                  pl.BlockSpec((1, tn), lambda i, j: (0, j + nb))],
        out_specs=pl.BlockSpec((tm, tn), lambda i, j: (i, j)),
        compiler_params=_params("parallel", "arbitrary"),
        name="mix",
    )(c, o, h, wpc, wpn, wm, wm, bm2, bm2)


def _out_router_kernel(mix_ref, x_ref, wout_ref, g_ref, wr_ref, br_ref, x1_ref, h2_ref, route_ref):
    x1 = x_ref[...] + _dot(mix_ref[...], wout_ref[...])
    x1_ref[...] = x1
    ms = jnp.mean(x1 * x1, axis=-1, keepdims=True)
    h2 = x1 * lax.rsqrt(ms + NORM_EPS) * g_ref[...]
    h2_ref[...] = h2
    logits = jnp.dot(h2, wr_ref[...], preferred_element_type=_F32,
                     precision=lax.Precision.HIGHEST) + br_ref[...]
    lane = lax.broadcasted_iota(jnp.int32, (1, LANES), 1)
    lane_f = lane.astype(_F32)
    big = float(LANES)
    gl = jnp.where(lane < N_GROUPS, logits, NEG_INF)
    gmax = jnp.max(gl, axis=-1, keepdims=True)
    p_g = 1.0 / jnp.sum(jnp.where(lane < N_GROUPS, jnp.exp(gl - gmax), 0.0), axis=-1, keepdims=True)
    g_sel = jnp.min(jnp.where(gl == gmax, lane_f, big), axis=-1, keepdims=True)
    lo = N_GROUPS + g_sel * EXPERTS_PER_GROUP
    in_grp = (lane_f >= lo) & (lane_f < lo + EXPERTS_PER_GROUP)
    el = jnp.where(in_grp, logits, NEG_INF)
    v0 = jnp.max(el, axis=-1, keepdims=True)
    i0 = jnp.min(jnp.where(el == v0, lane_f, big), axis=-1, keepdims=True)
    el2 = jnp.where(lane_f == i0, NEG_INF, el)
    v1 = jnp.max(el2, axis=-1, keepdims=True)
    i1 = jnp.min(jnp.where(el2 == v1, lane_f, big), axis=-1, keepdims=True)
    t1 = jnp.exp(v1 - v0)
    w0 = p_g / (1.0 + t1)
    w1 = p_g * t1 / (1.0 + t1)
    route = jnp.where(lane == 0, i0 - N_GROUPS, 0.0)
    route = jnp.where(lane == 1, i1 - N_GROUPS, route)
    route = jnp.where(lane == 2, w0, route)
    route = jnp.where(lane == 3, w1, route)
    route_ref[...] = route


def _out_router(mixed, x, w_out, ffn_g, w_route, b_route):
    t, d = x.shape
    tm = OUT_TILE
    row = pl.BlockSpec((tm, d), lambda i: (i, 0))
    return pl.pallas_call(
        _out_router_kernel,
        out_shape=(jax.ShapeDtypeStruct((t, d), _F32),
                   jax.ShapeDtypeStruct((t, d), _F32),
                   jax.ShapeDtypeStruct((t, LANES), _F32)),
        grid=(t // tm,),
        in_specs=[row, row,
                  pl.BlockSpec((d, d), lambda i: (0, 0)),
                  pl.BlockSpec((1, d), lambda i: (0, 0)),
                  pl.BlockSpec((d, LANES), lambda i: (0, 0)),
                  pl.BlockSpec((1, LANES), lambda i: (0, 0))],
        out_specs=(row, row, pl.BlockSpec((tm, LANES), lambda i: (i, 0))),
        compiler_params=_params("parallel"),
        name="out_router",
    )(mixed, x, w_out, ffn_g.reshape(1, d), w_route, b_route)


def _moe_kernel(be_ref, tok_ref, nb_ref, h2_hbm, w1_ref, w3_ref, w2_ref, y_ref,
                xbuf, w1b, w3b, w2b, sem):
    i = pl.program_id(0)
    nb = nb_ref[0]
    slot = i % 2
    bm = xbuf.shape[1]

    def row_copy(blk, r, sl):
        tok = tok_ref[blk * bm + r]
        return pltpu.make_async_copy(h2_hbm.at[pl.ds(tok, 1), :], xbuf.at[sl, pl.ds(r, 1), :], sem.at[sl])

    def issue(blk, sl):
        def body(r, carry):
            row_copy(blk, r, sl).start()
            return carry
        lax.fori_loop(0, bm, body, 0)

    @pl.when(i == 0)
    def _():
        issue(0, 0)

    @pl.when(i + 1 < nb)
    def _():
        issue(i + 1, 1 - slot)

    @pl.when(i < nb)
    def _():
        def wait_body(r, carry):
            row_copy(i, r, slot).wait()
            return carry
        lax.fori_loop(0, bm, wait_body, 0)

        changed = (i == 0) | (be_ref[i] != be_ref[jnp.maximum(i - 1, 0)])

        @pl.when(changed)
        def _():
            w1b[...] = w1_ref[0].astype(_BF16)
            w3b[...] = w3_ref[0].astype(_BF16)
            w2b[...] = w2_ref[0].astype(_BF16)

        x = xbuf[slot].astype(_BF16)
        a = _dot(x, w1b[...])
        b = _dot(x, w3b[...])
        act = (a * _sigmoid(a) * b).astype(_BF16)
        y_ref[...] = _dot(act, w2b[...])

    @pl.when(i >= nb)
    def _():
        y_ref[...] = jnp.zeros(y_ref.shape, y_ref.dtype)


def _moe(h2, block_e, row_tok, n_used, w1, w3, w2):
    t, d = h2.shape
    bm = MOE_ROWS
    n_blocks = block_e.shape[0]
    ff = w1.shape[2]
    grid_spec = pltpu.PrefetchScalarGridSpec(
        num_scalar_prefetch=3,
        grid=(n_blocks,),
        in_specs=[pl.BlockSpec(memory_space=pl.ANY),
                  pl.BlockSpec((1, d, ff), lambda i, be, tok, nb: (be[i], 0, 0)),
                  pl.BlockSpec((1, d, ff), lambda i, be, tok, nb: (be[i], 0, 0)),
                  pl.BlockSpec((1, ff, d), lambda i, be, tok, nb: (be[i], 0, 0))],
        out_specs=pl.BlockSpec((bm, d), lambda i, be, tok, nb: (i, 0)),
        scratch_shapes=[pltpu.VMEM((2, bm, d), _F32),
                        pltpu.VMEM((d, ff), _BF16),
                        pltpu.VMEM((d, ff), _BF16),
                        pltpu.VMEM((ff, d), _BF16),
                        pltpu.SemaphoreType.DMA((2,))],
    )
    return pl.pallas_call(
        _moe_kernel,
        out_shape=jax.ShapeDtypeStruct((n_blocks * bm, d), _F32),
        grid_spec=grid_spec,
        compiler_params=_params("arbitrary"),
        name="moe_experts",
    )(block_e, row_tok, n_used, h2, w1, w3, w2)


def _final_kernel(pos_ref, y_hbm, x1_ref, route_ref, g_ref, o_ref, ybuf, sem):
    i = pl.program_id(0)
    n = pl.num_programs(0)
    slot = i % 2
    tm = x1_ref.shape[0]

    def row_copy(tile, r, k, sl):
        p = pos_ref[(tile * tm + r) * TOP_K + k]
        return pltpu.make_async_copy(y_hbm.at[pl.ds(p, 1), :], ybuf.at[sl, k, pl.ds(r, 1), :], sem.at[sl])

    def issue(tile, sl):
        def body(r, carry):
            for k in range(TOP_K):
                row_copy(tile, r, k, sl).start()
            return carry
        lax.fori_loop(0, tm, body, 0)

    @pl.when(i == 0)
    def _():
        issue(0, 0)

    @pl.when(i + 1 < n)
    def _():
        issue(i + 1, 1 - slot)

    def wait_body(r, carry):
        for k in range(TOP_K):
            row_copy(i, r, k, slot).wait()
        return carry
    lax.fori_loop(0, tm, wait_body, 0)

    route = route_ref[...]
    x = x1_ref[...] + route[:, 2:3] * ybuf[slot, 0] + route[:, 3:4] * ybuf[slot, 1]
    ms = jnp.mean(x * x, axis=-1, keepdims=True)
    o_ref[...] = x * lax.rsqrt(ms + NORM_EPS) * g_ref[...]


def _final(pos, y_rows, x1, route, g):
    t, d = x1.shape
    tm = FINAL_TILE
    grid_spec = pltpu.PrefetchScalarGridSpec(
        num_scalar_prefetch=1,
        grid=(t // tm,),
        in_specs=[pl.BlockSpec(memory_space=pl.ANY),
                  pl.BlockSpec((tm, d), lambda i, pos: (i, 0)),
                  pl.BlockSpec((tm, LANES), lambda i, pos: (i, 0)),
                  pl.BlockSpec((1, d), lambda i, pos: (0, 0))],
        out_specs=pl.BlockSpec((tm, d), lambda i, pos: (i, 0)),
        scratch_shapes=[pltpu.VMEM((2, TOP_K, tm, d), _F32),
                        pltpu.SemaphoreType.DMA((2,))],
    )
    return pl.pallas_call(
        _final_kernel,
        out_shape=jax.ShapeDtypeStruct((t, d), _F32),
        grid_spec=grid_spec,
        compiler_params=_params("arbitrary"),
        name="combine_norm",
    )(pos, y_rows, x1, route, g.reshape(1, d))


def _routing_tables(expert, n_tokens):
    bm = MOE_ROWS
    n_assign = n_tokens * TOP_K
    n_blocks = -(-(n_assign + N_EXPERTS * (bm - 1)) // bm)
    flat_e = expert.reshape(-1)
    order = jnp.argsort(flat_e).astype(jnp.int32)
    inv = jnp.argsort(order).astype(jnp.int32)
    counts = jnp.sum((flat_e[:, None] == jnp.arange(N_EXPERTS, dtype=jnp.int32)[None, :]).astype(jnp.int32), axis=0)
    padded = (counts + bm - 1) // bm * bm
    start = jnp.cumsum(counts) - counts
    pend = jnp.cumsum(padded)
    pstart = pend - padded
    pos = pstart[flat_e] + inv - start[flat_e]
    blocks = jnp.arange(n_blocks, dtype=jnp.int32)
    block_e = jnp.minimum(jnp.searchsorted(pend, blocks * bm, side='right'), N_EXPERTS - 1).astype(jnp.int32)
    rows = jnp.arange(n_blocks * bm, dtype=jnp.int32)
    row_e = jnp.repeat(block_e, bm)
    r = rows - pstart[row_e]
    valid = (r >= 0) & (r < counts[row_e])
    src = jnp.clip(start[row_e] + r, 0, n_assign - 1)
    row_tok = jnp.where(valid, order[src] // TOP_K, 0).astype(jnp.int32)
    n_used = (pend[-1] // bm).astype(jnp.int32).reshape(1)
    return block_e, row_tok, n_used, pos.astype(jnp.int32)


def kernel(x, attn_norm, w_in, conv_dw, conv_dw_b, conv_ln_g, conv_ln_b, cmp_pos_k, cmp_k_w1, cmp_k_w2, cmp_pos_v, cmp_v_w1, cmp_v_w2, w_proj_conv, w_proj_nsa, w_merge, b_merge, w_out, ffn_norm, w_grp, b_grp, w_exp, b_exp, exp_w1, exp_w3, exp_w2, final_norm):
    batch, seq, d = x.shape
    t = batch * seq
    assert w_in.shape[0] == 1, "the block is specified with a single layer"
    xf = x.reshape(t, d)

    for l in range(1):
        w_in_b = w_in[l].astype(_BF16)
        w_gate = w_in[l][:, 2 * CONV_CHANNELS + HEAD_COLS:].reshape(d, 3, N_KV_GROUPS, GROUP_SIZE)
        w_gate = w_gate.transpose(0, 2, 1, 3).reshape(d, GATE_COLS).astype(_BF16)

        h = _rmsnorm(xf, attn_norm[l], _BF16)
        glu = _glu_proj(h, w_in_b)
        heads = _heads_proj(h, w_in_b, seq)
        gates = _gate_proj(h, w_gate).reshape(t, N_KV_GROUPS, 3 * GROUP_SIZE).transpose(1, 0, 2)

        half = CMP_STRIDE * HEAD_DIM
        pos = jnp.stack([cmp_pos_k[l], cmp_pos_v[l]]).reshape(2, 2, 1, half)
        cw1 = jnp.stack([cmp_k_w1[l], cmp_v_w1[l]]).astype(_BF16)
        cw2 = jnp.stack([cmp_k_w2[l], cmp_v_w2[l]]).astype(_BF16)
        cmp_kv = _compress(heads, pos, cw1, cw2, batch, seq)

        c = _conv(glu, conv_dw[l], conv_dw_b[l], conv_ln_g[l], conv_ln_b[l], batch, seq)
        o = _nsa(heads, cmp_kv, gates, batch, seq)
        mixed = _mix(c, o, h, w_proj_conv[l].astype(_BF16), w_proj_nsa[l].astype(_BF16),
                     w_merge[l].astype(_BF16), b_merge[l])

        w_route = jnp.concatenate(
            [w_grp[l], w_exp[l], jnp.zeros((d, LANES - N_GROUPS - N_EXPERTS), _F32)], axis=1)
        b_route = jnp.concatenate(
            [b_grp[l], b_exp[l], jnp.zeros((LANES - N_GROUPS - N_EXPERTS,), _F32)]).reshape(1, LANES)
        x1, h2, route = _out_router(mixed, xf, w_out[l].astype(_BF16), ffn_norm[l], w_route, b_route)

        expert = route[:, :TOP_K].astype(jnp.int32)
        block_e, row_tok, n_used, pos_rows = _routing_tables(expert, t)
        y_rows = _moe(h2, block_e, row_tok, n_used, exp_w1[l], exp_w3[l], exp_w2[l])

        xf = _final(pos_rows, y_rows, x1, route, final_norm)
    return xf.reshape(batch, seq, d)
```

```python
import numpy as np
import jax
import jax.numpy as jnp
from jax import lax
from jax.experimental import pallas as pl
from jax.experimental.pallas import tpu as pltpu

D_MODEL = 2048
CONV_CHANNELS = D_MODEL // 2
DW_TAPS = 31
N_HEADS = 16
HEAD_DIM = 128
N_KV_GROUPS = 4
GROUP_SIZE = N_HEADS // N_KV_GROUPS
ROT_DIM = HEAD_DIM // 4
ROPE_THETA = 500000.0
CMP_BLOCK = 32
CMP_STRIDE = 16
CMP_HIDDEN = 256
SEL_BLOCK = 64
N_SELECT = 16
N_FORCED_LOCAL = 2
WINDOW = 512
N_GROUPS = 8
EXPERTS_PER_GROUP = 8
N_EXPERTS = N_GROUPS * EXPERTS_PER_GROUP
TOP_K = 2
EXPERT_FF = 512
Q_COLS = N_HEADS * HEAD_DIM
KV_COLS = N_KV_GROUPS * HEAD_DIM
HEAD_COLS = Q_COLS + 6 * KV_COLS
GATE_COLS = 3 * N_HEADS
NORM_EPS = 1e-6
NEG_INF = -1e30
FORCE_BONUS = 1e4
LOG2_E = 1.4426950408889634

LANES = 128
VMEM_LIMIT_BYTES = 56 * 1024 * 1024

ROW_TILE = 512
HEAD_TILE = GROUP_SIZE * HEAD_DIM
Q_TILE = 128
SEL_CHUNK = 512
CONV_TILE = 256
CONV_HALO = 32
OUT_TILE = 256
MOE_ROWS = 256
FINAL_TILE = 128
DMA_UNROLL = 8

_BF16 = jnp.bfloat16
_F32 = jnp.float32


def _params(*sem, **kw):
    return pltpu.CompilerParams(dimension_semantics=sem, vmem_limit_bytes=VMEM_LIMIT_BYTES, **kw)


def _dot(a, b):
    return jnp.dot(a, b, preferred_element_type=_F32)


def _dot_nt(a, b):
    return lax.dot_general(a, b, (((1,), (1,)), ((), ())), preferred_element_type=_F32)


def _sigmoid(x):
    return 1.0 / (1.0 + jnp.exp(-x))


def _rmsnorm_kernel(x_ref, g_ref, o_ref):
    x = x_ref[...]
    ms = jnp.mean(x * x, axis=-1, keepdims=True)
    o_ref[...] = (x * lax.rsqrt(ms + NORM_EPS) * g_ref[...]).astype(o_ref.dtype)


def _rmsnorm(x, g, out_dtype):
    t, d = x.shape
    return pl.pallas_call(
        _rmsnorm_kernel,
        out_shape=jax.ShapeDtypeStruct((t, d), out_dtype),
        grid=(t // ROW_TILE,),
        in_specs=[pl.BlockSpec((ROW_TILE, d), lambda i: (i, 0)),
                  pl.BlockSpec((1, d), lambda i: (0, 0))],
        out_specs=pl.BlockSpec((ROW_TILE, d), lambda i: (i, 0)),
        compiler_params=_params("parallel"),
        name="rmsnorm",
    )(x, g.reshape(1, d))


def _glu_kernel(h_ref, wa_ref, wb_ref, o_ref):
    h = h_ref[...]
    a = _dot(h, wa_ref[...])
    b = _dot(h, wb_ref[...])
    o_ref[...] = a * _sigmoid(b)


def _glu_proj(h, w_in_bf16):
    t, d = h.shape
    tn = HEAD_TILE
    nb = CONV_CHANNELS // tn
    return pl.pallas_call(
        _glu_kernel,
        out_shape=jax.ShapeDtypeStruct((t, CONV_CHANNELS), _F32),
        grid=(t // ROW_TILE, nb),
        in_specs=[pl.BlockSpec((ROW_TILE, d), lambda i, j: (i, 0)),
                  pl.BlockSpec((d, tn), lambda i, j: (0, j)),
                  pl.BlockSpec((d, tn), lambda i, j: (0, j + nb))],
        out_specs=pl.BlockSpec((ROW_TILE, tn), lambda i, j: (i, j)),
        compiler_params=_params("parallel", "arbitrary"),
        name="glu_proj",
    )(h, w_in_bf16, w_in_bf16)


N_Q_SLABS = Q_COLS // HEAD_TILE
N_HEAD_SLABS = N_Q_SLABS + 4
Q_COL_BLOCK = 2 * CONV_CHANNELS // HEAD_TILE
CMP_COL_BLOCK = Q_COL_BLOCK + N_Q_SLABS
KV_COL_BLOCK = CMP_COL_BLOCK + 2


def _rope_tables(seq):
    pos = jnp.arange(seq, dtype=_F32)
    inv = ROPE_THETA ** (-jnp.arange(0, ROT_DIM, 2, dtype=_F32) / ROT_DIM)
    ang = pos[:, None] * inv[None, :]
    cos, sin = jnp.cos(ang), jnp.sin(ang)
    half = ROT_DIM // 2
    ones = jnp.ones((seq, HEAD_DIM - ROT_DIM), _F32)
    zeros_h = jnp.zeros((seq, half), _F32)
    zeros_r = jnp.zeros((seq, HEAD_DIM - ROT_DIM), _F32)
    c = jnp.concatenate([cos, cos, ones], axis=-1)
    s_lo = jnp.concatenate([zeros_h, sin, zeros_r], axis=-1)
    s_hi = jnp.concatenate([-sin, zeros_h, zeros_r], axis=-1)
    return c, s_lo, s_hi


def _rotate_head(xs, c, slo, shi):
    half = ROT_DIM // 2
    return xs * c + pltpu.roll(xs, half, 1) * slo + pltpu.roll(xs, HEAD_DIM - half, 1) * shi


def _heads_kernel(h_ref, w_ref, c_ref, slo_ref, shi_ref, o_ref):
    j = pl.program_id(1)
    acc = _dot(h_ref[...], w_ref[...])
    is_q = j < N_Q_SLABS
    rotate = is_q | (j % 2 == 0)

    @pl.when(rotate)
    def _():
        scale = jnp.where(is_q, HEAD_DIM ** -0.5 * LOG2_E, 1.0).astype(_F32)
        c, slo, shi = c_ref[...], slo_ref[...], shi_ref[...]
        for hd in range(GROUP_SIZE):
            r = _rotate_head(acc[:, hd * HEAD_DIM:(hd + 1) * HEAD_DIM], c, slo, shi)
            o_ref[0, hd] = (r * scale).astype(o_ref.dtype)

    @pl.when(jnp.logical_not(rotate))
    def _():
        for hd in range(GROUP_SIZE):
            o_ref[0, hd] = acc[:, hd * HEAD_DIM:(hd + 1) * HEAD_DIM].astype(o_ref.dtype)


def _heads_proj(h, w_in_bf16, tables, seq):
    t, d = h.shape
    tm = ROW_TILE
    spt = seq // tm
    tab = pl.BlockSpec((tm, HEAD_DIM), lambda i, j: (i % spt, 0))

    def w_block(i, j):
        return (0, jnp.where(j < N_Q_SLABS, j + Q_COL_BLOCK, j - N_Q_SLABS + KV_COL_BLOCK))

    return pl.pallas_call(
        _heads_kernel,
        out_shape=jax.ShapeDtypeStruct((N_HEAD_SLABS, GROUP_SIZE, t, HEAD_DIM), _BF16),
        grid=(t // tm, N_HEAD_SLABS),
        in_specs=[pl.BlockSpec((tm, d), lambda i, j: (i, 0)),
                  pl.BlockSpec((d, HEAD_TILE), w_block),
                  tab, tab, tab],
        out_specs=pl.BlockSpec((1, GROUP_SIZE, tm, HEAD_DIM), lambda i, j: (j, 0, i, 0)),
        compiler_params=_params("parallel", "arbitrary"),
        name="heads_proj",
    )(h, w_in_bf16, *tables)


def _cmp_proj_kernel(h_ref, w_ref, c_ref, slo_ref, shi_ref, o_ref, stage_ref):
    j = pl.program_id(1)
    acc = _dot(h_ref[...], w_ref[...])

    @pl.when(j == 0)
    def _():
        c, slo, shi = c_ref[...], slo_ref[...], shi_ref[...]
        for g in range(N_KV_GROUPS):
            stage_ref[g] = _rotate_head(acc[:, g * HEAD_DIM:(g + 1) * HEAD_DIM], c, slo, shi)

    @pl.when(j != 0)
    def _():
        for g in range(N_KV_GROUPS):
            stage_ref[g] = acc[:, g * HEAD_DIM:(g + 1) * HEAD_DIM]

    n_chunk = stage_ref.shape[1] // CMP_STRIDE
    for g in range(N_KV_GROUPS):
        for l in range(CMP_STRIDE):
            rows = stage_ref[g, pl.ds(l, n_chunk, stride=CMP_STRIDE), :]
            o_ref[0, g, :, l * HEAD_DIM:(l + 1) * HEAD_DIM] = rows.astype(o_ref.dtype)


def _cmp_proj(h, w_in_bf16, tables, seq):
    t, d = h.shape
    tm = ROW_TILE
    spt = seq // tm
    tab = pl.BlockSpec((tm, HEAD_DIM), lambda i, j: (i % spt, 0))
    return pl.pallas_call(
        _cmp_proj_kernel,
        out_shape=jax.ShapeDtypeStruct((2, N_KV_GROUPS, t // CMP_STRIDE, CMP_STRIDE * HEAD_DIM), _BF16),
        grid=(t // tm, 2),
        in_specs=[pl.BlockSpec((tm, d), lambda i, j: (i, 0)),
                  pl.BlockSpec((d, HEAD_TILE), lambda i, j: (0, j + CMP_COL_BLOCK)),
                  tab, tab, tab],
        out_specs=pl.BlockSpec((1, N_KV_GROUPS, tm // CMP_STRIDE, CMP_STRIDE * HEAD_DIM),
                               lambda i, j: (j, 0, i, 0)),
        scratch_shapes=[pltpu.VMEM((N_KV_GROUPS, tm, HEAD_DIM), _F32)],
        compiler_params=_params("parallel", "arbitrary"),
        name="cmp_proj",
    )(h, w_in_bf16, *tables)


def _gate_kernel(h_ref, w_ref, o_ref):
    o_ref[...] = _sigmoid(_dot(h_ref[...], w_ref[...]))


def _gate_proj(h, w_gate_bf16):
    t, d = h.shape
    n = w_gate_bf16.shape[1]
    return pl.pallas_call(
        _gate_kernel,
        out_shape=jax.ShapeDtypeStruct((t, n), _F32),
        grid=(t // ROW_TILE,),
        in_specs=[pl.BlockSpec((ROW_TILE, d), lambda i: (i, 0)),
                  pl.BlockSpec((d, n), lambda i: (0, 0))],
        out_specs=pl.BlockSpec((ROW_TILE, n), lambda i: (i, 0)),
        compiler_params=_params("parallel"),
        name="gate_proj",
    )(h, w_gate_bf16)


def _compress_kernel(kv_ref, pos_ref, w1_ref, w2_ref, o_ref):
    half = CMP_STRIDE * HEAD_DIM
    c = kv_ref[0, 0].astype(_F32)
    n_chunk = c.shape[0]
    top = (c + pos_ref[0, 0]).astype(_BF16)
    bot = (c + pos_ref[0, 1]).astype(_BF16)
    u = _dot(top, w1_ref[0, :half, :])
    v = _dot(bot, w1_ref[0, half:, :])
    hidden = u + pltpu.roll(v, n_chunk - 1, 0)
    act = jax.nn.gelu(hidden).astype(_BF16)
    o_ref[0, 0, 0] = _dot(act, w2_ref[0]).astype(o_ref.dtype)


def _compress(cmp_flat, pos, w1, w2, batch, seq):
    n_chunk = seq // CMP_STRIDE
    return pl.pallas_call(
        _compress_kernel,
        out_shape=jax.ShapeDtypeStruct((2, batch, N_KV_GROUPS, n_chunk, HEAD_DIM), _BF16),
        grid=(2, batch, N_KV_GROUPS),
        in_specs=[pl.BlockSpec((1, 1, n_chunk, CMP_STRIDE * HEAD_DIM), lambda s, b, g: (s, g, b, 0)),
                  pl.BlockSpec((1, 2, 1, CMP_STRIDE * HEAD_DIM), lambda s, b, g: (s, 0, 0, 0)),
                  pl.BlockSpec((1, CMP_BLOCK * HEAD_DIM, CMP_HIDDEN), lambda s, b, g: (s, 0, 0)),
                  pl.BlockSpec((1, CMP_HIDDEN, HEAD_DIM), lambda s, b, g: (s, 0, 0))],
        out_specs=pl.BlockSpec((1, 1, 1, n_chunk, HEAD_DIM), lambda s, b, g: (s, b, g, 0, 0)),
        compiler_params=_params("parallel", "parallel", "parallel"),
        name="compress",
    )(cmp_flat, pos, w1, w2)


def _conv_kernel(cur_ref, halo_ref, dw_ref, dwb_ref, g_ref, b_ref, o_ref, ext_ref):
    i = pl.program_id(1)
    ts = cur_ref.shape[0]
    ext_ref[:CONV_HALO, :] = jnp.where(i > 0, halo_ref[...], 0.0)
    ext_ref[CONV_HALO:, :] = cur_ref[...]
    off = CONV_HALO - (DW_TAPS - 1)
    y = ext_ref[off:off + ts, :] * dw_ref[0:1, :]
    for k in range(1, DW_TAPS):
        y = y + ext_ref[off + k:off + k + ts, :] * dw_ref[k:k + 1, :]
    y = y + dwb_ref[...]
    mu = jnp.mean(y, axis=-1, keepdims=True)
    yc = y - mu
    var = jnp.mean(yc * yc, axis=-1, keepdims=True)
    z = yc * lax.rsqrt(var + NORM_EPS) * g_ref[...] + b_ref[...]
    o_ref[...] = (z * _sigmoid(z)).astype(o_ref.dtype)


def _conv(glu, dw, dw_b, ln_g, ln_b, batch, seq):
    t, c = glu.shape
    ts = CONV_TILE
    nt = seq // ts
    hb = ts // CONV_HALO
    dw_pad = jnp.concatenate([dw, jnp.zeros((CONV_HALO - DW_TAPS, c), _F32)], axis=0)
    vec = pl.BlockSpec((1, c), lambda b, i: (0, 0))
    return pl.pallas_call(
        _conv_kernel,
        out_shape=jax.ShapeDtypeStruct((t, c), _BF16),
        grid=(batch, nt),
        in_specs=[pl.BlockSpec((ts, c), lambda b, i: (b * nt + i, 0)),
                  pl.BlockSpec((CONV_HALO, c), lambda b, i: (jnp.maximum((b * nt + i) * hb - 1, 0), 0)),
                  pl.BlockSpec((CONV_HALO, c), lambda b, i: (0, 0)),
                  vec, vec, vec],
        out_specs=pl.BlockSpec((ts, c), lambda b, i: (b * nt + i, 0)),
        scratch_shapes=[pltpu.VMEM((ts + CONV_HALO, c), _F32)],
        compiler_params=_params("parallel", "arbitrary"),
        name="conv",
    )(glu, glu, dw_pad, dw_b.reshape(1, c), ln_g.reshape(1, c), ln_b.reshape(1, c))


def _importance_matrix(n_cmp_pad, n_blk):
    r_sel, r_cmp = SEL_BLOCK // CMP_STRIDE, CMP_BLOCK // CMP_STRIDE
    a = np.zeros((n_cmp_pad, LANES), np.float32)
    for j in range(n_blk):
        for m in range(r_sel):
            for n in range(r_cmp):
                c = r_sel * j + m - n
                if 0 <= c < n_cmp_pad - 1:
                    a[c, j] += 1.0
    return a


def _expand_matrix(n_blk, seq):
    e = np.zeros((LANES, seq), np.float32)
    for j in range(n_blk):
        e[j, j * SEL_BLOCK:(j + 1) * SEL_BLOCK] = 1.0
    return e


def _nsa_kernel(q_ref, kc_ref, vc_ref, ks_ref, vs_ref, kw_ref, vw_ref, gate_ref, amat_ref, emat_ref,
                o_ref, m_scr, l_scr, acc_scr):
    tq_n = q_ref.shape[2]
    seq = ks_ref.shape[2]
    n_cmp_pad = kc_ref.shape[3]
    n_blk = seq // SEL_BLOCK
    rows = GROUP_SIZE * tq_n
    start = pl.program_id(2) * tq_n
    q = q_ref[0].reshape(rows, HEAD_DIM)
    tq = start + lax.broadcasted_iota(jnp.int32, (tq_n, 1), 0)

    cidx = lax.broadcasted_iota(jnp.int32, (1, n_cmp_pad), 1)
    cvalid = (cidx * CMP_STRIDE + (CMP_BLOCK - 1) <= tq) & (cidx < n_cmp_pad - 1)
    cbias = jnp.where(cvalid, 0.0, NEG_INF)
    s = _dot_nt(q, kc_ref[0, 0, 0]).reshape(GROUP_SIZE, tq_n, n_cmp_pad) + cbias[None]
    e = jnp.exp2(s - jnp.max(s, axis=-1, keepdims=True))
    inv_l = jnp.where((tq >= CMP_BLOCK - 1)[None], 1.0 / jnp.sum(e, axis=-1, keepdims=True), 0.0)
    p = e * inv_l
    o_cmp = _dot(p.reshape(rows, n_cmp_pad).astype(_BF16), vc_ref[0, 0, 0])

    imp = jnp.dot(jnp.sum(p, axis=0), amat_ref[...], preferred_element_type=_F32,
                  precision=lax.Precision.HIGHEST)
    blk = lax.broadcasted_iota(jnp.int32, (1, LANES), 1)
    cur = tq // SEL_BLOCK
    forced = (blk == 0) | ((blk <= cur) & (blk > cur - N_FORCED_LOCAL))
    imp = jnp.where(forced, imp + FORCE_BONUS, imp)
    imp = jnp.where(blk * SEL_BLOCK <= tq, imp, NEG_INF)
    imp = jnp.where(blk < n_blk, imp, -3e38)
    rank = jnp.zeros((tq_n, LANES), _F32)
    for i in range(n_blk):
        col = imp[:, i:i + 1]
        before = (col > imp) | ((col == imp) & (blk > i))
        rank = rank + jnp.where(before, 1.0, 0.0)
    sel = jnp.where(rank < float(min(N_SELECT, n_blk)), 1.0, 0.0).astype(_BF16)

    m_scr[...] = jnp.full(m_scr.shape, NEG_INF, _F32)
    l_scr[...] = jnp.zeros(l_scr.shape, _F32)
    acc_scr[...] = jnp.zeros(acc_scr.shape, _F32)
    for c in range(seq // SEL_CHUNK):
        @pl.when(c * SEL_CHUNK <= start + tq_n - 1)
        def _(c=c):
            lo = c * SEL_CHUNK
            kpos = lo + lax.broadcasted_iota(jnp.int32, (1, SEL_CHUNK), 1)
            chosen = _dot(sel, emat_ref[:, lo:lo + SEL_CHUNK]) > 0.5
            bias = jnp.where(chosen & (kpos <= tq), 0.0, NEG_INF)
            sc = _dot_nt(q, ks_ref[0, 0, lo:lo + SEL_CHUNK, :]).reshape(GROUP_SIZE, tq_n, SEL_CHUNK) + bias[None]
            m_old = m_scr[...]
            m_new = jnp.maximum(m_old, jnp.max(sc, axis=-1, keepdims=True))
            e_c = jnp.exp2(sc - m_new)
            alpha = jnp.exp2(m_old - m_new)
            l_scr[...] = alpha * l_scr[...] + jnp.sum(e_c, axis=-1, keepdims=True)
            pv = _dot(e_c.reshape(rows, SEL_CHUNK).astype(_BF16), vs_ref[0, 0, lo:lo + SEL_CHUNK, :])
            acc_scr[...] = alpha * acc_scr[...] + pv.reshape(GROUP_SIZE, tq_n, HEAD_DIM)
            m_scr[...] = m_new

    span = WINDOW + tq_n
    kstart = pl.multiple_of(jnp.maximum(start - WINDOW, 0), tq_n)
    kpos = kstart + lax.broadcasted_iota(jnp.int32, (1, span), 1)
    wbias = jnp.where((kpos <= tq) & (tq - kpos < WINDOW), 0.0, NEG_INF)
    sw = _dot_nt(q, kw_ref[0, 0, pl.ds(kstart, span), :]).reshape(GROUP_SIZE, tq_n, span) + wbias[None]
    e_w = jnp.exp2(sw - jnp.max(sw, axis=-1, keepdims=True))
    l_win = jnp.sum(e_w, axis=-1, keepdims=True)
    o_win = _dot(e_w.reshape(rows, span).astype(_BF16), vw_ref[0, 0, pl.ds(kstart, span), :])

    o_cmp = o_cmp.reshape(GROUP_SIZE, tq_n, HEAD_DIM)
    o_win = o_win.reshape(GROUP_SIZE, tq_n, HEAD_DIM)
    gates = gate_ref[0]
    for r in range(GROUP_SIZE):
        g_cmp = gates[:, r:r + 1]
        g_slc = gates[:, GROUP_SIZE + r:GROUP_SIZE + r + 1] / l_scr[r]
        g_win = gates[:, 2 * GROUP_SIZE + r:2 * GROUP_SIZE + r + 1] / l_win[r]
        o = g_cmp * o_cmp[r] + g_slc * acc_scr[r] + g_win * o_win[r]
        o_ref[:, r * HEAD_DIM:(r + 1) * HEAD_DIM] = o.astype(o_ref.dtype)


def _nsa(heads, cmp_kv, gates, batch, seq):
    t = heads.shape[2]
    tq = Q_TILE
    nq = seq // tq
    n_chunk = cmp_kv.shape[3]
    n_blk = seq // SEL_BLOCK
    assert n_chunk == LANES and n_blk <= LANES and seq % SEL_CHUNK == 0 and WINDOW % tq == 0
    amat = jnp.asarray(_importance_matrix(n_chunk, n_blk))
    emat = jnp.asarray(_expand_matrix(n_blk, seq), dtype=_BF16)

    def kv_spec(slab):
        return pl.BlockSpec((1, 1, seq, HEAD_DIM), lambda b, g, i: (slab, g, b, 0))

    def cmp_spec(s):
        return pl.BlockSpec((1, 1, 1, n_chunk, HEAD_DIM), lambda b, g, i: (s, b, g, 0, 0))

    return pl.pallas_call(
        _nsa_kernel,
        out_shape=jax.ShapeDtypeStruct((t, Q_COLS), _BF16),
        grid=(batch, N_KV_GROUPS, nq),
        in_specs=[pl.BlockSpec((1, GROUP_SIZE, tq, HEAD_DIM), lambda b, g, i: (g, 0, b * nq + i, 0)),
                  cmp_spec(0), cmp_spec(1),
                  kv_spec(N_Q_SLABS), kv_spec(N_Q_SLABS + 1),
                  kv_spec(N_Q_SLABS + 2), kv_spec(N_Q_SLABS + 3),
                  pl.BlockSpec((1, tq, 3 * GROUP_SIZE), lambda b, g, i: (g, b * nq + i, 0)),
                  pl.BlockSpec((n_chunk, LANES), lambda b, g, i: (0, 0)),
                  pl.BlockSpec((LANES, seq), lambda b, g, i: (0, 0))],
        out_specs=pl.BlockSpec((tq, HEAD_TILE), lambda b, g, i: (b * nq + i, g)),
        scratch_shapes=[pltpu.VMEM((GROUP_SIZE, tq, 1), _F32),
                        pltpu.VMEM((GROUP_SIZE, tq, 1), _F32),
                        pltpu.VMEM((GROUP_SIZE, tq, HEAD_DIM), _F32)],
        compiler_params=_params("parallel", "parallel", "arbitrary"),
        name="nsa",
    )(heads, cmp_kv, cmp_kv, heads, heads, heads, heads, gates, amat, emat)


def _mix_kernel(c_ref, o_ref, h_ref, wpc_ref, wpn_ref, wmc_ref, wmn_ref, bc_ref, bn_ref, out_ref):
    h = h_ref[...]
    g_conv = _sigmoid(_dot(h, wmc_ref[...]) + bc_ref[...])
    g_nsa = _sigmoid(_dot(h, wmn_ref[...]) + bn_ref[...])
    y_conv = _dot(c_ref[...], wpc_ref[...])
    y_nsa = _dot(o_ref[...], wpn_ref[...])
    out_ref[...] = (g_conv * y_conv + g_nsa * y_nsa).astype(out_ref.dtype)


def _mix(c, o, h, wpc, wpn, wm, bm):
    t, d = h.shape
    tm, tn = ROW_TILE, HEAD_TILE
    nb = d // tn
    bm2 = bm.reshape(1, 2 * d)
    return pl.pallas_call(
        _mix_kernel,
        out_shape=jax.ShapeDtypeStruct((t, d), _BF16),
        grid=(t // tm, nb),
        in_specs=[pl.BlockSpec((tm, CONV_CHANNELS), lambda i, j: (i, 0)),
                  pl.BlockSpec((tm, Q_COLS), lambda i, j: (i, 0)),
                  pl.BlockSpec((tm, d), lambda i, j: (i, 0)),
                  pl.BlockSpec((CONV_CHANNELS, tn), lambda i, j: (0, j)),
                  pl.BlockSpec((Q_COLS, tn), lambda i, j: (0, j)),
                  pl.BlockSpec((d, tn), lambda i, j: (0, j)),
                  pl.BlockSpec((d, tn), lambda i, j: (0, j + nb)),
                  pl.BlockSpec((1, tn), lambda i, j: (0, j)),
                  pl.BlockSpec((1, tn), lambda i, j: (0, j + nb))],
        out_specs=pl.BlockSpec((tm, tn), lambda i, j: (i, j)),
        compiler_params=_params("parallel", "arbitrary"),
        name="mix",
    )(c, o, h, wpc, wpn, wm, wm, bm2, bm2)


def _out_router_kernel(mix_ref, x_ref, wout_ref, g_ref, wr_ref, br_ref, x1_ref, h2_ref, route_ref):
    x1 = x_ref[...] + _dot(mix_ref[...], wout_ref[...])
    x1_ref[...] = x1
    ms = jnp.mean(x1 * x1, axis=-1, keepdims=True)
    h2 = x1 * lax.rsqrt(ms + NORM_EPS) * g_ref[...]
    h2_ref[...] = h2
    logits = jnp.dot(h2, wr_ref[...], preferred_element_type=_F32,
                     precision=lax.Precision.HIGHEST) + br_ref[...]
    lane = lax.broadcasted_iota(jnp.int32, (1, LANES), 1)
    lane_f = lane.astype(_F32)
    big = float(LANES)
    gl = jnp.where(lane < N_GROUPS, logits, NEG_INF)
    gmax = jnp.max(gl, axis=-1, keepdims=True)
    p_g = 1.0 / jnp.sum(jnp.where(lane < N_GROUPS, jnp.exp(gl - gmax), 0.0), axis=-1, keepdims=True)
    g_sel = jnp.min(jnp.where(gl == gmax, lane_f, big), axis=-1, keepdims=True)
    lo = N_GROUPS + g_sel * EXPERTS_PER_GROUP
    in_grp = (lane_f >= lo) & (lane_f < lo + EXPERTS_PER_GROUP)
    el = jnp.where(in_grp, logits, NEG_INF)
    v0 = jnp.max(el, axis=-1, keepdims=True)
    i0 = jnp.min(jnp.where(el == v0, lane_f, big), axis=-1, keepdims=True)
    el2 = jnp.where(lane_f == i0, NEG_INF, el)
    v1 = jnp.max(el2, axis=-1, keepdims=True)
    i1 = jnp.min(jnp.where(el2 == v1, lane_f, big), axis=-1, keepdims=True)
    t1 = jnp.exp(v1 - v0)
    w0 = p_g / (1.0 + t1)
    w1 = p_g * t1 / (1.0 + t1)
    route = jnp.where(lane == 0, i0 - N_GROUPS, 0.0)
    route = jnp.where(lane == 1, i1 - N_GROUPS, route)
    route = jnp.where(lane == 2, w0, route)
    route = jnp.where(lane == 3, w1, route)
    route_ref[...] = route


def _out_router(mixed, x, w_out, ffn_g, w_route, b_route):
    t, d = x.shape
    tm = OUT_TILE
    row = pl.BlockSpec((tm, d), lambda i: (i, 0))
    return pl.pallas_call(
        _out_router_kernel,
        out_shape=(jax.ShapeDtypeStruct((t, d), _F32),
                   jax.ShapeDtypeStruct((t, d), _F32),
                   jax.ShapeDtypeStruct((t, LANES), _F32)),
        grid=(t // tm,),
        in_specs=[row, row,
                  pl.BlockSpec((d, d), lambda i: (0, 0)),
                  pl.BlockSpec((1, d), lambda i: (0, 0)),
                  pl.BlockSpec((d, LANES), lambda i: (0, 0)),
                  pl.BlockSpec((1, LANES), lambda i: (0, 0))],
        out_specs=(row, row, pl.BlockSpec((tm, LANES), lambda i: (i, 0))),
        compiler_params=_params("parallel"),
        name="out_router",
    )(mixed, x, w_out, ffn_g.reshape(1, d), w_route, b_route)


def _moe_kernel(be_ref, tok_ref, nb_ref, h2_hbm, w1_ref, w3_ref, w2_ref, y_ref,
                xbuf, w1b, w3b, w2b, sem):
    i = pl.program_id(0)
    nb = nb_ref[0]
    slot = i % 2
    bm = xbuf.shape[1]

    def issue(blk, sl):
        def body(r, carry):
            tok = tok_ref[blk * bm + r]
            pltpu.make_async_copy(h2_hbm.at[pl.ds(tok, 1), :], xbuf.at[sl, pl.ds(r, 1), :], sem.at[sl]).start()
            return carry
        lax.fori_loop(0, bm, body, 0, unroll=DMA_UNROLL)

    @pl.when(i == 0)
    def _():
        issue(0, 0)

    @pl.when(i + 1 < nb)
    def _():
        issue(i + 1, 1 - slot)

    @pl.when(i < nb)
    def _():
        pltpu.make_async_copy(h2_hbm.at[pl.ds(0, bm), :], xbuf.at[slot], sem.at[slot]).wait()

        changed = (i == 0) | (be_ref[i] != be_ref[jnp.maximum(i - 1, 0)])

        @pl.when(changed)
        def _():
            w1b[...] = w1_ref[0].astype(_BF16)
            w3b[...] = w3_ref[0].astype(_BF16)
            w2b[...] = w2_ref[0].astype(_BF16)

        x = xbuf[slot].astype(_BF16)
        a = _dot(x, w1b[...])
        b = _dot(x, w3b[...])
        act = (a * _sigmoid(a) * b).astype(_BF16)
        y_ref[...] = _dot(act, w2b[...])

    @pl.when(i >= nb)
    def _():
        y_ref[...] = jnp.zeros(y_ref.shape, y_ref.dtype)


def _moe(h2, block_e, row_tok, n_used, w1, w3, w2):
    t, d = h2.shape
    bm = MOE_ROWS
    n_blocks = block_e.shape[0]
    ff = w1.shape[2]
    grid_spec = pltpu.PrefetchScalarGridSpec(
        num_scalar_prefetch=3,
        grid=(n_blocks,),
        in_specs=[pl.BlockSpec(memory_space=pl.ANY),
                  pl.BlockSpec((1, d, ff), lambda i, be, tok, nb: (be[i], 0, 0)),
                  pl.BlockSpec((1, d, ff), lambda i, be, tok, nb: (be[i], 0, 0)),
                  pl.BlockSpec((1, ff, d), lambda i, be, tok, nb: (be[i], 0, 0))],
        out_specs=pl.BlockSpec((bm, d), lambda i, be, tok, nb: (i, 0)),
        scratch_shapes=[pltpu.VMEM((2, bm, d), _F32),
                        pltpu.VMEM((d, ff), _BF16),
                        pltpu.VMEM((d, ff), _BF16),
                        pltpu.VMEM((ff, d), _BF16),
                        pltpu.SemaphoreType.DMA((2,))],
    )
    return pl.pallas_call(
        _moe_kernel,
        out_shape=jax.ShapeDtypeStruct((n_blocks * bm, d), _F32),
        grid_spec=grid_spec,
        compiler_params=_params("arbitrary", disable_bounds_checks=True),
        name="moe_experts",
    )(block_e, row_tok, n_used, h2, w1, w3, w2)


def _final_kernel(pos_ref, y_hbm, x1_ref, route_ref, g_ref, o_ref, ybuf, sem):
    i = pl.program_id(0)
    n = pl.num_programs(0)
    slot = i % 2
    tm = x1_ref.shape[0]

    def issue(tile, sl):
        def body(r, carry):
            for k in range(TOP_K):
                p = pos_ref[(tile * tm + r) * TOP_K + k]
                pltpu.make_async_copy(y_hbm.at[pl.ds(p, 1), :], ybuf.at[sl, k, pl.ds(r, 1), :], sem.at[sl]).start()
            return carry
        lax.fori_loop(0, tm, body, 0, unroll=DMA_UNROLL)

    @pl.when(i == 0)
    def _():
        issue(0, 0)

    @pl.when(i + 1 < n)
    def _():
        issue(i + 1, 1 - slot)

    for k in range(TOP_K):
        pltpu.make_async_copy(y_hbm.at[pl.ds(0, tm), :], ybuf.at[slot, k], sem.at[slot]).wait()

    route = route_ref[...]
    x = x1_ref[...] + route[:, 2:3] * ybuf[slot, 0] + route[:, 3:4] * ybuf[slot, 1]
    ms = jnp.mean(x * x, axis=-1, keepdims=True)
    o_ref[...] = x * lax.rsqrt(ms + NORM_EPS) * g_ref[...]


def _final(pos, y_rows, x1, route, g):
    t, d = x1.shape
    tm = FINAL_TILE
    grid_spec = pltpu.PrefetchScalarGridSpec(
        num_scalar_prefetch=1,
        grid=(t // tm,),
        in_specs=[pl.BlockSpec(memory_space=pl.ANY),
                  pl.BlockSpec((tm, d), lambda i, pos: (i, 0)),
                  pl.BlockSpec((tm, LANES), lambda i, pos: (i, 0)),
                  pl.BlockSpec((1, d), lambda i, pos: (0, 0))],
        out_specs=pl.BlockSpec((tm, d), lambda i, pos: (i, 0)),
        scratch_shapes=[pltpu.VMEM((2, TOP_K, tm, d), _F32),
                        pltpu.SemaphoreType.DMA((2,))],
    )
    return pl.pallas_call(
        _final_kernel,
        out_shape=jax.ShapeDtypeStruct((t, d), _F32),
        grid_spec=grid_spec,
        compiler_params=_params("arbitrary", disable_bounds_checks=True),
        name="combine_norm",
    )(pos, y_rows, x1, route, g.reshape(1, d))


def _routing_tables(expert, n_tokens):
    bm = MOE_ROWS
    n_assign = n_tokens * TOP_K
    n_blocks = -(-(n_assign + N_EXPERTS * (bm - 1)) // bm)
    i32 = jnp.int32
    flat_e = expert.reshape(-1)
    experts = jnp.arange(N_EXPERTS, dtype=i32)
    assign = jnp.arange(n_assign, dtype=i32)
    e_sorted, order = lax.sort((flat_e, assign), num_keys=1)
    start = jnp.sum((e_sorted[:, None] < experts[None, :]).astype(i32), axis=0)
    counts = jnp.concatenate([start[1:], jnp.full((1,), n_assign, i32)]) - start
    padded = (counts + bm - 1) // bm * bm
    pend = jnp.cumsum(padded)
    pstart = pend - padded
    shift = pstart - start
    dshift = shift - jnp.concatenate([jnp.zeros((1,), i32), shift[:-1]])
    shift_sorted = jnp.sum(jnp.where(assign[:, None] >= start[None, :], dshift[None, :], 0), axis=1)
    _, pos = lax.sort((order, assign + shift_sorted), num_keys=1)
    block_row0 = jnp.arange(n_blocks, dtype=i32) * bm
    block_e = jnp.minimum(jnp.sum((pend[None, :] <= block_row0[:, None]).astype(i32), axis=1), N_EXPERTS - 1)
    onehot = block_e[:, None] == experts[None, :]
    block_start = jnp.sum(jnp.where(onehot, (start - pstart)[None, :], 0), axis=1) + block_row0
    block_end = jnp.sum(jnp.where(onehot, (start + counts)[None, :], 0), axis=1)
    src = block_start[:, None] + jnp.arange(bm, dtype=i32)[None, :]
    valid = src < block_end[:, None]
    tok_sorted = order // TOP_K
    row_tok = jnp.where(valid, tok_sorted[jnp.clip(src, 0, n_assign - 1)], 0)
    n_used = (pend[-1] // bm).astype(i32).reshape(1)
    return block_e, row_tok.reshape(-1).astype(i32), n_used, pos.astype(i32)


def kernel(x, attn_norm, w_in, conv_dw, conv_dw_b, conv_ln_g, conv_ln_b, cmp_pos_k, cmp_k_w1, cmp_k_w2, cmp_pos_v, cmp_v_w1, cmp_v_w2, w_proj_conv, w_proj_nsa, w_merge, b_merge, w_out, ffn_norm, w_grp, b_grp, w_exp, b_exp, exp_w1, exp_w3, exp_w2, final_norm):
    batch, seq, d = x.shape
    t = batch * seq
    assert w_in.shape[0] == 1, "the block is specified with a single layer"
    xf = x.reshape(t, d)

    w_in_b = w_in[0].astype(_BF16)
    w_gate = w_in[0][:, 2 * CONV_CHANNELS + HEAD_COLS:].reshape(d, 3, N_KV_GROUPS, GROUP_SIZE)
    w_gate = w_gate.transpose(0, 2, 1, 3).reshape(d, GATE_COLS).astype(_BF16)

    h = _rmsnorm(xf, attn_norm[0], _BF16)
    glu = _glu_proj(h, w_in_b)
    tables = _rope_tables(seq)
    heads = _heads_proj(h, w_in_b, tables, seq)
    cmp_flat = _cmp_proj(h, w_in_b, tables, seq)
    gates = _gate_proj(h, w_gate).reshape(t, N_KV_GROUPS, 3 * GROUP_SIZE).transpose(1, 0, 2)

    half = CMP_STRIDE * HEAD_DIM
    pos = jnp.stack([cmp_pos_k[0], cmp_pos_v[0]]).reshape(2, 2, 1, half)
    cw1 = jnp.stack([cmp_k_w1[0], cmp_v_w1[0]]).astype(_BF16)
    cw2 = jnp.stack([cmp_k_w2[0], cmp_v_w2[0]]).astype(_BF16)
    cmp_kv = _compress(cmp_flat, pos, cw1, cw2, batch, seq)

    c = _conv(glu, conv_dw[0], conv_dw_b[0], conv_ln_g[0], conv_ln_b[0], batch, seq)
    o = _nsa(heads, cmp_kv, gates, batch, seq)
    mixed = _mix(c, o, h, w_proj_conv[0].astype(_BF16), w_proj_nsa[0].astype(_BF16),
                 w_merge[0].astype(_BF16), b_merge[0])

    w_route = jnp.concatenate(
        [w_grp[0], w_exp[0], jnp.zeros((d, LANES - N_GROUPS - N_EXPERTS), _F32)], axis=1)
    b_route = jnp.concatenate(
        [b_grp[0], b_exp[0], jnp.zeros((LANES - N_GROUPS - N_EXPERTS,), _F32)]).reshape(1, LANES)
    x1, h2, route = _out_router(mixed, xf, w_out[0].astype(_BF16), ffn_norm[0], w_route, b_route)

    expert = route[:, :TOP_K].astype(jnp.int32)
    block_e, row_tok, n_used, pos_rows = _routing_tables(expert, t)
    y_rows = _moe(h2, block_e, row_tok, n_used, exp_w1[0], exp_w3[0], exp_w2[0])
    out = _final(pos_rows, y_rows, x1, route, final_norm)
    return out.reshape(batch, seq, d)
```

```python
import numpy as np
import jax
import jax.numpy as jnp
from jax import lax
from jax.experimental import pallas as pl
from jax.experimental.pallas import tpu as pltpu

D_MODEL = 2048
CONV_CHANNELS = D_MODEL // 2
DW_TAPS = 31
N_HEADS = 16
HEAD_DIM = 128
N_KV_GROUPS = 4
GROUP_SIZE = N_HEADS // N_KV_GROUPS
ROT_DIM = HEAD_DIM // 4
ROPE_THETA = 500000.0
CMP_BLOCK = 32
CMP_STRIDE = 16
CMP_HIDDEN = 256
SEL_BLOCK = 64
N_SELECT = 16
N_FORCED_LOCAL = 2
WINDOW = 512
N_GROUPS = 8
EXPERTS_PER_GROUP = 8
N_EXPERTS = N_GROUPS * EXPERTS_PER_GROUP
TOP_K = 2
EXPERT_FF = 512
Q_COLS = N_HEADS * HEAD_DIM
KV_COLS = N_KV_GROUPS * HEAD_DIM
HEAD_COLS = Q_COLS + 6 * KV_COLS
GATE_COLS = 3 * N_HEADS
NORM_EPS = 1e-6
NEG_INF = -1e30
FORCE_BONUS = 1e4
LOG2_E = 1.4426950408889634

LANES = 128
SUBLANES = 8
VMEM_LIMIT_BYTES = 56 * 1024 * 1024

ROW_TILE = 512
HEAD_TILE = GROUP_SIZE * HEAD_DIM
Q_TILE = 128
SEL_CHUNK = 512
CONV_TILE = 256
CONV_HALO = 32
CONV_ROWS = 16
OUT_TILE = 512
MOE_ROWS = 256
FINAL_TILE = 128
DMA_UNROLL = 8

_BF16 = jnp.bfloat16
_F32 = jnp.float32


def _params(*sem, **kw):
    return pltpu.CompilerParams(dimension_semantics=sem, vmem_limit_bytes=VMEM_LIMIT_BYTES, **kw)


def _dot(a, b):
    return jnp.dot(a, b, preferred_element_type=_F32)


def _dot_nt(a, b):
    return lax.dot_general(a, b, (((1,), (1,)), ((), ())), preferred_element_type=_F32)


def _sigmoid(x):
    return 1.0 / (1.0 + jnp.exp(-x))


def _rmsnorm_kernel(x_ref, g_ref, o_ref):
    x = x_ref[...]
    ms = jnp.mean(x * x, axis=-1, keepdims=True)
    o_ref[...] = (x * lax.rsqrt(ms + NORM_EPS) * g_ref[...]).astype(o_ref.dtype)


def _rmsnorm(x, g, out_dtype):
    t, d = x.shape
    return pl.pallas_call(
        _rmsnorm_kernel,
        out_shape=jax.ShapeDtypeStruct((t, d), out_dtype),
        grid=(t // ROW_TILE,),
        in_specs=[pl.BlockSpec((ROW_TILE, d), lambda i: (i, 0)),
                  pl.BlockSpec((1, d), lambda i: (0, 0))],
        out_specs=pl.BlockSpec((ROW_TILE, d), lambda i: (i, 0)),
        compiler_params=_params("parallel"),
        name="rmsnorm",
    )(x, g.reshape(1, d))


def _glu_kernel(h_ref, wa_ref, wb_ref, o_ref):
    h = h_ref[...]
    a = _dot(h, wa_ref[...])
    b = _dot(h, wb_ref[...])
    o_ref[...] = a * _sigmoid(b)


def _glu_proj(h, w_in_bf16):
    t, d = h.shape
    tn = HEAD_TILE
    nb = CONV_CHANNELS // tn
    return pl.pallas_call(
        _glu_kernel,
        out_shape=jax.ShapeDtypeStruct((t, CONV_CHANNELS), _F32),
        grid=(t // ROW_TILE, nb),
        in_specs=[pl.BlockSpec((ROW_TILE, d), lambda i, j: (i, 0)),
                  pl.BlockSpec((d, tn), lambda i, j: (0, j)),
                  pl.BlockSpec((d, tn), lambda i, j: (0, j + nb))],
        out_specs=pl.BlockSpec((ROW_TILE, tn), lambda i, j: (i, j)),
        compiler_params=_params("parallel", "arbitrary"),
        name="glu_proj",
    )(h, w_in_bf16, w_in_bf16)


N_Q_SLABS = Q_COLS // HEAD_TILE
N_HEAD_SLABS = N_Q_SLABS + 4
Q_COL_BLOCK = 2 * CONV_CHANNELS // HEAD_TILE
CMP_COL_BLOCK = Q_COL_BLOCK + N_Q_SLABS
KV_COL_BLOCK = CMP_COL_BLOCK + 2


def _rope_tables(seq):
    pos = jnp.arange(seq, dtype=_F32)
    inv = ROPE_THETA ** (-jnp.arange(0, ROT_DIM, 2, dtype=_F32) / ROT_DIM)
    ang = pos[:, None] * inv[None, :]
    cos, sin = jnp.cos(ang), jnp.sin(ang)
    half = ROT_DIM // 2
    ones = jnp.ones((seq, HEAD_DIM - ROT_DIM), _F32)
    zeros_h = jnp.zeros((seq, half), _F32)
    zeros_r = jnp.zeros((seq, HEAD_DIM - ROT_DIM), _F32)
    c = jnp.concatenate([cos, cos, ones], axis=-1)
    s_lo = jnp.concatenate([zeros_h, sin, zeros_r], axis=-1)
    s_hi = jnp.concatenate([-sin, zeros_h, zeros_r], axis=-1)
    return c, s_lo, s_hi


def _rotate_head(xs, c, slo, shi):
    half = ROT_DIM // 2
    return xs * c + pltpu.roll(xs, half, 1) * slo + pltpu.roll(xs, HEAD_DIM - half, 1) * shi


def _heads_kernel(h_ref, w_ref, c_ref, slo_ref, shi_ref, o_ref):
    j = pl.program_id(1)
    acc = _dot(h_ref[...], w_ref[...])
    is_q = j < N_Q_SLABS
    rotate = is_q | (j % 2 == 0)

    @pl.when(rotate)
    def _():
        scale = jnp.where(is_q, HEAD_DIM ** -0.5 * LOG2_E, 1.0).astype(_F32)
        c, slo, shi = c_ref[...], slo_ref[...], shi_ref[...]
        for hd in range(GROUP_SIZE):
            r = _rotate_head(acc[:, hd * HEAD_DIM:(hd + 1) * HEAD_DIM], c, slo, shi)
            o_ref[0, hd] = (r * scale).astype(o_ref.dtype)

    @pl.when(jnp.logical_not(rotate))
    def _():
        for hd in range(GROUP_SIZE):
            o_ref[0, hd] = acc[:, hd * HEAD_DIM:(hd + 1) * HEAD_DIM].astype(o_ref.dtype)


def _heads_proj(h, w_in_bf16, tables, seq):
    t, d = h.shape
    tm = ROW_TILE
    spt = seq // tm
    tab = pl.BlockSpec((tm, HEAD_DIM), lambda i, j: (i % spt, 0))

    def w_block(i, j):
        return (0, jnp.where(j < N_Q_SLABS, j + Q_COL_BLOCK, j - N_Q_SLABS + KV_COL_BLOCK))

    return pl.pallas_call(
        _heads_kernel,
        out_shape=jax.ShapeDtypeStruct((N_HEAD_SLABS, GROUP_SIZE, t, HEAD_DIM), _BF16),
        grid=(t // tm, N_HEAD_SLABS),
        in_specs=[pl.BlockSpec((tm, d), lambda i, j: (i, 0)),
                  pl.BlockSpec((d, HEAD_TILE), w_block),
                  tab, tab, tab],
        out_specs=pl.BlockSpec((1, GROUP_SIZE, tm, HEAD_DIM), lambda i, j: (j, 0, i, 0)),
        compiler_params=_params("parallel", "arbitrary"),
        name="heads_proj",
    )(h, w_in_bf16, *tables)


def _cmp_proj_kernel(h_ref, w_ref, c_ref, slo_ref, shi_ref, o_ref, stage_ref):
    j = pl.program_id(1)
    acc = _dot(h_ref[...], w_ref[...])

    @pl.when(j == 0)
    def _():
        c, slo, shi = c_ref[...], slo_ref[...], shi_ref[...]
        for g in range(N_KV_GROUPS):
            stage_ref[g] = _rotate_head(acc[:, g * HEAD_DIM:(g + 1) * HEAD_DIM], c, slo, shi)

    @pl.when(j != 0)
    def _():
        for g in range(N_KV_GROUPS):
            stage_ref[g] = acc[:, g * HEAD_DIM:(g + 1) * HEAD_DIM]

    n_chunk = stage_ref.shape[1] // CMP_STRIDE
    for g in range(N_KV_GROUPS):
        for l in range(CMP_STRIDE):
            rows = stage_ref[g, pl.ds(l, n_chunk, stride=CMP_STRIDE), :]
            o_ref[0, g, :, l * HEAD_DIM:(l + 1) * HEAD_DIM] = rows.astype(o_ref.dtype)


def _cmp_proj(h, w_in_bf16, tables, seq):
    t, d = h.shape
    tm = ROW_TILE
    spt = seq // tm
    tab = pl.BlockSpec((tm, HEAD_DIM), lambda i, j: (i % spt, 0))
    return pl.pallas_call(
        _cmp_proj_kernel,
        out_shape=jax.ShapeDtypeStruct((2, N_KV_GROUPS, t // CMP_STRIDE, CMP_STRIDE * HEAD_DIM), _BF16),
        grid=(t // tm, 2),
        in_specs=[pl.BlockSpec((tm, d), lambda i, j: (i, 0)),
                  pl.BlockSpec((d, HEAD_TILE), lambda i, j: (0, j + CMP_COL_BLOCK)),
                  tab, tab, tab],
        out_specs=pl.BlockSpec((1, N_KV_GROUPS, tm // CMP_STRIDE, CMP_STRIDE * HEAD_DIM),
                               lambda i, j: (j, 0, i, 0)),
        scratch_shapes=[pltpu.VMEM((N_KV_GROUPS, tm, HEAD_DIM), _F32)],
        compiler_params=_params("parallel", "arbitrary"),
        name="cmp_proj",
    )(h, w_in_bf16, *tables)


def _gate_kernel(h_ref, w_ref, o_ref):
    o_ref[...] = _sigmoid(_dot(h_ref[...], w_ref[...]))


def _gate_proj(h, w_gate_bf16):
    t, d = h.shape
    n = w_gate_bf16.shape[1]
    return pl.pallas_call(
        _gate_kernel,
        out_shape=jax.ShapeDtypeStruct((t, n), _F32),
        grid=(t // ROW_TILE,),
        in_specs=[pl.BlockSpec((ROW_TILE, d), lambda i: (i, 0)),
                  pl.BlockSpec((d, n), lambda i: (0, 0))],
        out_specs=pl.BlockSpec((ROW_TILE, n), lambda i: (i, 0)),
        compiler_params=_params("parallel"),
        name="gate_proj",
    )(h, w_gate_bf16)


def _compress_kernel(kv_ref, pos_ref, w1_ref, w2_ref, o_ref):
    half = CMP_STRIDE * HEAD_DIM
    c = kv_ref[0, 0].astype(_F32)
    n_chunk = c.shape[0]
    top = (c + pos_ref[0, 0]).astype(_BF16)
    bot = (c + pos_ref[0, 1]).astype(_BF16)
    u = _dot(top, w1_ref[0, :half, :])
    v = _dot(bot, w1_ref[0, half:, :])
    hidden = u + pltpu.roll(v, n_chunk - 1, 0)
    act = jax.nn.gelu(hidden).astype(_BF16)
    o_ref[0, 0, 0] = _dot(act, w2_ref[0]).astype(o_ref.dtype)


def _compress(cmp_flat, pos, w1, w2, batch, seq):
    n_chunk = seq // CMP_STRIDE
    return pl.pallas_call(
        _compress_kernel,
        out_shape=jax.ShapeDtypeStruct((2, batch, N_KV_GROUPS, n_chunk, HEAD_DIM), _BF16),
        grid=(2, batch, N_KV_GROUPS),
        in_specs=[pl.BlockSpec((1, 1, n_chunk, CMP_STRIDE * HEAD_DIM), lambda s, b, g: (s, g, b, 0)),
                  pl.BlockSpec((1, 2, 1, CMP_STRIDE * HEAD_DIM), lambda s, b, g: (s, 0, 0, 0)),
                  pl.BlockSpec((1, CMP_BLOCK * HEAD_DIM, CMP_HIDDEN), lambda s, b, g: (s, 0, 0)),
                  pl.BlockSpec((1, CMP_HIDDEN, HEAD_DIM), lambda s, b, g: (s, 0, 0))],
        out_specs=pl.BlockSpec((1, 1, 1, n_chunk, HEAD_DIM), lambda s, b, g: (s, b, g, 0, 0)),
        compiler_params=_params("parallel", "parallel", "parallel"),
        name="compress",
    )(cmp_flat, pos, w1, w2)


def _conv_kernel(cur_ref, halo_ref, dw_ref, dwb_ref, g_ref, b_ref, o_ref, ext_ref, sh_ref, acc_ref):
    i = pl.program_id(1)
    ts = cur_ref.shape[0]
    ext_ref[:CONV_HALO, :] = jnp.where(i > 0, halo_ref[...], 0.0)
    ext_ref[CONV_HALO:, :] = cur_ref[...]
    n_sh = sh_ref.shape[1]
    for b in range(1, SUBLANES):
        sh_ref[b - 1] = ext_ref[b:b + n_sh, :]
    off = CONV_HALO - (DW_TAPS - 1)

    def chunk(c, carry):
        r0 = pl.multiple_of(c * CONV_ROWS, CONV_ROWS)
        y = None
        for k in range(DW_TAPS):
            a, b = divmod(off + k, SUBLANES)
            src = ext_ref if b == 0 else sh_ref.at[b - 1]
            term = src[pl.ds(r0 + SUBLANES * a, CONV_ROWS), :] * dw_ref[k:k + 1, :]
            y = term if y is None else y + term
        acc_ref[pl.ds(r0, CONV_ROWS), :] = y
        return carry

    lax.fori_loop(0, ts // CONV_ROWS, chunk, 0)
    y = acc_ref[...] + dwb_ref[...]
    mu = jnp.mean(y, axis=-1, keepdims=True)
    yc = y - mu
    var = jnp.mean(yc * yc, axis=-1, keepdims=True)
    z = yc * lax.rsqrt(var + NORM_EPS) * g_ref[...] + b_ref[...]
    o_ref[...] = (z * _sigmoid(z)).astype(o_ref.dtype)


def _conv(glu, dw, dw_b, ln_g, ln_b, batch, seq):
    t, c = glu.shape
    ts = CONV_TILE
    nt = seq // ts
    hb = ts // CONV_HALO
    dw_pad = jnp.concatenate([dw, jnp.zeros((CONV_HALO - DW_TAPS, c), _F32)], axis=0)
    vec = pl.BlockSpec((1, c), lambda b, i: (0, 0))
    return pl.pallas_call(
        _conv_kernel,
        out_shape=jax.ShapeDtypeStruct((t, c), _BF16),
        grid=(batch, nt),
        in_specs=[pl.BlockSpec((ts, c), lambda b, i: (b * nt + i, 0)),
                  pl.BlockSpec((CONV_HALO, c), lambda b, i: (jnp.maximum((b * nt + i) * hb - 1, 0), 0)),
                  pl.BlockSpec((CONV_HALO, c), lambda b, i: (0, 0)),
                  vec, vec, vec],
        out_specs=pl.BlockSpec((ts, c), lambda b, i: (b * nt + i, 0)),
        scratch_shapes=[pltpu.VMEM((ts + CONV_HALO, c), _F32),
                        pltpu.VMEM((SUBLANES - 1, ts + CONV_HALO - SUBLANES, c), _F32),
                        pltpu.VMEM((ts, c), _F32)],
        compiler_params=_params("parallel", "arbitrary"),
        name="conv",
    )(glu, glu, dw_pad, dw_b.reshape(1, c), ln_g.reshape(1, c), ln_b.reshape(1, c))


def _importance_matrix(n_cmp_pad, n_blk):
    r_sel, r_cmp = SEL_BLOCK // CMP_STRIDE, CMP_BLOCK // CMP_STRIDE
    a = np.zeros((n_blk, n_cmp_pad), np.float32)
    for j in range(n_blk):
        for m in range(r_sel):
            for n in range(r_cmp):
                c = r_sel * j + m - n
                if 0 <= c < n_cmp_pad - 1:
                    a[j, c] += 1.0
    return a


def _expand_matrix(n_blk, seq):
    e = np.zeros((LANES, seq), np.float32)
    for j in range(n_blk):
        e[j, j * SEL_BLOCK:(j + 1) * SEL_BLOCK] = 1.0
    return e


def _nsa_step(n_active, q_ref, kc_ref, vc_ref, ks_ref, vs_ref, kw_ref, vw_ref, gate_ref, amat_ref, emat_ref,
              o_ref):
    tq_n = q_ref.shape[2]
    n_cmp_pad = kc_ref.shape[3]
    n_blk = amat_ref.shape[0]
    rows = GROUP_SIZE * tq_n
    start = pl.program_id(2) * tq_n
    q = q_ref[0].reshape(rows, HEAD_DIM)
    tq = start + lax.broadcasted_iota(jnp.int32, (tq_n, 1), 0)

    cidx = lax.broadcasted_iota(jnp.int32, (1, n_cmp_pad), 1)
    cvalid = (cidx * CMP_STRIDE + (CMP_BLOCK - 1) <= tq) & (cidx < n_cmp_pad - 1)
    cbias = jnp.where(cvalid, 0.0, NEG_INF)
    s = _dot_nt(q, kc_ref[0, 0, 0]).reshape(GROUP_SIZE, tq_n, n_cmp_pad) + cbias[None]
    e = jnp.exp2(s - jnp.max(s, axis=-1, keepdims=True))
    inv_l = jnp.where((tq >= CMP_BLOCK - 1)[None], 1.0 / jnp.sum(e, axis=-1, keepdims=True), 0.0)
    p = e * inv_l
    o_cmp = _dot(p.reshape(rows, n_cmp_pad).astype(_BF16), vc_ref[0, 0, 0])

    imp = lax.dot_general(amat_ref[...], jnp.sum(p, axis=0), (((1,), (1,)), ((), ())),
                          preferred_element_type=_F32, precision=lax.Precision.HIGHEST)
    tq_row = start + lax.broadcasted_iota(jnp.int32, (1, tq_n), 1)
    blk = lax.broadcasted_iota(jnp.int32, (n_blk, 1), 0)
    cur = tq_row // SEL_BLOCK
    forced = (blk == 0) | ((blk <= cur) & (blk > cur - N_FORCED_LOCAL))
    imp = jnp.where(forced, imp + FORCE_BONUS, imp)
    imp = jnp.where(blk * SEL_BLOCK <= tq_row, imp, NEG_INF)
    rank = jnp.zeros((n_blk, tq_n), _F32)
    for i in range(n_blk):
        row = imp[i:i + 1, :]
        before = (row > imp) | ((row == imp) & (blk > i))
        rank = rank + jnp.where(before, 1.0, 0.0)
    sel_t = jnp.where(rank < float(min(N_SELECT, n_blk)), 1.0, 0.0)
    sel = jnp.concatenate([sel_t, jnp.zeros((LANES - n_blk, tq_n), _F32)], axis=0).T.astype(_BF16)

    ones_c = jnp.ones((SEL_CHUNK, HEAD_DIM), _BF16)
    m_run = None
    acc = None
    for c in range(n_active):
        lo = c * SEL_CHUNK
        chosen = _dot(sel, emat_ref[:, lo:lo + SEL_CHUNK]) > 0.5
        if c == n_active - 1:
            kpos = lo + lax.broadcasted_iota(jnp.int32, (1, SEL_CHUNK), 1)
            chosen = chosen & (kpos <= tq)
        bias = jnp.where(chosen, 0.0, NEG_INF)
        sc = _dot_nt(q, ks_ref[0, 0, lo:lo + SEL_CHUNK, :]).reshape(GROUP_SIZE, tq_n, SEL_CHUNK) + bias[None]
        m_c = jnp.max(sc, axis=-1, keepdims=True)
        m_new = m_c if c == 0 else jnp.maximum(m_run, m_c)
        e_c = jnp.exp2(sc - m_new).astype(_BF16)
        v_aug = jnp.concatenate([vs_ref[0, 0, lo:lo + SEL_CHUNK, :], ones_c], axis=1)
        pv = _dot(e_c.reshape(rows, SEL_CHUNK), v_aug).reshape(GROUP_SIZE, tq_n, 2 * HEAD_DIM)
        acc = pv if c == 0 else jnp.exp2(m_run - m_new) * acc + pv
        m_run = m_new

    span = WINDOW + tq_n
    kstart = pl.multiple_of(jnp.maximum(start - WINDOW, 0), tq_n)
    kpos = kstart + lax.broadcasted_iota(jnp.int32, (1, span), 1)
    wbias = jnp.where((kpos <= tq) & (tq - kpos < WINDOW), 0.0, NEG_INF)
    sw = _dot_nt(q, kw_ref[0, 0, pl.ds(kstart, span), :]).reshape(GROUP_SIZE, tq_n, span) + wbias[None]
    e_w = jnp.exp2(sw - jnp.max(sw, axis=-1, keepdims=True)).astype(_BF16)
    vw_aug = jnp.concatenate([vw_ref[0, 0, pl.ds(kstart, span), :], jnp.ones((span, HEAD_DIM), _BF16)], axis=1)
    win = _dot(e_w.reshape(rows, span), vw_aug).reshape(GROUP_SIZE, tq_n, 2 * HEAD_DIM)

    o_cmp = o_cmp.reshape(GROUP_SIZE, tq_n, HEAD_DIM)
    gates = gate_ref[0]
    for r in range(GROUP_SIZE):
        g_cmp = gates[:, r:r + 1]
        g_slc = gates[:, GROUP_SIZE + r:GROUP_SIZE + r + 1] / acc[r, :, HEAD_DIM:]
        g_win = gates[:, 2 * GROUP_SIZE + r:2 * GROUP_SIZE + r + 1] / win[r, :, HEAD_DIM:]
        o = g_cmp * o_cmp[r] + g_slc * acc[r, :, :HEAD_DIM] + g_win * win[r, :, :HEAD_DIM]
        o_ref[:, r * HEAD_DIM:(r + 1) * HEAD_DIM] = o.astype(o_ref.dtype)


def _nsa_kernel(q_ref, kc_ref, vc_ref, ks_ref, vs_ref, kw_ref, vw_ref, gate_ref, amat_ref, emat_ref, o_ref):
    tq_n = q_ref.shape[2]
    seq = ks_ref.shape[2]
    n_active = (pl.program_id(2) * tq_n + tq_n - 1) // SEL_CHUNK + 1
    for k in range(1, seq // SEL_CHUNK + 1):
        @pl.when(n_active == k)
        def _(k=k):
            _nsa_step(k, q_ref, kc_ref, vc_ref, ks_ref, vs_ref, kw_ref, vw_ref, gate_ref, amat_ref, emat_ref,
                      o_ref)


def _nsa(heads, cmp_kv, gates, batch, seq):
    t = heads.shape[2]
    tq = Q_TILE
    nq = seq // tq
    n_chunk = cmp_kv.shape[3]
    n_blk = seq // SEL_BLOCK
    assert n_chunk == LANES and n_blk <= LANES and seq % SEL_CHUNK == 0 and WINDOW % tq == 0
    amat = jnp.asarray(_importance_matrix(n_chunk, n_blk))
    emat = jnp.asarray(_expand_matrix(n_blk, seq), dtype=_BF16)

    def kv_spec(slab):
        return pl.BlockSpec((1, 1, seq, HEAD_DIM), lambda b, g, i: (slab, g, b, 0))

    def cmp_spec(s):
        return pl.BlockSpec((1, 1, 1, n_chunk, HEAD_DIM), lambda b, g, i: (s, b, g, 0, 0))

    return pl.pallas_call(
        _nsa_kernel,
        out_shape=jax.ShapeDtypeStruct((t, Q_COLS), _BF16),
        grid=(batch, N_KV_GROUPS, nq),
        in_specs=[pl.BlockSpec((1, GROUP_SIZE, tq, HEAD_DIM), lambda b, g, i: (g, 0, b * nq + i, 0)),
                  cmp_spec(0), cmp_spec(1),
                  kv_spec(N_Q_SLABS), kv_spec(N_Q_SLABS + 1),
                  kv_spec(N_Q_SLABS + 2), kv_spec(N_Q_SLABS + 3),
                  pl.BlockSpec((1, tq, 3 * GROUP_SIZE), lambda b, g, i: (g, b * nq + i, 0)),
                  pl.BlockSpec((n_blk, n_chunk), lambda b, g, i: (0, 0)),
                  pl.BlockSpec((LANES, seq), lambda b, g, i: (0, 0))],
        out_specs=pl.BlockSpec((tq, HEAD_TILE), lambda b, g, i: (b * nq + i, g)),
        compiler_params=_params("parallel", "parallel", "arbitrary"),
        name="nsa",
    )(heads, cmp_kv, cmp_kv, heads, heads, heads, heads, gates, amat, emat)


def _mix_kernel(c_ref, o_ref, h_ref, wpc_ref, wpn_ref, wmc_ref, wmn_ref, bc_ref, bn_ref, out_ref):
    h = h_ref[...]
    g_conv = _sigmoid(_dot(h, wmc_ref[...]) + bc_ref[...])
    g_nsa = _sigmoid(_dot(h, wmn_ref[...]) + bn_ref[...])
    y_conv = _dot(c_ref[...], wpc_ref[...])
    y_nsa = _dot(o_ref[...], wpn_ref[...])
    out_ref[...] = (g_conv * y_conv + g_nsa * y_nsa).astype(out_ref.dtype)


def _mix(c, o, h, wpc, wpn, wm, bm):
    t, d = h.shape
    tm, tn = ROW_TILE, HEAD_TILE
    nb = d // tn
    bm2 = bm.reshape(1, 2 * d)
    return pl.pallas_call(
        _mix_kernel,
        out_shape=jax.ShapeDtypeStruct((t, d), _BF16),
        grid=(t // tm, nb),
        in_specs=[pl.BlockSpec((tm, CONV_CHANNELS), lambda i, j: (i, 0)),
                  pl.BlockSpec((tm, Q_COLS), lambda i, j: (i, 0)),
                  pl.BlockSpec((tm, d), lambda i, j: (i, 0)),
                  pl.BlockSpec((CONV_CHANNELS, tn), lambda i, j: (0, j)),
                  pl.BlockSpec((Q_COLS, tn), lambda i, j: (0, j)),
                  pl.BlockSpec((d, tn), lambda i, j: (0, j)),
                  pl.BlockSpec((d, tn), lambda i, j: (0, j + nb)),
                  pl.BlockSpec((1, tn), lambda i, j: (0, j)),
                  pl.BlockSpec((1, tn), lambda i, j: (0, j + nb))],
        out_specs=pl.BlockSpec((tm, tn), lambda i, j: (i, j)),
        compiler_params=_params("parallel", "arbitrary"),
        name="mix",
    )(c, o, h, wpc, wpn, wm, wm, bm2, bm2)


def _out_router_kernel(mix_ref, x_ref, wout_ref, g_ref, wrh_ref, wrl_ref, br_ref, x1_ref, h2_ref, route_ref):
    x1 = x_ref[...] + _dot(mix_ref[...], wout_ref[...])
    x1_ref[...] = x1
    ms = jnp.mean(x1 * x1, axis=-1, keepdims=True)
    h2 = x1 * lax.rsqrt(ms + NORM_EPS) * g_ref[...]
    h2_ref[...] = h2
    h_hi = h2.astype(_BF16)
    h_lo = (h2 - h_hi.astype(_F32)).astype(_BF16)
    logits = (_dot(h_hi, wrh_ref[...]) + _dot(h_lo, wrh_ref[...]) + _dot(h_hi, wrl_ref[...])
              + br_ref[...])
    lane = lax.broadcasted_iota(jnp.int32, (1, LANES), 1)
    lane_f = lane.astype(_F32)
    big = float(LANES)
    gl = jnp.where(lane < N_GROUPS, logits, NEG_INF)
    gmax = jnp.max(gl, axis=-1, keepdims=True)
    p_g = 1.0 / jnp.sum(jnp.where(lane < N_GROUPS, jnp.exp(gl - gmax), 0.0), axis=-1, keepdims=True)
    g_sel = jnp.min(jnp.where(gl == gmax, lane_f, big), axis=-1, keepdims=True)
    lo = N_GROUPS + g_sel * EXPERTS_PER_GROUP
    in_grp = (lane_f >= lo) & (lane_f < lo + EXPERTS_PER_GROUP)
    el = jnp.where(in_grp, logits, NEG_INF)
    v0 = jnp.max(el, axis=-1, keepdims=True)
    i0 = jnp.min(jnp.where(el == v0, lane_f, big), axis=-1, keepdims=True)
    el2 = jnp.where(lane_f == i0, NEG_INF, el)
    v1 = jnp.max(el2, axis=-1, keepdims=True)
    i1 = jnp.min(jnp.where(el2 == v1, lane_f, big), axis=-1, keepdims=True)
    t1 = jnp.exp(v1 - v0)
    w0 = p_g / (1.0 + t1)
    w1 = p_g * t1 / (1.0 + t1)
    route = jnp.where(lane == 0, i0 - N_GROUPS, 0.0)
    route = jnp.where(lane == 1, i1 - N_GROUPS, route)
    route = jnp.where(lane == 2, w0, route)
    route = jnp.where(lane == 3, w1, route)
    route_ref[...] = route


def _out_router(mixed, x, w_out, ffn_g, w_route, b_route):
    t, d = x.shape
    tm = OUT_TILE
    row = pl.BlockSpec((tm, d), lambda i: (i, 0))
    w_route_hi = w_route.astype(_BF16)
    w_route_lo = (w_route - w_route_hi.astype(_F32)).astype(_BF16)
    return pl.pallas_call(
        _out_router_kernel,
        out_shape=(jax.ShapeDtypeStruct((t, d), _F32),
                   jax.ShapeDtypeStruct((t, d), _F32),
                   jax.ShapeDtypeStruct((t, LANES), _F32)),
        grid=(t // tm,),
        in_specs=[row, row,
                  pl.BlockSpec((d, d), lambda i: (0, 0), pipeline_mode=pl.Buffered(1)),
                  pl.BlockSpec((1, d), lambda i: (0, 0)),
                  pl.BlockSpec((d, LANES), lambda i: (0, 0)),
                  pl.BlockSpec((d, LANES), lambda i: (0, 0)),
                  pl.BlockSpec((1, LANES), lambda i: (0, 0))],
        out_specs=(row, row, pl.BlockSpec((tm, LANES), lambda i: (i, 0))),
        compiler_params=_params("parallel"),
        name="out_router",
    )(mixed, x, w_out, ffn_g.reshape(1, d), w_route_hi, w_route_lo, b_route)


def _moe_kernel(be_ref, nxt_ref, ord_ref, tok_ref, nb_ref, h2_hbm, w1_hbm, w3_hbm, w2_hbm, y_ref,
                xbuf, wf1, wf3, wf2, w1b, w3b, w2b, xsem, wsem):
    i = pl.program_id(0)
    nb = nb_ref[0]
    bm = xbuf.shape[1]

    def row_copy(tok, r, sl):
        return pltpu.make_async_copy(h2_hbm.at[pl.ds(tok, 1), :], xbuf.at[sl, pl.ds(r, 1), :], xsem.at[sl])

    def rows_wait(sl):
        pltpu.make_async_copy(h2_hbm.at[pl.ds(0, bm), :], xbuf.at[sl], xsem.at[sl]).wait()

    def weight_copies(e, ws):
        return (pltpu.make_async_copy(w1_hbm.at[e], wf1.at[ws], wsem.at[ws]),
                pltpu.make_async_copy(w3_hbm.at[e], wf3.at[ws], wsem.at[ws]),
                pltpu.make_async_copy(w2_hbm.at[e], wf2.at[ws], wsem.at[ws]))

    @pl.when(i == 0)
    def _():
        def body(r, carry):
            row_copy(tok_ref[r], r, 0).start()
            return carry
        lax.fori_loop(0, bm, body, 0, unroll=DMA_UNROLL)
        for cp in weight_copies(be_ref[0], 0):
            cp.start()

    def step(par):
        rows_wait(par)
        e = be_ref[i]
        ws = ord_ref[i] % 2
        first = (i == 0) | (e != be_ref[jnp.maximum(i - 1, 0)])

        @pl.when(first)
        def _():
            for cp in weight_copies(e, ws):
                cp.wait()
            nxt = nxt_ref[i]

            @pl.when(nxt >= 0)
            def _():
                for cp in weight_copies(nxt, 1 - ws):
                    cp.start()

            w1b[...] = wf1[ws].astype(_BF16)
            w3b[...] = wf3[ws].astype(_BF16)
            w2b[...] = wf2[ws].astype(_BF16)

        base = (i + 1) * bm
        for r in range(bm):
            row_copy(tok_ref[base + r], r, 1 - par).start()
        x = xbuf[par].astype(_BF16)
        a = _dot(x, w1b[...])
        b = _dot(x, w3b[...])
        act = (a * _sigmoid(a) * b).astype(_BF16)
        y_ref[...] = _dot(act, w2b[...])

    for par in range(2):
        @pl.when((i < nb) & (i % 2 == par))
        def _(par=par):
            step(par)

        @pl.when((i == nb) & (i % 2 == par))
        def _(par=par):
            rows_wait(par)

    @pl.when(i >= nb)
    def _():
        y_ref[...] = jnp.zeros(y_ref.shape, y_ref.dtype)


def _moe(h2, tables, w1, w3, w2):
    t, d = h2.shape
    bm = MOE_ROWS
    block_e, block_next, block_ord, row_tok, n_used = tables
    n_steps = block_e.shape[0] + 1
    ff = w1.shape[2]
    grid_spec = pltpu.PrefetchScalarGridSpec(
        num_scalar_prefetch=5,
        grid=(n_steps,),
        in_specs=[pl.BlockSpec(memory_space=pl.ANY)] * 4,
        out_specs=pl.BlockSpec((bm, d), lambda i, *_: (i, 0)),
        scratch_shapes=[pltpu.VMEM((2, bm, d), _F32),
                        pltpu.VMEM((2, d, ff), _F32),
                        pltpu.VMEM((2, d, ff), _F32),
                        pltpu.VMEM((2, ff, d), _F32),
                        pltpu.VMEM((d, ff), _BF16),
                        pltpu.VMEM((d, ff), _BF16),
                        pltpu.VMEM((ff, d), _BF16),
                        pltpu.SemaphoreType.DMA((2,)),
                        pltpu.SemaphoreType.DMA((2,))],
    )
    return pl.pallas_call(
        _moe_kernel,
        out_shape=jax.ShapeDtypeStruct((n_steps * bm, d), _F32),
        grid_spec=grid_spec,
        compiler_params=_params("arbitrary", disable_bounds_checks=True),
        name="moe_experts",
    )(block_e, block_next, block_ord, row_tok, n_used, h2, w1, w3, w2)


def _final_kernel(pos_ref, y_hbm, x1_ref, route_ref, g_ref, o_ref, ybuf, sem):
    i = pl.program_id(0)
    n = pl.num_programs(0)
    tm = x1_ref.shape[0]

    def row_copy(p, r, k, sl):
        return pltpu.make_async_copy(y_hbm.at[pl.ds(p, 1), :], ybuf.at[sl, k, pl.ds(r, 1), :], sem.at[sl])

    @pl.when(i == 0)
    def _():
        def body(r, carry):
            for k in range(TOP_K):
                row_copy(pos_ref[r * TOP_K + k], r, k, 0).start()
            return carry
        lax.fori_loop(0, tm, body, 0, unroll=DMA_UNROLL)

    def step(par, issue_next):
        if issue_next:
            base = (i + 1) * tm * TOP_K
            for r in range(tm):
                for k in range(TOP_K):
                    row_copy(pos_ref[base + r * TOP_K + k], r, k, 1 - par).start()
        for k in range(TOP_K):
            pltpu.make_async_copy(y_hbm.at[pl.ds(0, tm), :], ybuf.at[par, k], sem.at[par]).wait()
        route = route_ref[...]
        x = x1_ref[...] + route[:, 2:3] * ybuf[par, 0] + route[:, 3:4] * ybuf[par, 1]
        ms = jnp.mean(x * x, axis=-1, keepdims=True)
        o_ref[...] = x * lax.rsqrt(ms + NORM_EPS) * g_ref[...]

    for par in range(2):
        @pl.when((i % 2 == par) & (i + 1 < n))
        def _(par=par):
            step(par, True)

        @pl.when((i % 2 == par) & (i + 1 == n))
        def _(par=par):
            step(par, False)


def _final(pos, y_rows, x1, route, g):
    t, d = x1.shape
    tm = FINAL_TILE
    grid_spec = pltpu.PrefetchScalarGridSpec(
        num_scalar_prefetch=1,
        grid=(t // tm,),
        in_specs=[pl.BlockSpec(memory_space=pl.ANY),
                  pl.BlockSpec((tm, d), lambda i, pos: (i, 0)),
                  pl.BlockSpec((tm, LANES), lambda i, pos: (i, 0)),
                  pl.BlockSpec((1, d), lambda i, pos: (0, 0))],
        out_specs=pl.BlockSpec((tm, d), lambda i, pos: (i, 0)),
        scratch_shapes=[pltpu.VMEM((2, TOP_K, tm, d), _F32),
                        pltpu.SemaphoreType.DMA((2,))],
    )
    return pl.pallas_call(
        _final_kernel,
        out_shape=jax.ShapeDtypeStruct((t, d), _F32),
        grid_spec=grid_spec,
        compiler_params=_params("arbitrary", disable_bounds_checks=True),
        name="combine_norm",
    )(pos, y_rows, x1, route, g.reshape(1, d))


def _routing_tables(expert, n_tokens):
    bm = MOE_ROWS
    n_assign = n_tokens * TOP_K
    n_blocks = -(-(n_assign + N_EXPERTS * (bm - 1)) // bm)
    i32 = jnp.int32
    flat_e = expert.reshape(-1)
    experts = jnp.arange(N_EXPERTS, dtype=i32)
    assign = jnp.arange(n_assign, dtype=i32)
    e_sorted, order = lax.sort((flat_e, assign), num_keys=1)
    start = jnp.sum((e_sorted[:, None] < experts[None, :]).astype(i32), axis=0)
    counts = jnp.concatenate([start[1:], jnp.full((1,), n_assign, i32)]) - start
    padded = (counts + bm - 1) // bm * bm
    pend = jnp.cumsum(padded)
    pstart = pend - padded
    shift = pstart - start
    dshift = shift - jnp.concatenate([jnp.zeros((1,), i32), shift[:-1]])
    shift_sorted = jnp.sum(jnp.where(assign[:, None] >= start[None, :], dshift[None, :], 0), axis=1)
    _, pos = lax.sort((order, assign + shift_sorted), num_keys=1)
    block_row0 = jnp.arange(n_blocks, dtype=i32) * bm
    block_e = jnp.minimum(jnp.sum((pend[None, :] <= block_row0[:, None]).astype(i32), axis=1), N_EXPERTS - 1)
    onehot = block_e[:, None] == experts[None, :]

    def per_block(table):
        return jnp.sum(jnp.where(onehot, table[None, :], 0), axis=1)

    block_start = per_block(start - pstart) + block_row0
    block_end = per_block(start + counts)
    src = block_start[:, None] + jnp.arange(bm, dtype=i32)[None, :]
    valid = src < block_end[:, None]
    tok_sorted = order // TOP_K
    row_tok = jnp.where(valid, tok_sorted[jnp.clip(src, 0, n_assign - 1)], 0)
    row_tok = jnp.concatenate([row_tok.reshape(-1), jnp.zeros((bm,), i32)])
    nonempty = counts > 0
    ordinal = jnp.cumsum(nonempty.astype(i32)) - 1
    later = jnp.where(nonempty, experts, N_EXPERTS)
    next_nonempty = jnp.concatenate([lax.cummin(later, reverse=True)[1:], jnp.full((1,), N_EXPERTS, i32)])
    next_nonempty = jnp.where(next_nonempty >= N_EXPERTS, -1, next_nonempty)
    n_used = (pend[-1] // bm).astype(i32).reshape(1)
    tables = (block_e, per_block(next_nonempty).astype(i32), per_block(ordinal).astype(i32),
              row_tok.astype(i32), n_used)
    return tables, pos.astype(i32)


def kernel(x, attn_norm, w_in, conv_dw, conv_dw_b, conv_ln_g, conv_ln_b, cmp_pos_k, cmp_k_w1, cmp_k_w2, cmp_pos_v, cmp_v_w1, cmp_v_w2, w_proj_conv, w_proj_nsa, w_merge, b_merge, w_out, ffn_norm, w_grp, b_grp, w_exp, b_exp, exp_w1, exp_w3, exp_w2, final_norm):
    batch, seq, d = x.shape
    t = batch * seq
    assert w_in.shape[0] == 1, "the block is specified with a single layer"
    xf = x.reshape(t, d)

    w_in_b = w_in[0].astype(_BF16)
    w_gate = w_in[0][:, 2 * CONV_CHANNELS + HEAD_COLS:].reshape(d, 3, N_KV_GROUPS, GROUP_SIZE)
    w_gate = w_gate.transpose(0, 2, 1, 3).reshape(d, GATE_COLS).astype(_BF16)

    h = _rmsnorm(xf, attn_norm[0], _BF16)
    glu = _glu_proj(h, w_in_b)
    tables = _rope_tables(seq)
    heads = _heads_proj(h, w_in_b, tables, seq)
    cmp_flat = _cmp_proj(h, w_in_b, tables, seq)
    gates = _gate_proj(h, w_gate).reshape(t, N_KV_GROUPS, 3 * GROUP_SIZE).transpose(1, 0, 2)

    half = CMP_STRIDE * HEAD_DIM
    pos = jnp.stack([cmp_pos_k[0], cmp_pos_v[0]]).reshape(2, 2, 1, half)
    cw1 = jnp.stack([cmp_k_w1[0], cmp_v_w1[0]]).astype(_BF16)
    cw2 = jnp.stack([cmp_k_w2[0], cmp_v_w2[0]]).astype(_BF16)
    cmp_kv = _compress(cmp_flat, pos, cw1, cw2, batch, seq)

    c = _conv(glu, conv_dw[0], conv_dw_b[0], conv_ln_g[0], conv_ln_b[0], batch, seq)
    o = _nsa(heads, cmp_kv, gates, batch, seq)
    mixed = _mix(c, o, h, w_proj_conv[0].astype(_BF16), w_proj_nsa[0].astype(_BF16),
                 w_merge[0].astype(_BF16), b_merge[0])

    w_route = jnp.concatenate(
        [w_grp[0], w_exp[0], jnp.zeros((d, LANES - N_GROUPS - N_EXPERTS), _F32)], axis=1)
    b_route = jnp.concatenate(
        [b_grp[0], b_exp[0], jnp.zeros((LANES - N_GROUPS - N_EXPERTS,), _F32)]).reshape(1, LANES)
    x1, h2, route = _out_router(mixed, xf, w_out[0].astype(_BF16), ffn_norm[0], w_route, b_route)

    expert = route[:, :TOP_K].astype(jnp.int32)
    moe_tables, pos_rows = _routing_tables(expert, t)
    y_rows = _moe(h2, moe_tables, exp_w1[0], exp_w3[0], exp_w2[0])
    out = _final(pos_rows, y_rows, x1, route, final_norm)
    return out.reshape(batch, seq, d)
```

```python
import numpy as np
import jax
import jax.numpy as jnp
from jax import lax
from jax.experimental import pallas as pl
from jax.experimental.pallas import tpu as pltpu

D_MODEL = 2048
CONV_CHANNELS = D_MODEL // 2
DW_TAPS = 31
N_HEADS = 16
HEAD_DIM = 128
N_KV_GROUPS = 4
GROUP_SIZE = N_HEADS // N_KV_GROUPS
ROT_DIM = HEAD_DIM // 4
ROPE_THETA = 500000.0
CMP_BLOCK = 32
CMP_STRIDE = 16
CMP_HIDDEN = 256
SEL_BLOCK = 64
N_SELECT = 16
N_FORCED_LOCAL = 2
WINDOW = 512
N_GROUPS = 8
EXPERTS_PER_GROUP = 8
N_EXPERTS = N_GROUPS * EXPERTS_PER_GROUP
TOP_K = 2
EXPERT_FF = 512
Q_COLS = N_HEADS * HEAD_DIM
KV_COLS = N_KV_GROUPS * HEAD_DIM
HEAD_COLS = Q_COLS + 6 * KV_COLS
GATE_COLS = 3 * N_HEADS
NORM_EPS = 1e-6
NEG_INF = -1e30
FORCE_BONUS = 1e4
LOG2_E = 1.4426950408889634

LANES = 128
SUBLANES = 8
VMEM_LIMIT_BYTES = 56 * 1024 * 1024

ROW_TILE = 512
HEAD_TILE = GROUP_SIZE * HEAD_DIM
Q_TILE = 128
SEL_CHUNK = 512
CONV_TILE = 256
CONV_HALO = 32
CONV_ROWS = 16
OUT_TILE = 512
MOE_ROWS = 256
FINAL_TILE = 128
DMA_UNROLL = 8

_BF16 = jnp.bfloat16
_F32 = jnp.float32


def _params(*sem, **kw):
    return pltpu.CompilerParams(dimension_semantics=sem, vmem_limit_bytes=VMEM_LIMIT_BYTES, **kw)


def _dot(a, b):
    return jnp.dot(a, b, preferred_element_type=_F32)


def _dot_nt(a, b):
    return lax.dot_general(a, b, (((1,), (1,)), ((), ())), preferred_element_type=_F32)


def _sigmoid(x):
    return 1.0 / (1.0 + jnp.exp(-x))


def _rmsnorm_kernel(x_ref, g_ref, o_ref):
    x = x_ref[...]
    ms = jnp.mean(x * x, axis=-1, keepdims=True)
    o_ref[...] = (x * lax.rsqrt(ms + NORM_EPS) * g_ref[...]).astype(o_ref.dtype)


def _rmsnorm(x, g, out_dtype):
    t, d = x.shape
    return pl.pallas_call(
        _rmsnorm_kernel,
        out_shape=jax.ShapeDtypeStruct((t, d), out_dtype),
        grid=(t // ROW_TILE,),
        in_specs=[pl.BlockSpec((ROW_TILE, d), lambda i: (i, 0)),
                  pl.BlockSpec((1, d), lambda i: (0, 0))],
        out_specs=pl.BlockSpec((ROW_TILE, d), lambda i: (i, 0)),
        compiler_params=_params("parallel"),
        name="rmsnorm",
    )(x, g.reshape(1, d))


def _glu_kernel(h_ref, wa_ref, wb_ref, o_ref):
    h = h_ref[...]
    a = _dot(h, wa_ref[...])
    b = _dot(h, wb_ref[...])
    o_ref[...] = a * _sigmoid(b)


def _glu_proj(h, w_in_bf16):
    t, d = h.shape
    tn = HEAD_TILE
    nb = CONV_CHANNELS // tn
    return pl.pallas_call(
        _glu_kernel,
        out_shape=jax.ShapeDtypeStruct((t, CONV_CHANNELS), _F32),
        grid=(t // ROW_TILE, nb),
        in_specs=[pl.BlockSpec((ROW_TILE, d), lambda i, j: (i, 0)),
                  pl.BlockSpec((d, tn), lambda i, j: (0, j)),
                  pl.BlockSpec((d, tn), lambda i, j: (0, j + nb))],
        out_specs=pl.BlockSpec((ROW_TILE, tn), lambda i, j: (i, j)),
        compiler_params=_params("parallel", "arbitrary"),
        name="glu_proj",
    )(h, w_in_bf16, w_in_bf16)


N_Q_SLABS = Q_COLS // HEAD_TILE
N_HEAD_SLABS = N_Q_SLABS + 4
Q_COL_BLOCK = 2 * CONV_CHANNELS // HEAD_TILE
CMP_COL_BLOCK = Q_COL_BLOCK + N_Q_SLABS
KV_COL_BLOCK = CMP_COL_BLOCK + 2


def _rope_tables(seq):
    pos = jnp.arange(seq, dtype=_F32)
    inv = ROPE_THETA ** (-jnp.arange(0, ROT_DIM, 2, dtype=_F32) / ROT_DIM)
    ang = pos[:, None] * inv[None, :]
    cos, sin = jnp.cos(ang), jnp.sin(ang)
    half = ROT_DIM // 2
    ones = jnp.ones((seq, HEAD_DIM - ROT_DIM), _F32)
    zeros_h = jnp.zeros((seq, half), _F32)
    zeros_r = jnp.zeros((seq, HEAD_DIM - ROT_DIM), _F32)
    c = jnp.concatenate([cos, cos, ones], axis=-1)
    s_lo = jnp.concatenate([zeros_h, sin, zeros_r], axis=-1)
    s_hi = jnp.concatenate([-sin, zeros_h, zeros_r], axis=-1)
    return c, s_lo, s_hi


def _rotate_head(xs, c, slo, shi):
    half = ROT_DIM // 2
    return xs * c + pltpu.roll(xs, half, 1) * slo + pltpu.roll(xs, HEAD_DIM - half, 1) * shi


def _heads_kernel(h_ref, w_ref, c_ref, slo_ref, shi_ref, o_ref):
    j = pl.program_id(1)
    acc = _dot(h_ref[...], w_ref[...])
    is_q = j < N_Q_SLABS
    rotate = is_q | (j % 2 == 0)

    @pl.when(rotate)
    def _():
        scale = jnp.where(is_q, HEAD_DIM ** -0.5 * LOG2_E, 1.0).astype(_F32)
        c, slo, shi = c_ref[...], slo_ref[...], shi_ref[...]
        for hd in range(GROUP_SIZE):
            r = _rotate_head(acc[:, hd * HEAD_DIM:(hd + 1) * HEAD_DIM], c, slo, shi)
            o_ref[0, hd] = (r * scale).astype(o_ref.dtype)

    @pl.when(jnp.logical_not(rotate))
    def _():
        for hd in range(GROUP_SIZE):
            o_ref[0, hd] = acc[:, hd * HEAD_DIM:(hd + 1) * HEAD_DIM].astype(o_ref.dtype)


def _heads_proj(h, w_in_bf16, tables, seq):
    t, d = h.shape
    tm = ROW_TILE
    spt = seq // tm
    tab = pl.BlockSpec((tm, HEAD_DIM), lambda i, j: (i % spt, 0))

    def w_block(i, j):
        return (0, jnp.where(j < N_Q_SLABS, j + Q_COL_BLOCK, j - N_Q_SLABS + KV_COL_BLOCK))

    return pl.pallas_call(
        _heads_kernel,
        out_shape=jax.ShapeDtypeStruct((N_HEAD_SLABS, GROUP_SIZE, t, HEAD_DIM), _BF16),
        grid=(t // tm, N_HEAD_SLABS),
        in_specs=[pl.BlockSpec((tm, d), lambda i, j: (i, 0)),
                  pl.BlockSpec((d, HEAD_TILE), w_block),
                  tab, tab, tab],
        out_specs=pl.BlockSpec((1, GROUP_SIZE, tm, HEAD_DIM), lambda i, j: (j, 0, i, 0)),
        compiler_params=_params("parallel", "arbitrary"),
        name="heads_proj",
    )(h, w_in_bf16, *tables)


def _cmp_proj_kernel(h_ref, w_ref, c_ref, slo_ref, shi_ref, o_ref, stage_ref):
    j = pl.program_id(1)
    acc = _dot(h_ref[...], w_ref[...])

    @pl.when(j == 0)
    def _():
        c, slo, shi = c_ref[...], slo_ref[...], shi_ref[...]
        for g in range(N_KV_GROUPS):
            stage_ref[g] = _rotate_head(acc[:, g * HEAD_DIM:(g + 1) * HEAD_DIM], c, slo, shi)

    @pl.when(j != 0)
    def _():
        for g in range(N_KV_GROUPS):
            stage_ref[g] = acc[:, g * HEAD_DIM:(g + 1) * HEAD_DIM]

    n_chunk = stage_ref.shape[1] // CMP_STRIDE
    for g in range(N_KV_GROUPS):
        for l in range(CMP_STRIDE):
            rows = stage_ref[g, pl.ds(l, n_chunk, stride=CMP_STRIDE), :]
            o_ref[0, g, :, l * HEAD_DIM:(l + 1) * HEAD_DIM] = rows.astype(o_ref.dtype)


def _cmp_proj(h, w_in_bf16, tables, seq):
    t, d = h.shape
    tm = ROW_TILE
    spt = seq // tm
    tab = pl.BlockSpec((tm, HEAD_DIM), lambda i, j: (i % spt, 0))
    return pl.pallas_call(
        _cmp_proj_kernel,
        out_shape=jax.ShapeDtypeStruct((2, N_KV_GROUPS, t // CMP_STRIDE, CMP_STRIDE * HEAD_DIM), _BF16),
        grid=(t // tm, 2),
        in_specs=[pl.BlockSpec((tm, d), lambda i, j: (i, 0)),
                  pl.BlockSpec((d, HEAD_TILE), lambda i, j: (0, j + CMP_COL_BLOCK)),
                  tab, tab, tab],
        out_specs=pl.BlockSpec((1, N_KV_GROUPS, tm // CMP_STRIDE, CMP_STRIDE * HEAD_DIM),
                               lambda i, j: (j, 0, i, 0)),
        scratch_shapes=[pltpu.VMEM((N_KV_GROUPS, tm, HEAD_DIM), _F32)],
        compiler_params=_params("parallel", "arbitrary"),
        name="cmp_proj",
    )(h, w_in_bf16, *tables)


def _gate_kernel(h_ref, w_ref, o_ref):
    o_ref[...] = _sigmoid(_dot(h_ref[...], w_ref[...]))


def _gate_proj(h, w_gate_bf16):
    t, d = h.shape
    n = w_gate_bf16.shape[1]
    return pl.pallas_call(
        _gate_kernel,
        out_shape=jax.ShapeDtypeStruct((t, n), _F32),
        grid=(t // ROW_TILE,),
        in_specs=[pl.BlockSpec((ROW_TILE, d), lambda i: (i, 0)),
                  pl.BlockSpec((d, n), lambda i: (0, 0))],
        out_specs=pl.BlockSpec((ROW_TILE, n), lambda i: (i, 0)),
        compiler_params=_params("parallel"),
        name="gate_proj",
    )(h, w_gate_bf16)


def _compress_kernel(kv_ref, pos_ref, w1_ref, w2_ref, o_ref):
    half = CMP_STRIDE * HEAD_DIM
    c = kv_ref[0, 0].astype(_F32)
    n_chunk = c.shape[0]
    top = (c + pos_ref[0, 0]).astype(_BF16)
    bot = (c + pos_ref[0, 1]).astype(_BF16)
    u = _dot(top, w1_ref[0, :half, :])
    v = _dot(bot, w1_ref[0, half:, :])
    hidden = u + pltpu.roll(v, n_chunk - 1, 0)
    act = jax.nn.gelu(hidden).astype(_BF16)
    o_ref[0, 0, 0] = _dot(act, w2_ref[0]).astype(o_ref.dtype)


def _compress(cmp_flat, pos, w1, w2, batch, seq):
    n_chunk = seq // CMP_STRIDE
    return pl.pallas_call(
        _compress_kernel,
        out_shape=jax.ShapeDtypeStruct((2, batch, N_KV_GROUPS, n_chunk, HEAD_DIM), _BF16),
        grid=(2, batch, N_KV_GROUPS),
        in_specs=[pl.BlockSpec((1, 1, n_chunk, CMP_STRIDE * HEAD_DIM), lambda s, b, g: (s, g, b, 0)),
                  pl.BlockSpec((1, 2, 1, CMP_STRIDE * HEAD_DIM), lambda s, b, g: (s, 0, 0, 0)),
                  pl.BlockSpec((1, CMP_BLOCK * HEAD_DIM, CMP_HIDDEN), lambda s, b, g: (s, 0, 0)),
                  pl.BlockSpec((1, CMP_HIDDEN, HEAD_DIM), lambda s, b, g: (s, 0, 0))],
        out_specs=pl.BlockSpec((1, 1, 1, n_chunk, HEAD_DIM), lambda s, b, g: (s, b, g, 0, 0)),
        compiler_params=_params("parallel", "parallel", "parallel"),
        name="compress",
    )(cmp_flat, pos, w1, w2)


def _conv_kernel(cur_ref, halo_ref, dw_ref, dwb_ref, g_ref, b_ref, o_ref, ext_ref, sh_ref, acc_ref):
    i = pl.program_id(1)
    ts = cur_ref.shape[0]
    ext_ref[:CONV_HALO, :] = jnp.where(i > 0, halo_ref[...], 0.0)
    ext_ref[CONV_HALO:, :] = cur_ref[...]
    n_sh = sh_ref.shape[1]
    for b in range(1, SUBLANES):
        sh_ref[b - 1] = ext_ref[b:b + n_sh, :]
    off = CONV_HALO - (DW_TAPS - 1)

    def chunk(c, carry):
        r0 = pl.multiple_of(c * CONV_ROWS, CONV_ROWS)
        y = None
        for k in range(DW_TAPS):
            a, b = divmod(off + k, SUBLANES)
            src = ext_ref if b == 0 else sh_ref.at[b - 1]
            term = src[pl.ds(r0 + SUBLANES * a, CONV_ROWS), :] * dw_ref[k:k + 1, :]
            y = term if y is None else y + term
        acc_ref[pl.ds(r0, CONV_ROWS), :] = y
        return carry

    lax.fori_loop(0, ts // CONV_ROWS, chunk, 0)
    y = acc_ref[...] + dwb_ref[...]
    mu = jnp.mean(y, axis=-1, keepdims=True)
    yc = y - mu
    var = jnp.mean(yc * yc, axis=-1, keepdims=True)
    z = yc * lax.rsqrt(var + NORM_EPS) * g_ref[...] + b_ref[...]
    o_ref[...] = (z * _sigmoid(z)).astype(o_ref.dtype)


def _conv(glu, dw, dw_b, ln_g, ln_b, batch, seq):
    t, c = glu.shape
    ts = CONV_TILE
    nt = seq // ts
    hb = ts // CONV_HALO
    dw_pad = jnp.concatenate([dw, jnp.zeros((CONV_HALO - DW_TAPS, c), _F32)], axis=0)
    vec = pl.BlockSpec((1, c), lambda b, i: (0, 0))
    return pl.pallas_call(
        _conv_kernel,
        out_shape=jax.ShapeDtypeStruct((t, c), _BF16),
        grid=(batch, nt),
        in_specs=[pl.BlockSpec((ts, c), lambda b, i: (b * nt + i, 0)),
                  pl.BlockSpec((CONV_HALO, c), lambda b, i: (jnp.maximum((b * nt + i) * hb - 1, 0), 0)),
                  pl.BlockSpec((CONV_HALO, c), lambda b, i: (0, 0)),
                  vec, vec, vec],
        out_specs=pl.BlockSpec((ts, c), lambda b, i: (b * nt + i, 0)),
        scratch_shapes=[pltpu.VMEM((ts + CONV_HALO, c), _F32),
                        pltpu.VMEM((SUBLANES - 1, ts + CONV_HALO - SUBLANES, c), _F32),
                        pltpu.VMEM((ts, c), _F32)],
        compiler_params=_params("parallel", "arbitrary"),
        name="conv",
    )(glu, glu, dw_pad, dw_b.reshape(1, c), ln_g.reshape(1, c), ln_b.reshape(1, c))


def _importance_matrix(n_cmp_pad, n_blk):
    r_sel, r_cmp = SEL_BLOCK // CMP_STRIDE, CMP_BLOCK // CMP_STRIDE
    a = np.zeros((n_blk, n_cmp_pad), np.float32)
    for j in range(n_blk):
        for m in range(r_sel):
            for n in range(r_cmp):
                c = r_sel * j + m - n
                if 0 <= c < n_cmp_pad - 1:
                    a[j, c] += 1.0
    return a


def _expand_matrix(n_blk, seq):
    e = np.zeros((LANES, seq), np.float32)
    for j in range(n_blk):
        e[j, j * SEL_BLOCK:(j + 1) * SEL_BLOCK] = 1.0
    return e


def _nsa_step(n_active, q_ref, kc_ref, vc_ref, ks_ref, vs_ref, kw_ref, vw_ref, gate_ref, amat_ref, emat_ref,
              o_ref):
    tq_n = q_ref.shape[2]
    n_cmp_pad = kc_ref.shape[3]
    n_blk = amat_ref.shape[0]
    rows = GROUP_SIZE * tq_n
    start = pl.program_id(2) * tq_n
    q = q_ref[0].reshape(rows, HEAD_DIM)
    tq = start + lax.broadcasted_iota(jnp.int32, (tq_n, 1), 0)

    span = WINDOW + tq_n
    kstart = pl.multiple_of(jnp.maximum(start - WINDOW, 0), tq_n)
    kpos = kstart + lax.broadcasted_iota(jnp.int32, (1, span), 1)
    wbias = jnp.where((kpos <= tq) & (tq - kpos < WINDOW), 0.0, NEG_INF)
    sw = _dot_nt(q, kw_ref[0, 0, pl.ds(kstart, span), :]).reshape(GROUP_SIZE, tq_n, span) + wbias[None]

    cidx = lax.broadcasted_iota(jnp.int32, (1, n_cmp_pad), 1)
    cvalid = (cidx * CMP_STRIDE + (CMP_BLOCK - 1) <= tq) & (cidx < n_cmp_pad - 1)
    cbias = jnp.where(cvalid, 0.0, NEG_INF)
    s = _dot_nt(q, kc_ref[0, 0, 0]).reshape(GROUP_SIZE, tq_n, n_cmp_pad) + cbias[None]
    e_w = jnp.exp2(sw - jnp.max(sw, axis=-1, keepdims=True)).astype(_BF16)
    e = jnp.exp2(s - jnp.max(s, axis=-1, keepdims=True))
    inv_l = jnp.where((tq >= CMP_BLOCK - 1)[None], 1.0 / jnp.sum(e, axis=-1, keepdims=True), 0.0)
    p = e * inv_l
    vw_aug = jnp.concatenate([vw_ref[0, 0, pl.ds(kstart, span), :], jnp.ones((span, HEAD_DIM), _BF16)], axis=1)
    win = _dot(e_w.reshape(rows, span), vw_aug).reshape(GROUP_SIZE, tq_n, 2 * HEAD_DIM)
    o_cmp = _dot(p.reshape(rows, n_cmp_pad).astype(_BF16), vc_ref[0, 0, 0])

    imp = lax.dot_general(amat_ref[...], jnp.sum(p, axis=0), (((1,), (1,)), ((), ())),
                          preferred_element_type=_F32, precision=lax.Precision.HIGHEST)
    tq_row = start + lax.broadcasted_iota(jnp.int32, (1, tq_n), 1)
    blk = lax.broadcasted_iota(jnp.int32, (n_blk, 1), 0)
    cur = tq_row // SEL_BLOCK
    forced = (blk == 0) | ((blk <= cur) & (blk > cur - N_FORCED_LOCAL))
    imp = jnp.where(forced, imp + FORCE_BONUS, imp)
    imp = jnp.where(blk * SEL_BLOCK <= tq_row, imp, NEG_INF)
    rank = jnp.zeros((n_blk, tq_n), _F32)
    for i in range(n_blk):
        row = imp[i:i + 1, :]
        before = (row > imp) | ((row == imp) & (blk > i))
        rank = rank + jnp.where(before, 1.0, 0.0)
    sel_t = jnp.where(rank < float(min(N_SELECT, n_blk)), 1.0, 0.0)
    sel = jnp.concatenate([sel_t, jnp.zeros((LANES - n_blk, tq_n), _F32)], axis=0).T.astype(_BF16)

    ones_c = jnp.ones((SEL_CHUNK, HEAD_DIM), _BF16)

    def scores(c):
        return _dot_nt(q, ks_ref[0, 0, c * SEL_CHUNK:(c + 1) * SEL_CHUNK, :]).reshape(GROUP_SIZE, tq_n, SEL_CHUNK)

    m_run = None
    acc = None
    sc_next = scores(0)
    for c in range(n_active):
        lo = c * SEL_CHUNK
        sc_raw = sc_next
        if c + 1 < n_active:
            sc_next = scores(c + 1)
        chosen = _dot(sel, emat_ref[:, lo:lo + SEL_CHUNK]) > 0.5
        if c == n_active - 1:
            kpos = lo + lax.broadcasted_iota(jnp.int32, (1, SEL_CHUNK), 1)
            chosen = chosen & (kpos <= tq)
        sc = sc_raw + jnp.where(chosen, 0.0, NEG_INF)[None]
        m_c = jnp.max(sc, axis=-1, keepdims=True)
        m_new = m_c if c == 0 else jnp.maximum(m_run, m_c)
        e_c = jnp.exp2(sc - m_new).astype(_BF16)
        v_aug = jnp.concatenate([vs_ref[0, 0, lo:lo + SEL_CHUNK, :], ones_c], axis=1)
        pv = _dot(e_c.reshape(rows, SEL_CHUNK), v_aug).reshape(GROUP_SIZE, tq_n, 2 * HEAD_DIM)
        acc = pv if c == 0 else jnp.exp2(m_run - m_new) * acc + pv
        m_run = m_new

    o_cmp = o_cmp.reshape(GROUP_SIZE, tq_n, HEAD_DIM)
    gates = gate_ref[0]
    for r in range(GROUP_SIZE):
        g_cmp = gates[:, r:r + 1]
        g_slc = gates[:, GROUP_SIZE + r:GROUP_SIZE + r + 1] / acc[r, :, HEAD_DIM:]
        g_win = gates[:, 2 * GROUP_SIZE + r:2 * GROUP_SIZE + r + 1] / win[r, :, HEAD_DIM:]
        o = g_cmp * o_cmp[r] + g_slc * acc[r, :, :HEAD_DIM] + g_win * win[r, :, :HEAD_DIM]
        o_ref[:, r * HEAD_DIM:(r + 1) * HEAD_DIM] = o.astype(o_ref.dtype)


def _nsa_kernel(q_ref, kc_ref, vc_ref, ks_ref, vs_ref, kw_ref, vw_ref, gate_ref, amat_ref, emat_ref, o_ref):
    tq_n = q_ref.shape[2]
    seq = ks_ref.shape[2]
    n_active = (pl.program_id(2) * tq_n + tq_n - 1) // SEL_CHUNK + 1
    for k in range(1, seq // SEL_CHUNK + 1):
        @pl.when(n_active == k)
        def _(k=k):
            _nsa_step(k, q_ref, kc_ref, vc_ref, ks_ref, vs_ref, kw_ref, vw_ref, gate_ref, amat_ref, emat_ref,
                      o_ref)


def _nsa(heads, cmp_kv, gates, batch, seq):
    t = heads.shape[2]
    tq = Q_TILE
    nq = seq // tq
    n_chunk = cmp_kv.shape[3]
    n_blk = seq // SEL_BLOCK
    assert n_chunk == LANES and n_blk <= LANES and seq % SEL_CHUNK == 0 and WINDOW % tq == 0
    amat = jnp.asarray(_importance_matrix(n_chunk, n_blk))
    emat = jnp.asarray(_expand_matrix(n_blk, seq), dtype=_BF16)

    def kv_spec(slab):
        return pl.BlockSpec((1, 1, seq, HEAD_DIM), lambda b, g, i: (slab, g, b, 0))

    def cmp_spec(s):
        return pl.BlockSpec((1, 1, 1, n_chunk, HEAD_DIM), lambda b, g, i: (s, b, g, 0, 0))

    return pl.pallas_call(
        _nsa_kernel,
        out_shape=jax.ShapeDtypeStruct((t, Q_COLS), _BF16),
        grid=(batch, N_KV_GROUPS, nq),
        in_specs=[pl.BlockSpec((1, GROUP_SIZE, tq, HEAD_DIM), lambda b, g, i: (g, 0, b * nq + i, 0)),
                  cmp_spec(0), cmp_spec(1),
                  kv_spec(N_Q_SLABS), kv_spec(N_Q_SLABS + 1),
                  kv_spec(N_Q_SLABS + 2), kv_spec(N_Q_SLABS + 3),
                  pl.BlockSpec((1, tq, 3 * GROUP_SIZE), lambda b, g, i: (g, b * nq + i, 0)),
                  pl.BlockSpec((n_blk, n_chunk), lambda b, g, i: (0, 0)),
                  pl.BlockSpec((LANES, seq), lambda b, g, i: (0, 0))],
        out_specs=pl.BlockSpec((tq, HEAD_TILE), lambda b, g, i: (b * nq + i, g)),
        compiler_params=_params("parallel", "parallel", "arbitrary"),
        name="nsa",
    )(heads, cmp_kv, cmp_kv, heads, heads, heads, heads, gates, amat, emat)


def _mix_kernel(c_ref, o_ref, h_ref, wpc_ref, wpn_ref, wmc_ref, wmn_ref, bc_ref, bn_ref, out_ref):
    h = h_ref[...]
    g_conv = _sigmoid(_dot(h, wmc_ref[...]) + bc_ref[...])
    g_nsa = _sigmoid(_dot(h, wmn_ref[...]) + bn_ref[...])
    y_conv = _dot(c_ref[...], wpc_ref[...])
    y_nsa = _dot(o_ref[...], wpn_ref[...])
    out_ref[...] = (g_conv * y_conv + g_nsa * y_nsa).astype(out_ref.dtype)


def _mix(c, o, h, wpc, wpn, wm, bm):
    t, d = h.shape
    tm, tn = ROW_TILE, HEAD_TILE
    nb = d // tn
    bm2 = bm.reshape(1, 2 * d)
    return pl.pallas_call(
        _mix_kernel,
        out_shape=jax.ShapeDtypeStruct((t, d), _BF16),
        grid=(t // tm, nb),
        in_specs=[pl.BlockSpec((tm, CONV_CHANNELS), lambda i, j: (i, 0)),
                  pl.BlockSpec((tm, Q_COLS), lambda i, j: (i, 0)),
                  pl.BlockSpec((tm, d), lambda i, j: (i, 0)),
                  pl.BlockSpec((CONV_CHANNELS, tn), lambda i, j: (0, j)),
                  pl.BlockSpec((Q_COLS, tn), lambda i, j: (0, j)),
                  pl.BlockSpec((d, tn), lambda i, j: (0, j)),
                  pl.BlockSpec((d, tn), lambda i, j: (0, j + nb)),
                  pl.BlockSpec((1, tn), lambda i, j: (0, j)),
                  pl.BlockSpec((1, tn), lambda i, j: (0, j + nb))],
        out_specs=pl.BlockSpec((tm, tn), lambda i, j: (i, j)),
        compiler_params=_params("parallel", "arbitrary"),
        name="mix",
    )(c, o, h, wpc, wpn, wm, wm, bm2, bm2)


def _out_router_kernel(mix_ref, x_ref, wout_ref, g_ref, wrh_ref, wrl_ref, br_ref, x1_ref, h2_ref, route_ref):
    x1 = x_ref[...] + _dot(mix_ref[...], wout_ref[...])
    x1_ref[...] = x1
    ms = jnp.mean(x1 * x1, axis=-1, keepdims=True)
    h2 = x1 * lax.rsqrt(ms + NORM_EPS) * g_ref[...]
    h2_ref[...] = h2
    h_hi = h2.astype(_BF16)
    h_lo = (h2 - h_hi.astype(_F32)).astype(_BF16)
    logits = (_dot(h_hi, wrh_ref[...]) + _dot(h_lo, wrh_ref[...]) + _dot(h_hi, wrl_ref[...])
              + br_ref[...])
    lane = lax.broadcasted_iota(jnp.int32, (1, LANES), 1)
    lane_f = lane.astype(_F32)
    big = float(LANES)
    gl = jnp.where(lane < N_GROUPS, logits, NEG_INF)
    gmax = jnp.max(gl, axis=-1, keepdims=True)
    p_g = 1.0 / jnp.sum(jnp.where(lane < N_GROUPS, jnp.exp(gl - gmax), 0.0), axis=-1, keepdims=True)
    g_sel = jnp.min(jnp.where(gl == gmax, lane_f, big), axis=-1, keepdims=True)
    lo = N_GROUPS + g_sel * EXPERTS_PER_GROUP
    in_grp = (lane_f >= lo) & (lane_f < lo + EXPERTS_PER_GROUP)
    el = jnp.where(in_grp, logits, NEG_INF)
    v0 = jnp.max(el, axis=-1, keepdims=True)
    i0 = jnp.min(jnp.where(el == v0, lane_f, big), axis=-1, keepdims=True)
    el2 = jnp.where(lane_f == i0, NEG_INF, el)
    v1 = jnp.max(el2, axis=-1, keepdims=True)
    i1 = jnp.min(jnp.where(el2 == v1, lane_f, big), axis=-1, keepdims=True)
    t1 = jnp.exp(v1 - v0)
    w0 = p_g / (1.0 + t1)
    w1 = p_g * t1 / (1.0 + t1)
    route = jnp.where(lane == 0, i0 - N_GROUPS, 0.0)
    route = jnp.where(lane == 1, i1 - N_GROUPS, route)
    route = jnp.where(lane == 2, w0, route)
    route = jnp.where(lane == 3, w1, route)
    route_ref[...] = route


def _out_router(mixed, x, w_out, ffn_g, w_route, b_route):
    t, d = x.shape
    tm = OUT_TILE
    row = pl.BlockSpec((tm, d), lambda i: (i, 0))
    w_route_hi = w_route.astype(_BF16)
    w_route_lo = (w_route - w_route_hi.astype(_F32)).astype(_BF16)
    return pl.pallas_call(
        _out_router_kernel,
        out_shape=(jax.ShapeDtypeStruct((t, d), _F32),
                   jax.ShapeDtypeStruct((t, d), _F32),
                   jax.ShapeDtypeStruct((t, LANES), _F32)),
        grid=(t // tm,),
        in_specs=[row, row,
                  pl.BlockSpec((d, d), lambda i: (0, 0), pipeline_mode=pl.Buffered(1)),
                  pl.BlockSpec((1, d), lambda i: (0, 0)),
                  pl.BlockSpec((d, LANES), lambda i: (0, 0)),
                  pl.BlockSpec((d, LANES), lambda i: (0, 0)),
                  pl.BlockSpec((1, LANES), lambda i: (0, 0))],
        out_specs=(row, row, pl.BlockSpec((tm, LANES), lambda i: (i, 0))),
        compiler_params=_params("parallel"),
        name="out_router",
    )(mixed, x, w_out, ffn_g.reshape(1, d), w_route_hi, w_route_lo, b_route)


def _moe_kernel(be_ref, nxt_ref, ord_ref, tok_ref, nb_ref, h2_hbm, w1_hbm, w3_hbm, w2_hbm, y_ref,
                xbuf, wf1, wf3, wf2, w1b, w3b, w2b, xsem, wsem):
    i = pl.program_id(0)
    nb = nb_ref[0]
    bm = xbuf.shape[1]

    def row_copy(tok, r, sl):
        return pltpu.make_async_copy(h2_hbm.at[pl.ds(tok, 1), :], xbuf.at[sl, pl.ds(r, 1), :], xsem.at[sl])

    def rows_wait(sl):
        pltpu.make_async_copy(h2_hbm.at[pl.ds(0, bm), :], xbuf.at[sl], xsem.at[sl]).wait()

    def weight_copies(e, ws):
        return (pltpu.make_async_copy(w1_hbm.at[e], wf1.at[ws], wsem.at[ws]),
                pltpu.make_async_copy(w3_hbm.at[e], wf3.at[ws], wsem.at[ws]),
                pltpu.make_async_copy(w2_hbm.at[e], wf2.at[ws], wsem.at[ws]))

    @pl.when(i == 0)
    def _():
        def body(r, carry):
            row_copy(tok_ref[r], r, 0).start()
            return carry
        lax.fori_loop(0, bm, body, 0, unroll=DMA_UNROLL)
        for cp in weight_copies(be_ref[0], 0):
            cp.start()

    def step(par):
        rows_wait(par)
        e = be_ref[i]
        ws = ord_ref[i] % 2
        first = (i == 0) | (e != be_ref[jnp.maximum(i - 1, 0)])

        @pl.when(first)
        def _():
            for cp in weight_copies(e, ws):
                cp.wait()
            nxt = nxt_ref[i]

            @pl.when(nxt >= 0)
            def _():
                for cp in weight_copies(nxt, 1 - ws):
                    cp.start(priority=1)

            w1b[...] = wf1[ws].astype(_BF16)
            w3b[...] = wf3[ws].astype(_BF16)
            w2b[...] = wf2[ws].astype(_BF16)

        base = (i + 1) * bm
        for r in range(bm):
            row_copy(tok_ref[base + r], r, 1 - par).start()
        x = xbuf[par].astype(_BF16)
        a = _dot(x, w1b[...])
        b = _dot(x, w3b[...])
        act = (a * _sigmoid(a) * b).astype(_BF16)
        y_ref[...] = _dot(act, w2b[...])

    for par in range(2):
        @pl.when((i < nb) & (i % 2 == par))
        def _(par=par):
            step(par)

        @pl.when((i == nb) & (i % 2 == par))
        def _(par=par):
            rows_wait(par)

    @pl.when(i >= nb)
    def _():
        y_ref[...] = jnp.zeros(y_ref.shape, y_ref.dtype)


def _moe(h2, tables, w1, w3, w2):
    t, d = h2.shape
    bm = MOE_ROWS
    block_e, block_next, block_ord, row_tok, n_used = tables
    n_steps = block_e.shape[0] + 1
    ff = w1.shape[2]
    grid_spec = pltpu.PrefetchScalarGridSpec(
        num_scalar_prefetch=5,
        grid=(n_steps,),
        in_specs=[pl.BlockSpec(memory_space=pl.ANY)] * 4,
        out_specs=pl.BlockSpec((bm, d), lambda i, *_: (i, 0)),
        scratch_shapes=[pltpu.VMEM((2, bm, d), _F32),
                        pltpu.VMEM((2, d, ff), _F32),
                        pltpu.VMEM((2, d, ff), _F32),
                        pltpu.VMEM((2, ff, d), _F32),
                        pltpu.VMEM((d, ff), _BF16),
                        pltpu.VMEM((d, ff), _BF16),
                        pltpu.VMEM((ff, d), _BF16),
                        pltpu.SemaphoreType.DMA((2,)),
                        pltpu.SemaphoreType.DMA((2,))],
    )
    return pl.pallas_call(
        _moe_kernel,
        out_shape=jax.ShapeDtypeStruct((n_steps * bm, d), _F32),
        grid_spec=grid_spec,
        compiler_params=_params("arbitrary", disable_bounds_checks=True),
        name="moe_experts",
    )(block_e, block_next, block_ord, row_tok, n_used, h2, w1, w3, w2)


def _final_kernel(pos_ref, y_hbm, x1_ref, route_ref, g_ref, o_ref, ybuf, sem):
    i = pl.program_id(0)
    n = pl.num_programs(0)
    tm = x1_ref.shape[0]

    def row_copy(p, r, k, sl):
        return pltpu.make_async_copy(y_hbm.at[pl.ds(p, 1), :], ybuf.at[sl, k, pl.ds(r, 1), :], sem.at[sl])

    @pl.when(i == 0)
    def _():
        def body(r, carry):
            for k in range(TOP_K):
                row_copy(pos_ref[r * TOP_K + k], r, k, 0).start()
            return carry
        lax.fori_loop(0, tm, body, 0, unroll=DMA_UNROLL)

    def step(par, issue_next):
        if issue_next:
            base = (i + 1) * tm * TOP_K
            for r in range(tm):
                for k in range(TOP_K):
                    row_copy(pos_ref[base + r * TOP_K + k], r, k, 1 - par).start()
        for k in range(TOP_K):
            pltpu.make_async_copy(y_hbm.at[pl.ds(0, tm), :], ybuf.at[par, k], sem.at[par]).wait()
        route = route_ref[...]
        x = x1_ref[...] + route[:, 2:3] * ybuf[par, 0] + route[:, 3:4] * ybuf[par, 1]
        ms = jnp.mean(x * x, axis=-1, keepdims=True)
        o_ref[...] = x * lax.rsqrt(ms + NORM_EPS) * g_ref[...]

    for par in range(2):
        @pl.when((i % 2 == par) & (i + 1 < n))
        def _(par=par):
            step(par, True)

        @pl.when((i % 2 == par) & (i + 1 == n))
        def _(par=par):
            step(par, False)


def _final(pos, y_rows, x1, route, g):
    t, d = x1.shape
    tm = FINAL_TILE
    grid_spec = pltpu.PrefetchScalarGridSpec(
        num_scalar_prefetch=1,
        grid=(t // tm,),
        in_specs=[pl.BlockSpec(memory_space=pl.ANY),
                  pl.BlockSpec((tm, d), lambda i, pos: (i, 0)),
                  pl.BlockSpec((tm, LANES), lambda i, pos: (i, 0)),
                  pl.BlockSpec((1, d), lambda i, pos: (0, 0))],
        out_specs=pl.BlockSpec((tm, d), lambda i, pos: (i, 0)),
        scratch_shapes=[pltpu.VMEM((2, TOP_K, tm, d), _F32),
                        pltpu.SemaphoreType.DMA((2,))],
    )
    return pl.pallas_call(
        _final_kernel,
        out_shape=jax.ShapeDtypeStruct((t, d), _F32),
        grid_spec=grid_spec,
        compiler_params=_params("arbitrary", disable_bounds_checks=True),
        name="combine_norm",
    )(pos, y_rows, x1, route, g.reshape(1, d))


def _routing_tables(expert, n_tokens):
    bm = MOE_ROWS
    n_assign = n_tokens * TOP_K
    n_blocks = -(-(n_assign + N_EXPERTS * (bm - 1)) // bm)
    i32 = jnp.int32
    flat_e = expert.reshape(-1)
    experts = jnp.arange(N_EXPERTS, dtype=i32)
    assign = jnp.arange(n_assign, dtype=i32)
    e_sorted, order = lax.sort((flat_e, assign), num_keys=1)
    start = jnp.sum((e_sorted[:, None] < experts[None, :]).astype(i32), axis=0)
    counts = jnp.concatenate([start[1:], jnp.full((1,), n_assign, i32)]) - start
    padded = (counts + bm - 1) // bm * bm
    pend = jnp.cumsum(padded)
    pstart = pend - padded
    shift = pstart - start
    dshift = shift - jnp.concatenate([jnp.zeros((1,), i32), shift[:-1]])
    shift_sorted = jnp.sum(jnp.where(assign[:, None] >= start[None, :], dshift[None, :], 0), axis=1)
    _, pos = lax.sort((order, assign + shift_sorted), num_keys=1)
    block_row0 = jnp.arange(n_blocks, dtype=i32) * bm
    block_e = jnp.minimum(jnp.sum((pend[None, :] <= block_row0[:, None]).astype(i32), axis=1), N_EXPERTS - 1)
    onehot = block_e[:, None] == experts[None, :]

    def per_block(table):
        return jnp.sum(jnp.where(onehot, table[None, :], 0), axis=1)

    block_start = per_block(start - pstart) + block_row0
    block_end = per_block(start + counts)
    src = block_start[:, None] + jnp.arange(bm, dtype=i32)[None, :]
    valid = src < block_end[:, None]
    tok_sorted = order // TOP_K
    row_tok = jnp.where(valid, tok_sorted[jnp.clip(src, 0, n_assign - 1)], 0)
    row_tok = jnp.concatenate([row_tok.reshape(-1), jnp.zeros((bm,), i32)])
    nonempty = counts > 0
    ordinal = jnp.cumsum(nonempty.astype(i32)) - 1
    later = jnp.where(nonempty, experts, N_EXPERTS)
    next_nonempty = jnp.concatenate([lax.cummin(later, reverse=True)[1:], jnp.full((1,), N_EXPERTS, i32)])
    next_nonempty = jnp.where(next_nonempty >= N_EXPERTS, -1, next_nonempty)
    n_used = (pend[-1] // bm).astype(i32).reshape(1)
    tables = (block_e, per_block(next_nonempty).astype(i32), per_block(ordinal).astype(i32),
              row_tok.astype(i32), n_used)
    return tables, pos.astype(i32)


def kernel(x, attn_norm, w_in, conv_dw, conv_dw_b, conv_ln_g, conv_ln_b, cmp_pos_k, cmp_k_w1, cmp_k_w2, cmp_pos_v, cmp_v_w1, cmp_v_w2, w_proj_conv, w_proj_nsa, w_merge, b_merge, w_out, ffn_norm, w_grp, b_grp, w_exp, b_exp, exp_w1, exp_w3, exp_w2, final_norm):
    batch, seq, d = x.shape
    t = batch * seq
    assert w_in.shape[0] == 1, "the block is specified with a single layer"
    xf = x.reshape(t, d)

    w_in_b = w_in[0].astype(_BF16)
    w_gate = w_in[0][:, 2 * CONV_CHANNELS + HEAD_COLS:].reshape(d, 3, N_KV_GROUPS, GROUP_SIZE)
    w_gate = w_gate.transpose(0, 2, 1, 3).reshape(d, GATE_COLS).astype(_BF16)

    h = _rmsnorm(xf, attn_norm[0], _BF16)
    glu = _glu_proj(h, w_in_b)
    tables = _rope_tables(seq)
    heads = _heads_proj(h, w_in_b, tables, seq)
    cmp_flat = _cmp_proj(h, w_in_b, tables, seq)
    gates = _gate_proj(h, w_gate).reshape(t, N_KV_GROUPS, 3 * GROUP_SIZE).transpose(1, 0, 2)

    half = CMP_STRIDE * HEAD_DIM
    pos = jnp.stack([cmp_pos_k[0], cmp_pos_v[0]]).reshape(2, 2, 1, half)
    cw1 = jnp.stack([cmp_k_w1[0], cmp_v_w1[0]]).astype(_BF16)
    cw2 = jnp.stack([cmp_k_w2[0], cmp_v_w2[0]]).astype(_BF16)
    cmp_kv = _compress(cmp_flat, pos, cw1, cw2, batch, seq)

    c = _conv(glu, conv_dw[0], conv_dw_b[0], conv_ln_g[0], conv_ln_b[0], batch, seq)
    o = _nsa(heads, cmp_kv, gates, batch, seq)
    mixed = _mix(c, o, h, w_proj_conv[0].astype(_BF16), w_proj_nsa[0].astype(_BF16),
                 w_merge[0].astype(_BF16), b_merge[0])

    w_route = jnp.concatenate(
        [w_grp[0], w_exp[0], jnp.zeros((d, LANES - N_GROUPS - N_EXPERTS), _F32)], axis=1)
    b_route = jnp.concatenate(
        [b_grp[0], b_exp[0], jnp.zeros((LANES - N_GROUPS - N_EXPERTS,), _F32)]).reshape(1, LANES)
    x1, h2, route = _out_router(mixed, xf, w_out[0].astype(_BF16), ffn_norm[0], w_route, b_route)

    expert = route[:, :TOP_K].astype(jnp.int32)
    moe_tables, pos_rows = _routing_tables(expert, t)
    y_rows = _moe(h2, moe_tables, exp_w1[0], exp_w3[0], exp_w2[0])
    out = _final(pos_rows, y_rows, x1, route, final_norm)
    return out.reshape(batch, seq, d)
```

```python
import numpy as np
import jax
import jax.numpy as jnp
from jax import lax
from jax.experimental import pallas as pl
from jax.experimental.pallas import tpu as pltpu

D_MODEL = 2048
CONV_CHANNELS = D_MODEL // 2
DW_TAPS = 31
N_HEADS = 16
HEAD_DIM = 128
N_KV_GROUPS = 4
GROUP_SIZE = N_HEADS // N_KV_GROUPS
ROT_DIM = HEAD_DIM // 4
ROPE_THETA = 500000.0
CMP_BLOCK = 32
CMP_STRIDE = 16
CMP_HIDDEN = 256
SEL_BLOCK = 64
N_SELECT = 16
N_FORCED_LOCAL = 2
WINDOW = 512
N_GROUPS = 8
EXPERTS_PER_GROUP = 8
N_EXPERTS = N_GROUPS * EXPERTS_PER_GROUP
TOP_K = 2
EXPERT_FF = 512
Q_COLS = N_HEADS * HEAD_DIM
KV_COLS = N_KV_GROUPS * HEAD_DIM
HEAD_COLS = Q_COLS + 6 * KV_COLS
GATE_COLS = 3 * N_HEADS
NORM_EPS = 1e-6
NEG_INF = -1e30
FORCE_BONUS = 1e4
LOG2_E = 1.4426950408889634

LANES = 128
SUBLANES = 8
VMEM_LIMIT_BYTES = 56 * 1024 * 1024

ROW_TILE = 512
PROJ_TILE = 1024
HEAD_TILE = GROUP_SIZE * HEAD_DIM
Q_TILE = 256
SEL_CHUNK = 512
CONV_TILE = 256
CONV_HALO = 32
CONV_ROWS = 16
OUT_TILE = 512
MOE_ROWS = 256
FINAL_TILE = 128
DMA_UNROLL = 8

_BF16 = jnp.bfloat16
_F32 = jnp.float32


def _params(*sem, **kw):
    return pltpu.CompilerParams(dimension_semantics=sem, vmem_limit_bytes=VMEM_LIMIT_BYTES, **kw)


def _dot(a, b):
    return jnp.dot(a, b, preferred_element_type=_F32)


def _dot_nt(a, b):
    return lax.dot_general(a, b, (((1,), (1,)), ((), ())), preferred_element_type=_F32)


def _sigmoid(x):
    return 1.0 / (1.0 + jnp.exp(-x))


def _rmsnorm_kernel(x_ref, g_ref, o_ref):
    x = x_ref[...]
    ms = jnp.mean(x * x, axis=-1, keepdims=True)
    o_ref[...] = (x * lax.rsqrt(ms + NORM_EPS) * g_ref[...]).astype(o_ref.dtype)


def _rmsnorm(x, g, out_dtype):
    t, d = x.shape
    return pl.pallas_call(
        _rmsnorm_kernel,
        out_shape=jax.ShapeDtypeStruct((t, d), out_dtype),
        grid=(t // ROW_TILE,),
        in_specs=[pl.BlockSpec((ROW_TILE, d), lambda i: (i, 0)),
                  pl.BlockSpec((1, d), lambda i: (0, 0))],
        out_specs=pl.BlockSpec((ROW_TILE, d), lambda i: (i, 0)),
        compiler_params=_params("parallel"),
        name="rmsnorm",
    )(x, g.reshape(1, d))


def _glu_kernel(h_ref, wa_ref, wb_ref, o_ref):
    h = h_ref[...]
    a = _dot(h, wa_ref[...])
    b = _dot(h, wb_ref[...])
    o_ref[...] = a * _sigmoid(b)


def _glu_proj(h, w_in_bf16):
    t, d = h.shape
    tn = HEAD_TILE
    nb = CONV_CHANNELS // tn
    return pl.pallas_call(
        _glu_kernel,
        out_shape=jax.ShapeDtypeStruct((t, CONV_CHANNELS), _F32),
        grid=(t // PROJ_TILE, nb),
        in_specs=[pl.BlockSpec((PROJ_TILE, d), lambda i, j: (i, 0)),
                  pl.BlockSpec((d, tn), lambda i, j: (0, j)),
                  pl.BlockSpec((d, tn), lambda i, j: (0, j + nb))],
        out_specs=pl.BlockSpec((PROJ_TILE, tn), lambda i, j: (i, j)),
        compiler_params=_params("parallel", "arbitrary"),
        name="glu_proj",
    )(h, w_in_bf16, w_in_bf16)


N_Q_SLABS = Q_COLS // HEAD_TILE
N_HEAD_SLABS = N_Q_SLABS + 4
Q_COL_BLOCK = 2 * CONV_CHANNELS // HEAD_TILE
CMP_COL_BLOCK = Q_COL_BLOCK + N_Q_SLABS
KV_COL_BLOCK = CMP_COL_BLOCK + 2


def _rope_tables(seq):
    pos = jnp.arange(seq, dtype=_F32)
    inv = ROPE_THETA ** (-jnp.arange(0, ROT_DIM, 2, dtype=_F32) / ROT_DIM)
    ang = pos[:, None] * inv[None, :]
    cos, sin = jnp.cos(ang), jnp.sin(ang)
    half = ROT_DIM // 2
    ones = jnp.ones((seq, HEAD_DIM - ROT_DIM), _F32)
    zeros_h = jnp.zeros((seq, half), _F32)
    zeros_r = jnp.zeros((seq, HEAD_DIM - ROT_DIM), _F32)
    c = jnp.concatenate([cos, cos, ones], axis=-1)
    s_lo = jnp.concatenate([zeros_h, sin, zeros_r], axis=-1)
    s_hi = jnp.concatenate([-sin, zeros_h, zeros_r], axis=-1)
    q_scale = HEAD_DIM ** -0.5 * LOG2_E
    zeros = jnp.zeros_like(c)
    return (jnp.stack([c * q_scale, c, jnp.ones_like(c)]),
            jnp.stack([s_lo * q_scale, s_lo, zeros]),
            jnp.stack([s_hi * q_scale, s_hi, zeros]))


ROPE_Q, ROPE_K, ROPE_NONE = 0, 1, 2


def _rotate_head(xs, c, slo, shi):
    half = ROT_DIM // 2
    return xs * c + pltpu.roll(xs, half, 1) * slo + pltpu.roll(xs, HEAD_DIM - half, 1) * shi


def _heads_kernel(h_ref, w_ref, c_ref, slo_ref, shi_ref, o_ref):
    acc = _dot(h_ref[...], w_ref[...])
    c, slo, shi = c_ref[0], slo_ref[0], shi_ref[0]
    for hd in range(GROUP_SIZE):
        r = _rotate_head(acc[:, hd * HEAD_DIM:(hd + 1) * HEAD_DIM], c, slo, shi)
        o_ref[0, hd] = r.astype(o_ref.dtype)


def _heads_proj(h, w_in_bf16, tables, seq):
    t, d = h.shape
    tm = PROJ_TILE
    spt = seq // tm

    def variant(j):
        return jnp.where(j < N_Q_SLABS, ROPE_Q, jnp.where(j % 2 == 0, ROPE_K, ROPE_NONE))

    tab = pl.BlockSpec((1, tm, HEAD_DIM), lambda i, j: (variant(j), i % spt, 0))

    def w_block(i, j):
        return (0, jnp.where(j < N_Q_SLABS, j + Q_COL_BLOCK, j - N_Q_SLABS + KV_COL_BLOCK))

    return pl.pallas_call(
        _heads_kernel,
        out_shape=jax.ShapeDtypeStruct((N_HEAD_SLABS, GROUP_SIZE, t, HEAD_DIM), _BF16),
        grid=(t // tm, N_HEAD_SLABS),
        in_specs=[pl.BlockSpec((tm, d), lambda i, j: (i, 0)),
                  pl.BlockSpec((d, HEAD_TILE), w_block),
                  tab, tab, tab],
        out_specs=pl.BlockSpec((1, GROUP_SIZE, tm, HEAD_DIM), lambda i, j: (j, 0, i, 0)),
        compiler_params=_params("parallel", "arbitrary"),
        name="heads_proj",
    )(h, w_in_bf16, *tables)


def _cmp_proj_kernel(h_ref, w_ref, c_ref, slo_ref, shi_ref, o_ref, stage_ref):
    acc = _dot(h_ref[...], w_ref[...])
    c, slo, shi = c_ref[0], slo_ref[0], shi_ref[0]
    for g in range(N_KV_GROUPS):
        stage_ref[g] = _rotate_head(acc[:, g * HEAD_DIM:(g + 1) * HEAD_DIM], c, slo, shi)

    n_chunk = stage_ref.shape[1] // CMP_STRIDE
    for g in range(N_KV_GROUPS):
        for l in range(CMP_STRIDE):
            rows = stage_ref[g, pl.ds(l, n_chunk, stride=CMP_STRIDE), :]
            o_ref[0, g, :, l * HEAD_DIM:(l + 1) * HEAD_DIM] = rows.astype(o_ref.dtype)


def _cmp_proj(h, w_in_bf16, tables, seq):
    t, d = h.shape
    tm = PROJ_TILE
    spt = seq // tm
    tab = pl.BlockSpec((1, tm, HEAD_DIM), lambda i, j: (jnp.where(j == 0, ROPE_K, ROPE_NONE), i % spt, 0))
    return pl.pallas_call(
        _cmp_proj_kernel,
        out_shape=jax.ShapeDtypeStruct((2, N_KV_GROUPS, t // CMP_STRIDE, CMP_STRIDE * HEAD_DIM), _BF16),
        grid=(t // tm, 2),
        in_specs=[pl.BlockSpec((tm, d), lambda i, j: (i, 0)),
                  pl.BlockSpec((d, HEAD_TILE), lambda i, j: (0, j + CMP_COL_BLOCK)),
                  tab, tab, tab],
        out_specs=pl.BlockSpec((1, N_KV_GROUPS, tm // CMP_STRIDE, CMP_STRIDE * HEAD_DIM),
                               lambda i, j: (j, 0, i, 0)),
        scratch_shapes=[pltpu.VMEM((N_KV_GROUPS, tm, HEAD_DIM), _F32)],
        compiler_params=_params("parallel", "arbitrary"),
        name="cmp_proj",
    )(h, w_in_bf16, *tables)


def _gate_kernel(h_ref, w_ref, o_ref):
    o_ref[...] = _sigmoid(_dot(h_ref[...], w_ref[...]))


def _gate_proj(h, w_gate_bf16):
    t, d = h.shape
    n = w_gate_bf16.shape[1]
    return pl.pallas_call(
        _gate_kernel,
        out_shape=jax.ShapeDtypeStruct((t, n), _F32),
        grid=(t // ROW_TILE,),
        in_specs=[pl.BlockSpec((ROW_TILE, d), lambda i: (i, 0)),
                  pl.BlockSpec((d, n), lambda i: (0, 0))],
        out_specs=pl.BlockSpec((ROW_TILE, n), lambda i: (i, 0)),
        compiler_params=_params("parallel"),
        name="gate_proj",
    )(h, w_gate_bf16)


def _compress_kernel(kv_ref, pos_ref, w1_ref, w2_ref, o_ref):
    half = CMP_STRIDE * HEAD_DIM
    c = kv_ref[0, 0].astype(_F32)
    n_chunk = c.shape[0]
    top = (c + pos_ref[0, 0]).astype(_BF16)
    bot = (c + pos_ref[0, 1]).astype(_BF16)
    u = _dot(top, w1_ref[0, :half, :])
    v = _dot(bot, w1_ref[0, half:, :])
    hidden = u + pltpu.roll(v, n_chunk - 1, 0)
    act = jax.nn.gelu(hidden).astype(_BF16)
    o_ref[0, 0, 0] = _dot(act, w2_ref[0]).astype(o_ref.dtype)


def _compress(cmp_flat, pos, w1, w2, batch, seq):
    n_chunk = seq // CMP_STRIDE
    return pl.pallas_call(
        _compress_kernel,
        out_shape=jax.ShapeDtypeStruct((2, batch, N_KV_GROUPS, n_chunk, HEAD_DIM), _BF16),
        grid=(2, batch, N_KV_GROUPS),
        in_specs=[pl.BlockSpec((1, 1, n_chunk, CMP_STRIDE * HEAD_DIM), lambda s, b, g: (s, g, b, 0)),
                  pl.BlockSpec((1, 2, 1, CMP_STRIDE * HEAD_DIM), lambda s, b, g: (s, 0, 0, 0)),
                  pl.BlockSpec((1, CMP_BLOCK * HEAD_DIM, CMP_HIDDEN), lambda s, b, g: (s, 0, 0)),
                  pl.BlockSpec((1, CMP_HIDDEN, HEAD_DIM), lambda s, b, g: (s, 0, 0))],
        out_specs=pl.BlockSpec((1, 1, 1, n_chunk, HEAD_DIM), lambda s, b, g: (s, b, g, 0, 0)),
        compiler_params=_params("parallel", "parallel", "parallel"),
        name="compress",
    )(cmp_flat, pos, w1, w2)


def _conv_kernel(cur_ref, halo_ref, dw_ref, dwb_ref, g_ref, b_ref, o_ref, ext_ref, sh_ref, acc_ref):
    i = pl.program_id(1)
    ts = cur_ref.shape[0]
    ext_ref[:CONV_HALO, :] = jnp.where(i > 0, halo_ref[...], 0.0)
    ext_ref[CONV_HALO:, :] = cur_ref[...]
    n_sh = sh_ref.shape[1]
    for b in range(1, SUBLANES):
        sh_ref[b - 1] = ext_ref[b:b + n_sh, :]
    off = CONV_HALO - (DW_TAPS - 1)

    def chunk(c, carry):
        r0 = pl.multiple_of(c * CONV_ROWS, CONV_ROWS)
        y = None
        for k in range(DW_TAPS):
            a, b = divmod(off + k, SUBLANES)
            src = ext_ref if b == 0 else sh_ref.at[b - 1]
            term = src[pl.ds(r0 + SUBLANES * a, CONV_ROWS), :] * dw_ref[k:k + 1, :]
            y = term if y is None else y + term
        acc_ref[pl.ds(r0, CONV_ROWS), :] = y
        return carry

    lax.fori_loop(0, ts // CONV_ROWS, chunk, 0)
    y = acc_ref[...] + dwb_ref[...]
    mu = jnp.mean(y, axis=-1, keepdims=True)
    yc = y - mu
    var = jnp.mean(yc * yc, axis=-1, keepdims=True)
    z = yc * lax.rsqrt(var + NORM_EPS) * g_ref[...] + b_ref[...]
    o_ref[...] = (z * _sigmoid(z)).astype(o_ref.dtype)


def _conv(glu, dw, dw_b, ln_g, ln_b, batch, seq):
    t, c = glu.shape
    ts = CONV_TILE
    nt = seq // ts
    hb = ts // CONV_HALO
    dw_pad = jnp.concatenate([dw, jnp.zeros((CONV_HALO - DW_TAPS, c), _F32)], axis=0)
    vec = pl.BlockSpec((1, c), lambda b, i: (0, 0))
    return pl.pallas_call(
        _conv_kernel,
        out_shape=jax.ShapeDtypeStruct((t, c), _BF16),
        grid=(batch, nt),
        in_specs=[pl.BlockSpec((ts, c), lambda b, i: (b * nt + i, 0)),
                  pl.BlockSpec((CONV_HALO, c), lambda b, i: (jnp.maximum((b * nt + i) * hb - 1, 0), 0)),
                  pl.BlockSpec((CONV_HALO, c), lambda b, i: (0, 0)),
                  vec, vec, vec],
        out_specs=pl.BlockSpec((ts, c), lambda b, i: (b * nt + i, 0)),
        scratch_shapes=[pltpu.VMEM((ts + CONV_HALO, c), _F32),
                        pltpu.VMEM((SUBLANES - 1, ts + CONV_HALO - SUBLANES, c), _F32),
                        pltpu.VMEM((ts, c), _F32)],
        compiler_params=_params("parallel", "arbitrary"),
        name="conv",
    )(glu, glu, dw_pad, dw_b.reshape(1, c), ln_g.reshape(1, c), ln_b.reshape(1, c))


def _importance_matrix(n_cmp_pad, n_blk):
    r_sel, r_cmp = SEL_BLOCK // CMP_STRIDE, CMP_BLOCK // CMP_STRIDE
    a = np.zeros((n_blk, n_cmp_pad), np.float32)
    for j in range(n_blk):
        for m in range(r_sel):
            for n in range(r_cmp):
                c = r_sel * j + m - n
                if 0 <= c < n_cmp_pad - 1:
                    a[j, c] += 1.0
    return a


def _expand_matrix(n_blk, seq):
    e = np.zeros((LANES, seq), np.float32)
    for j in range(n_blk):
        e[j, j * SEL_BLOCK:(j + 1) * SEL_BLOCK] = 1.0
    return e


def _nsa_step(n_active, q_ref, kc_ref, vc_ref, ks_ref, vs_ref, kw_ref, vw_ref, gate_ref, amat_ref, emat_ref,
              o_ref):
    tq_n = q_ref.shape[2]
    n_cmp_pad = kc_ref.shape[3]
    n_blk = amat_ref.shape[0]
    rows = GROUP_SIZE * tq_n
    start = pl.program_id(2) * tq_n
    q = q_ref[0].reshape(rows, HEAD_DIM)
    tq = start + lax.broadcasted_iota(jnp.int32, (tq_n, 1), 0)

    span = WINDOW + tq_n
    kstart = pl.multiple_of(jnp.maximum(start - WINDOW, 0), tq_n)
    kpos = kstart + lax.broadcasted_iota(jnp.int32, (1, span), 1)
    wbias = jnp.where((kpos <= tq) & (tq - kpos < WINDOW), 0.0, NEG_INF)
    sw = _dot_nt(q, kw_ref[0, 0, pl.ds(kstart, span), :]).reshape(GROUP_SIZE, tq_n, span) + wbias[None]

    cidx = lax.broadcasted_iota(jnp.int32, (1, n_cmp_pad), 1)
    cvalid = (cidx * CMP_STRIDE + (CMP_BLOCK - 1) <= tq) & (cidx < n_cmp_pad - 1)
    cbias = jnp.where(cvalid, 0.0, NEG_INF)
    s = _dot_nt(q, kc_ref[0, 0, 0]).reshape(GROUP_SIZE, tq_n, n_cmp_pad) + cbias[None]
    e_w = jnp.exp2(sw - jnp.max(sw, axis=-1, keepdims=True)).astype(_BF16)
    e = jnp.exp2(s - jnp.max(s, axis=-1, keepdims=True))
    inv_l = jnp.where((tq >= CMP_BLOCK - 1)[None], 1.0 / jnp.sum(e, axis=-1, keepdims=True), 0.0)
    p = e * inv_l
    vw_aug = jnp.concatenate([vw_ref[0, 0, pl.ds(kstart, span), :], jnp.ones((span, HEAD_DIM), _BF16)], axis=1)
    win = _dot(e_w.reshape(rows, span), vw_aug).reshape(GROUP_SIZE, tq_n, 2 * HEAD_DIM)
    o_cmp = _dot(p.reshape(rows, n_cmp_pad).astype(_BF16), vc_ref[0, 0, 0])

    imp = lax.dot_general(amat_ref[...], jnp.sum(p, axis=0), (((1,), (1,)), ((), ())),
                          preferred_element_type=_F32, precision=lax.Precision.HIGHEST)
    tq_row = start + lax.broadcasted_iota(jnp.int32, (1, tq_n), 1)
    blk = lax.broadcasted_iota(jnp.int32, (n_blk, 1), 0)
    cur = tq_row // SEL_BLOCK
    forced = (blk == 0) | ((blk <= cur) & (blk > cur - N_FORCED_LOCAL))
    imp = jnp.where(forced, imp + FORCE_BONUS, imp)
    imp = jnp.where(blk * SEL_BLOCK <= tq_row, imp, NEG_INF)
    rank = jnp.zeros((n_blk, tq_n), _F32)
    for i in range(n_blk):
        row = imp[i:i + 1, :]
        before = (row > imp) | ((row == imp) & (blk > i))
        rank = rank + jnp.where(before, 1.0, 0.0)
    sel_t = jnp.where(rank < float(min(N_SELECT, n_blk)), 1.0, 0.0)
    sel = jnp.concatenate([sel_t, jnp.zeros((LANES - n_blk, tq_n), _F32)], axis=0).T.astype(_BF16)

    ones_c = jnp.ones((SEL_CHUNK, HEAD_DIM), _BF16)

    def scores(c):
        return _dot_nt(q, ks_ref[0, 0, c * SEL_CHUNK:(c + 1) * SEL_CHUNK, :]).reshape(GROUP_SIZE, tq_n, SEL_CHUNK)

    m_run = None
    acc = None
    sc_next = scores(0)
    for c in range(n_active):
        lo = c * SEL_CHUNK
        sc_raw = sc_next
        if c + 1 < n_active:
            sc_next = scores(c + 1)
        chosen = _dot(sel, emat_ref[:, lo:lo + SEL_CHUNK]) > 0.5
        if c == n_active - 1:
            kpos = lo + lax.broadcasted_iota(jnp.int32, (1, SEL_CHUNK), 1)
            chosen = chosen & (kpos <= tq)
        sc = sc_raw + jnp.where(chosen, 0.0, NEG_INF)[None]
        m_c = jnp.max(sc, axis=-1, keepdims=True)
        m_new = m_c if c == 0 else jnp.maximum(m_run, m_c)
        e_c = jnp.exp2(sc - m_new).astype(_BF16)
        v_aug = jnp.concatenate([vs_ref[0, 0, lo:lo + SEL_CHUNK, :], ones_c], axis=1)
        pv = _dot(e_c.reshape(rows, SEL_CHUNK), v_aug).reshape(GROUP_SIZE, tq_n, 2 * HEAD_DIM)
        acc = pv if c == 0 else jnp.exp2(m_run - m_new) * acc + pv
        m_run = m_new

    o_cmp = o_cmp.reshape(GROUP_SIZE, tq_n, HEAD_DIM)
    gates = gate_ref[0]
    for r in range(GROUP_SIZE):
        g_cmp = gates[:, r:r + 1]
        g_slc = gates[:, GROUP_SIZE + r:GROUP_SIZE + r + 1] / acc[r, :, HEAD_DIM:]
        g_win = gates[:, 2 * GROUP_SIZE + r:2 * GROUP_SIZE + r + 1] / win[r, :, HEAD_DIM:]
        o = g_cmp * o_cmp[r] + g_slc * acc[r, :, :HEAD_DIM] + g_win * win[r, :, :HEAD_DIM]
        o_ref[:, r * HEAD_DIM:(r + 1) * HEAD_DIM] = o.astype(o_ref.dtype)


def _nsa_kernel(q_ref, kc_ref, vc_ref, ks_ref, vs_ref, kw_ref, vw_ref, gate_ref, amat_ref, emat_ref, o_ref):
    tq_n = q_ref.shape[2]
    seq = ks_ref.shape[2]
    n_active = (pl.program_id(2) * tq_n + tq_n - 1) // SEL_CHUNK + 1
    for k in range(1, seq // SEL_CHUNK + 1):
        @pl.when(n_active == k)
        def _(k=k):
            _nsa_step(k, q_ref, kc_ref, vc_ref, ks_ref, vs_ref, kw_ref, vw_ref, gate_ref, amat_ref, emat_ref,
                      o_ref)


def _nsa(heads, cmp_kv, gates, batch, seq):
    t = heads.shape[2]
    tq = Q_TILE
    nq = seq // tq
    n_chunk = cmp_kv.shape[3]
    n_blk = seq // SEL_BLOCK
    assert n_chunk == LANES and n_blk <= LANES and seq % SEL_CHUNK == 0 and WINDOW % tq == 0
    amat = jnp.asarray(_importance_matrix(n_chunk, n_blk))
    emat = jnp.asarray(_expand_matrix(n_blk, seq), dtype=_BF16)

    def kv_spec(slab):
        return pl.BlockSpec((1, 1, seq, HEAD_DIM), lambda b, g, i: (slab, g, b, 0))

    def cmp_spec(s):
        return pl.BlockSpec((1, 1, 1, n_chunk, HEAD_DIM), lambda b, g, i: (s, b, g, 0, 0))

    return pl.pallas_call(
        _nsa_kernel,
        out_shape=jax.ShapeDtypeStruct((t, Q_COLS), _BF16),
        grid=(batch, N_KV_GROUPS, nq),
        in_specs=[pl.BlockSpec((1, GROUP_SIZE, tq, HEAD_DIM), lambda b, g, i: (g, 0, b * nq + i, 0)),
                  cmp_spec(0), cmp_spec(1),
                  kv_spec(N_Q_SLABS), kv_spec(N_Q_SLABS + 1),
                  kv_spec(N_Q_SLABS + 2), kv_spec(N_Q_SLABS + 3),
                  pl.BlockSpec((1, tq, 3 * GROUP_SIZE), lambda b, g, i: (g, b * nq + i, 0)),
                  pl.BlockSpec((n_blk, n_chunk), lambda b, g, i: (0, 0)),
                  pl.BlockSpec((LANES, seq), lambda b, g, i: (0, 0))],
        out_specs=pl.BlockSpec((tq, HEAD_TILE), lambda b, g, i: (b * nq + i, g)),
        compiler_params=_params("parallel", "parallel", "arbitrary"),
        name="nsa",
    )(heads, cmp_kv, cmp_kv, heads, heads, heads, heads, gates, amat, emat)


def _mix_kernel(c_ref, o_ref, h_ref, wpc_ref, wpn_ref, wmc_ref, wmn_ref, bc_ref, bn_ref, out_ref):
    h = h_ref[...]
    g_conv = _sigmoid(_dot(h, wmc_ref[...]) + bc_ref[...])
    g_nsa = _sigmoid(_dot(h, wmn_ref[...]) + bn_ref[...])
    y_conv = _dot(c_ref[...], wpc_ref[...])
    y_nsa = _dot(o_ref[...], wpn_ref[...])
    out_ref[...] = (g_conv * y_conv + g_nsa * y_nsa).astype(out_ref.dtype)


def _mix(c, o, h, wpc, wpn, wm, bm):
    t, d = h.shape
    tm, tn = ROW_TILE, HEAD_TILE
    nb = d // tn
    bm2 = bm.reshape(1, 2 * d)
    return pl.pallas_call(
        _mix_kernel,
        out_shape=jax.ShapeDtypeStruct((t, d), _BF16),
        grid=(t // tm, nb),
        in_specs=[pl.BlockSpec((tm, CONV_CHANNELS), lambda i, j: (i, 0)),
                  pl.BlockSpec((tm, Q_COLS), lambda i, j: (i, 0)),
                  pl.BlockSpec((tm, d), lambda i, j: (i, 0)),
                  pl.BlockSpec((CONV_CHANNELS, tn), lambda i, j: (0, j)),
                  pl.BlockSpec((Q_COLS, tn), lambda i, j: (0, j)),
                  pl.BlockSpec((d, tn), lambda i, j: (0, j)),
                  pl.BlockSpec((d, tn), lambda i, j: (0, j + nb)),
                  pl.BlockSpec((1, tn), lambda i, j: (0, j)),
                  pl.BlockSpec((1, tn), lambda i, j: (0, j + nb))],
        out_specs=pl.BlockSpec((tm, tn), lambda i, j: (i, j)),
        compiler_params=_params("parallel", "arbitrary"),
        name="mix",
    )(c, o, h, wpc, wpn, wm, wm, bm2, bm2)


def _out_router_kernel(mix_ref, x_ref, wout_ref, g_ref, wrh_ref, wrl_ref, br_ref, x1_ref, h2_ref, route_ref):
    x1 = x_ref[...] + _dot(mix_ref[...], wout_ref[...])
    x1_ref[...] = x1
    ms = jnp.mean(x1 * x1, axis=-1, keepdims=True)
    h2 = x1 * lax.rsqrt(ms + NORM_EPS) * g_ref[...]
    h2_ref[...] = h2
    h_hi = h2.astype(_BF16)
    h_lo = (h2 - h_hi.astype(_F32)).astype(_BF16)
    logits = (_dot(h_hi, wrh_ref[...]) + _dot(h_lo, wrh_ref[...]) + _dot(h_hi, wrl_ref[...])
              + br_ref[...])
    lane = lax.broadcasted_iota(jnp.int32, (1, LANES), 1)
    lane_f = lane.astype(_F32)
    big = float(LANES)
    gl = jnp.where(lane < N_GROUPS, logits, NEG_INF)
    gmax = jnp.max(gl, axis=-1, keepdims=True)
    p_g = 1.0 / jnp.sum(jnp.where(lane < N_GROUPS, jnp.exp(gl - gmax), 0.0), axis=-1, keepdims=True)
    g_sel = jnp.min(jnp.where(gl == gmax, lane_f, big), axis=-1, keepdims=True)
    lo = N_GROUPS + g_sel * EXPERTS_PER_GROUP
    in_grp = (lane_f >= lo) & (lane_f < lo + EXPERTS_PER_GROUP)
    el = jnp.where(in_grp, logits, NEG_INF)
    v0 = jnp.max(el, axis=-1, keepdims=True)
    i0 = jnp.min(jnp.where(el == v0, lane_f, big), axis=-1, keepdims=True)
    el2 = jnp.where(lane_f == i0, NEG_INF, el)
    v1 = jnp.max(el2, axis=-1, keepdims=True)
    i1 = jnp.min(jnp.where(el2 == v1, lane_f, big), axis=-1, keepdims=True)
    t1 = jnp.exp(v1 - v0)
    w0 = p_g / (1.0 + t1)
    w1 = p_g * t1 / (1.0 + t1)
    route = jnp.where(lane == 0, i0 - N_GROUPS, 0.0)
    route = jnp.where(lane == 1, i1 - N_GROUPS, route)
    route = jnp.where(lane == 2, w0, route)
    route = jnp.where(lane == 3, w1, route)
    route_ref[...] = route


def _out_router(mixed, x, w_out, ffn_g, w_route, b_route):
    t, d = x.shape
    tm = OUT_TILE
    row = pl.BlockSpec((tm, d), lambda i: (i, 0))
    w_route_hi = w_route.astype(_BF16)
    w_route_lo = (w_route - w_route_hi.astype(_F32)).astype(_BF16)
    return pl.pallas_call(
        _out_router_kernel,
        out_shape=(jax.ShapeDtypeStruct((t, d), _F32),
                   jax.ShapeDtypeStruct((t, d), _F32),
                   jax.ShapeDtypeStruct((t, LANES), _F32)),
        grid=(t // tm,),
        in_specs=[row, row,
                  pl.BlockSpec((d, d), lambda i: (0, 0), pipeline_mode=pl.Buffered(1)),
                  pl.BlockSpec((1, d), lambda i: (0, 0)),
                  pl.BlockSpec((d, LANES), lambda i: (0, 0)),
                  pl.BlockSpec((d, LANES), lambda i: (0, 0)),
                  pl.BlockSpec((1, LANES), lambda i: (0, 0))],
        out_specs=(row, row, pl.BlockSpec((tm, LANES), lambda i: (i, 0))),
        compiler_params=_params("parallel"),
        name="out_router",
    )(mixed, x, w_out, ffn_g.reshape(1, d), w_route_hi, w_route_lo, b_route)


def _moe_kernel(be_ref, tok_ref, nb_ref, h2_hbm, w1_ref, w3_ref, w2_ref, y_ref, xbuf, w1b, w3b, w2b, sem):
    i = pl.program_id(0)
    nb = nb_ref[0]
    slot = i % 2
    bm = xbuf.shape[1]

    def row_copy(tok, r, sl):
        return pltpu.make_async_copy(h2_hbm.at[pl.ds(tok, 1), :], xbuf.at[sl, pl.ds(r, 1), :], sem.at[sl])

    @pl.when(i == 0)
    def _():
        def body(r, carry):
            row_copy(tok_ref[r], r, 0).start()
            return carry
        lax.fori_loop(0, bm, body, 0, unroll=DMA_UNROLL)

    for par in range(2):
        @pl.when((i + 1 < nb) & (slot == par))
        def _(par=par):
            base = (i + 1) * bm
            for r in range(bm):
                row_copy(tok_ref[base + r], r, 1 - par).start()

    @pl.when(i < nb)
    def _():
        pltpu.make_async_copy(h2_hbm.at[pl.ds(0, bm), :], xbuf.at[slot], sem.at[slot]).wait()

        changed = (i == 0) | (be_ref[i] != be_ref[jnp.maximum(i - 1, 0)])

        @pl.when(changed)
        def _():
            w1b[...] = w1_ref[0].astype(_BF16)
            w3b[...] = w3_ref[0].astype(_BF16)
            w2b[...] = w2_ref[0].astype(_BF16)

        x = xbuf[slot].astype(_BF16)
        a = _dot(x, w1b[...])
        b = _dot(x, w3b[...])
        act = (a * _sigmoid(a) * b).astype(_BF16)
        y_ref[...] = _dot(act, w2b[...])

    @pl.when(i >= nb)
    def _():
        y_ref[...] = jnp.zeros(y_ref.shape, y_ref.dtype)


def _moe(h2, tables, w1, w3, w2):
    t, d = h2.shape
    bm = MOE_ROWS
    block_e, row_tok, n_used = tables
    n_blocks = block_e.shape[0]
    ff = w1.shape[2]
    grid_spec = pltpu.PrefetchScalarGridSpec(
        num_scalar_prefetch=3,
        grid=(n_blocks,),
        in_specs=[pl.BlockSpec(memory_space=pl.ANY),
                  pl.BlockSpec((1, d, ff), lambda i, be, tok, nb: (be[i], 0, 0)),
                  pl.BlockSpec((1, d, ff), lambda i, be, tok, nb: (be[i], 0, 0)),
                  pl.BlockSpec((1, ff, d), lambda i, be, tok, nb: (be[i], 0, 0))],
        out_specs=pl.BlockSpec((bm, d), lambda i, be, tok, nb: (i, 0)),
        scratch_shapes=[pltpu.VMEM((2, bm, d), _F32),
                        pltpu.VMEM((d, ff), _BF16),
                        pltpu.VMEM((d, ff), _BF16),
                        pltpu.VMEM((ff, d), _BF16),
                        pltpu.SemaphoreType.DMA((2,))],
    )
    return pl.pallas_call(
        _moe_kernel,
        out_shape=jax.ShapeDtypeStruct((n_blocks * bm, d), _F32),
        grid_spec=grid_spec,
        compiler_params=_params("arbitrary", disable_bounds_checks=True),
        name="moe_experts",
    )(block_e, row_tok, n_used, h2, w1, w3, w2)


def _final_kernel(pos_ref, y_hbm, x1_ref, route_ref, g_ref, o_ref, ybuf, sem):
    i = pl.program_id(0)
    n = pl.num_programs(0)
    tm = x1_ref.shape[0]

    def row_copy(p, r, k, sl):
        return pltpu.make_async_copy(y_hbm.at[pl.ds(p, 1), :], ybuf.at[sl, k, pl.ds(r, 1), :], sem.at[sl])

    @pl.when(i == 0)
    def _():
        def body(r, carry):
            for k in range(TOP_K):
                row_copy(pos_ref[r * TOP_K + k], r, k, 0).start()
            return carry
        lax.fori_loop(0, tm, body, 0, unroll=DMA_UNROLL)

    def step(par, issue_next):
        if issue_next:
            base = (i + 1) * tm * TOP_K
            for r in range(tm):
                for k in range(TOP_K):
                    row_copy(pos_ref[base + r * TOP_K + k], r, k, 1 - par).start()
        for k in range(TOP_K):
            pltpu.make_async_copy(y_hbm.at[pl.ds(0, tm), :], ybuf.at[par, k], sem.at[par]).wait()
        route = route_ref[...]
        x = x1_ref[...] + route[:, 2:3] * ybuf[par, 0] + route[:, 3:4] * ybuf[par, 1]
        ms = jnp.mean(x * x, axis=-1, keepdims=True)
        o_ref[...] = x * lax.rsqrt(ms + NORM_EPS) * g_ref[...]

    for par in range(2):
        @pl.when((i % 2 == par) & (i + 1 < n))
        def _(par=par):
            step(par, True)

        @pl.when((i % 2 == par) & (i + 1 == n))
        def _(par=par):
            step(par, False)


def _final(pos, y_rows, x1, route, g):
    t, d = x1.shape
    tm = FINAL_TILE
    grid_spec = pltpu.PrefetchScalarGridSpec(
        num_scalar_prefetch=1,
        grid=(t // tm,),
        in_specs=[pl.BlockSpec(memory_space=pl.ANY),
                  pl.BlockSpec((tm, d), lambda i, pos: (i, 0)),
                  pl.BlockSpec((tm, LANES), lambda i, pos: (i, 0)),
                  pl.BlockSpec((1, d), lambda i, pos: (0, 0))],
        out_specs=pl.BlockSpec((tm, d), lambda i, pos: (i, 0)),
        scratch_shapes=[pltpu.VMEM((2, TOP_K, tm, d), _F32),
                        pltpu.SemaphoreType.DMA((2,))],
    )
    return pl.pallas_call(
        _final_kernel,
        out_shape=jax.ShapeDtypeStruct((t, d), _F32),
        grid_spec=grid_spec,
        compiler_params=_params("arbitrary", disable_bounds_checks=True),
        name="combine_norm",
    )(pos, y_rows, x1, route, g.reshape(1, d))


def _routing_tables(expert, n_tokens):
    bm = MOE_ROWS
    n_assign = n_tokens * TOP_K
    n_blocks = -(-(n_assign + N_EXPERTS * (bm - 1)) // bm)
    i32 = jnp.int32
    flat_e = expert.reshape(-1)
    experts = jnp.arange(N_EXPERTS, dtype=i32)
    assign = jnp.arange(n_assign, dtype=i32)
    e_sorted, order = lax.sort((flat_e, assign), num_keys=1)
    start = jnp.sum((e_sorted[:, None] < experts[None, :]).astype(i32), axis=0)
    counts = jnp.concatenate([start[1:], jnp.full((1,), n_assign, i32)]) - start
    padded = (counts + bm - 1) // bm * bm
    pend = jnp.cumsum(padded)
    pstart = pend - padded
    shift = pstart - start
    dshift = shift - jnp.concatenate([jnp.zeros((1,), i32), shift[:-1]])
    shift_sorted = jnp.sum(jnp.where(assign[:, None] >= start[None, :], dshift[None, :], 0), axis=1)
    _, pos = lax.sort((order, assign + shift_sorted), num_keys=1)
    block_row0 = jnp.arange(n_blocks, dtype=i32) * bm
    block_e = jnp.minimum(jnp.sum((pend[None, :] <= block_row0[:, None]).astype(i32), axis=1), N_EXPERTS - 1)
    onehot = block_e[:, None] == experts[None, :]

    def per_block(table):
        return jnp.sum(jnp.where(onehot, table[None, :], 0), axis=1)

    block_start = per_block(start - pstart) + block_row0
    block_end = per_block(start + counts)
    src = block_start[:, None] + jnp.arange(bm, dtype=i32)[None, :]
    valid = src < block_end[:, None]
    tok_sorted = order // TOP_K
    row_tok = jnp.where(valid, tok_sorted[jnp.clip(src, 0, n_assign - 1)], 0)
    n_used = (pend[-1] // bm).astype(i32).reshape(1)
    return (block_e, row_tok.reshape(-1).astype(i32), n_used), pos.astype(i32)


def kernel(x, attn_norm, w_in, conv_dw, conv_dw_b, conv_ln_g, conv_ln_b, cmp_pos_k, cmp_k_w1, cmp_k_w2, cmp_pos_v, cmp_v_w1, cmp_v_w2, w_proj_conv, w_proj_nsa, w_merge, b_merge, w_out, ffn_norm, w_grp, b_grp, w_exp, b_exp, exp_w1, exp_w3, exp_w2, final_norm):
    batch, seq, d = x.shape
    t = batch * seq
    assert w_in.shape[0] == 1, "the block is specified with a single layer"
    xf = x.reshape(t, d)

    w_in_b = w_in[0].astype(_BF16)
    w_gate = w_in[0][:, 2 * CONV_CHANNELS + HEAD_COLS:].reshape(d, 3, N_KV_GROUPS, GROUP_SIZE)
    w_gate = w_gate.transpose(0, 2, 1, 3).reshape(d, GATE_COLS).astype(_BF16)

    h = _rmsnorm(xf, attn_norm[0], _BF16)
    glu = _glu_proj(h, w_in_b)
    tables = _rope_tables(seq)
    heads = _heads_proj(h, w_in_b, tables, seq)
    cmp_flat = _cmp_proj(h, w_in_b, tables, seq)
    gates = _gate_proj(h, w_gate).reshape(t, N_KV_GROUPS, 3 * GROUP_SIZE).transpose(1, 0, 2)

    half = CMP_STRIDE * HEAD_DIM
    pos = jnp.stack([cmp_pos_k[0], cmp_pos_v[0]]).reshape(2, 2, 1, half)
    cw1 = jnp.stack([cmp_k_w1[0], cmp_v_w1[0]]).astype(_BF16)
    cw2 = jnp.stack([cmp_k_w2[0], cmp_v_w2[0]]).astype(_BF16)
    cmp_kv = _compress(cmp_flat, pos, cw1, cw2, batch, seq)

    c = _conv(glu, conv_dw[0], conv_dw_b[0], conv_ln_g[0], conv_ln_b[0], batch, seq)
    o = _nsa(heads, cmp_kv, gates, batch, seq)
    mixed = _mix(c, o, h, w_proj_conv[0].astype(_BF16), w_proj_nsa[0].astype(_BF16),
                 w_merge[0].astype(_BF16), b_merge[0])

    w_route = jnp.concatenate(
        [w_grp[0], w_exp[0], jnp.zeros((d, LANES - N_GROUPS - N_EXPERTS), _F32)], axis=1)
    b_route = jnp.concatenate(
        [b_grp[0], b_exp[0], jnp.zeros((LANES - N_GROUPS - N_EXPERTS,), _F32)]).reshape(1, LANES)
    x1, h2, route = _out_router(mixed, xf, w_out[0].astype(_BF16), ffn_norm[0], w_route, b_route)

    expert = route[:, :TOP_K].astype(jnp.int32)
    moe_tables, pos_rows = _routing_tables(expert, t)
    y_rows = _moe(h2, moe_tables, exp_w1[0], exp_w3[0], exp_w2[0])
    out = _final(pos_rows, y_rows, x1, route, final_norm)
    return out.reshape(batch, seq, d)
```

```python
import numpy as np
import jax
import jax.numpy as jnp
from jax import lax
from jax.experimental import pallas as pl
from jax.experimental.pallas import tpu as pltpu

D_MODEL = 2048
CONV_CHANNELS = D_MODEL // 2
DW_TAPS = 31
N_HEADS = 16
HEAD_DIM = 128
N_KV_GROUPS = 4
GROUP_SIZE = N_HEADS // N_KV_GROUPS
ROT_DIM = HEAD_DIM // 4
ROPE_THETA = 500000.0
CMP_BLOCK = 32
CMP_STRIDE = 16
CMP_HIDDEN = 256
SEL_BLOCK = 64
N_SELECT = 16
N_FORCED_LOCAL = 2
WINDOW = 512
N_GROUPS = 8
EXPERTS_PER_GROUP = 8
N_EXPERTS = N_GROUPS * EXPERTS_PER_GROUP
TOP_K = 2
EXPERT_FF = 512
Q_COLS = N_HEADS * HEAD_DIM
KV_COLS = N_KV_GROUPS * HEAD_DIM
HEAD_COLS = Q_COLS + 6 * KV_COLS
GATE_COLS = 3 * N_HEADS
NORM_EPS = 1e-6
NEG_INF = -1e30
FORCE_BONUS = 1e4
LOG2_E = 1.4426950408889634

LANES = 128
SUBLANES = 8
VMEM_LIMIT_BYTES = 56 * 1024 * 1024

ROW_TILE = 512
PROJ_TILE = 2048
MIX_TILE = 1024
MIX_COLS = 256
HEAD_TILE = GROUP_SIZE * HEAD_DIM
Q_TILE = 256
SEL_CHUNK = 512
CONV_TILE = 256
CONV_HALO = 32
CONV_ROWS = 16
OUT_TILE = 512
MOE_ROWS = 256
FINAL_TILE = 128
DMA_UNROLL = 8

_BF16 = jnp.bfloat16
_F32 = jnp.float32


def _params(*sem, **kw):
    return pltpu.CompilerParams(dimension_semantics=sem, vmem_limit_bytes=VMEM_LIMIT_BYTES, **kw)


def _dot(a, b):
    return jnp.dot(a, b, preferred_element_type=_F32)


def _dot_nt(a, b):
    return lax.dot_general(a, b, (((1,), (1,)), ((), ())), preferred_element_type=_F32)


def _sigmoid(x):
    return 1.0 / (1.0 + jnp.exp(-x))


def _rmsnorm_kernel(x_ref, g_ref, o_ref):
    x = x_ref[...]
    ms = jnp.mean(x * x, axis=-1, keepdims=True)
    o_ref[...] = (x * lax.rsqrt(ms + NORM_EPS) * g_ref[...]).astype(o_ref.dtype)


def _rmsnorm(x, g, out_dtype):
    t, d = x.shape
    return pl.pallas_call(
        _rmsnorm_kernel,
        out_shape=jax.ShapeDtypeStruct((t, d), out_dtype),
        grid=(t // ROW_TILE,),
        in_specs=[pl.BlockSpec((ROW_TILE, d), lambda i: (i, 0)),
                  pl.BlockSpec((1, d), lambda i: (0, 0))],
        out_specs=pl.BlockSpec((ROW_TILE, d), lambda i: (i, 0)),
        compiler_params=_params("parallel"),
        name="rmsnorm",
    )(x, g.reshape(1, d))


def _glu_kernel(h_ref, wa_ref, wb_ref, o_ref):
    h = h_ref[...]
    a = _dot(h, wa_ref[...])
    b = _dot(h, wb_ref[...])
    o_ref[...] = a * _sigmoid(b)


def _glu_proj(h, w_in_bf16):
    t, d = h.shape
    tn = HEAD_TILE
    nb = CONV_CHANNELS // tn
    return pl.pallas_call(
        _glu_kernel,
        out_shape=jax.ShapeDtypeStruct((t, CONV_CHANNELS), _F32),
        grid=(t // PROJ_TILE, nb),
        in_specs=[pl.BlockSpec((PROJ_TILE, d), lambda i, j: (i, 0)),
                  pl.BlockSpec((d, tn), lambda i, j: (0, j)),
                  pl.BlockSpec((d, tn), lambda i, j: (0, j + nb))],
        out_specs=pl.BlockSpec((PROJ_TILE, tn), lambda i, j: (i, j)),
        compiler_params=_params("parallel", "arbitrary"),
        name="glu_proj",
    )(h, w_in_bf16, w_in_bf16)


N_Q_SLABS = Q_COLS // HEAD_TILE
N_HEAD_SLABS = N_Q_SLABS + 4
Q_COL_BLOCK = 2 * CONV_CHANNELS // HEAD_TILE
CMP_COL_BLOCK = Q_COL_BLOCK + N_Q_SLABS
KV_COL_BLOCK = CMP_COL_BLOCK + 2


def _rope_tables(seq):
    pos = jnp.arange(seq, dtype=_F32)
    inv = ROPE_THETA ** (-jnp.arange(0, ROT_DIM, 2, dtype=_F32) / ROT_DIM)
    ang = pos[:, None] * inv[None, :]
    cos, sin = jnp.cos(ang), jnp.sin(ang)
    half = ROT_DIM // 2
    ones = jnp.ones((seq, HEAD_DIM - ROT_DIM), _F32)
    zeros_h = jnp.zeros((seq, half), _F32)
    zeros_r = jnp.zeros((seq, HEAD_DIM - ROT_DIM), _F32)
    c = jnp.concatenate([cos, cos, ones], axis=-1)
    s_lo = jnp.concatenate([zeros_h, sin, zeros_r], axis=-1)
    s_hi = jnp.concatenate([-sin, zeros_h, zeros_r], axis=-1)
    q_scale = HEAD_DIM ** -0.5 * LOG2_E
    zeros = jnp.zeros_like(c)
    return (jnp.stack([c * q_scale, c, jnp.ones_like(c)]),
            jnp.stack([s_lo * q_scale, s_lo, zeros]),
            jnp.stack([s_hi * q_scale, s_hi, zeros]))


ROPE_Q, ROPE_K, ROPE_NONE = 0, 1, 2


def _rotate_head(xs, c, slo, shi):
    half = ROT_DIM // 2
    return xs * c + pltpu.roll(xs, half, 1) * slo + pltpu.roll(xs, HEAD_DIM - half, 1) * shi


def _heads_kernel(h_ref, w_ref, c_ref, slo_ref, shi_ref, o_ref):
    acc = _dot(h_ref[...], w_ref[...])
    c, slo, shi = c_ref[0], slo_ref[0], shi_ref[0]
    for hd in range(GROUP_SIZE):
        r = _rotate_head(acc[:, hd * HEAD_DIM:(hd + 1) * HEAD_DIM], c, slo, shi)
        o_ref[0, hd] = r.astype(o_ref.dtype)


def _heads_proj(h, w_in_bf16, tables, seq):
    t, d = h.shape
    tm = PROJ_TILE
    spt = seq // tm

    def variant(j):
        return jnp.where(j < N_Q_SLABS, ROPE_Q, jnp.where(j % 2 == 0, ROPE_K, ROPE_NONE))

    tab = pl.BlockSpec((1, tm, HEAD_DIM), lambda i, j: (variant(j), i % spt, 0))

    def w_block(i, j):
        return (0, jnp.where(j < N_Q_SLABS, j + Q_COL_BLOCK, j - N_Q_SLABS + KV_COL_BLOCK))

    return pl.pallas_call(
        _heads_kernel,
        out_shape=jax.ShapeDtypeStruct((N_HEAD_SLABS, GROUP_SIZE, t, HEAD_DIM), _BF16),
        grid=(t // tm, N_HEAD_SLABS),
        in_specs=[pl.BlockSpec((tm, d), lambda i, j: (i, 0)),
                  pl.BlockSpec((d, HEAD_TILE), w_block),
                  tab, tab, tab],
        out_specs=pl.BlockSpec((1, GROUP_SIZE, tm, HEAD_DIM), lambda i, j: (j, 0, i, 0)),
        compiler_params=_params("parallel", "arbitrary"),
        name="heads_proj",
    )(h, w_in_bf16, *tables)


def _cmp_proj_kernel(h_ref, w_ref, c_ref, slo_ref, shi_ref, o_ref, stage_ref):
    acc = _dot(h_ref[...], w_ref[...])
    c, slo, shi = c_ref[0], slo_ref[0], shi_ref[0]
    for g in range(N_KV_GROUPS):
        stage_ref[g] = _rotate_head(acc[:, g * HEAD_DIM:(g + 1) * HEAD_DIM], c, slo, shi)

    n_chunk = stage_ref.shape[1] // CMP_STRIDE
    for g in range(N_KV_GROUPS):
        for l in range(CMP_STRIDE):
            rows = stage_ref[g, pl.ds(l, n_chunk, stride=CMP_STRIDE), :]
            o_ref[0, g, :, l * HEAD_DIM:(l + 1) * HEAD_DIM] = rows.astype(o_ref.dtype)


def _cmp_proj(h, w_in_bf16, tables, seq):
    t, d = h.shape
    tm = PROJ_TILE
    spt = seq // tm
    tab = pl.BlockSpec((1, tm, HEAD_DIM), lambda i, j: (jnp.where(j == 0, ROPE_K, ROPE_NONE), i % spt, 0))
    return pl.pallas_call(
        _cmp_proj_kernel,
        out_shape=jax.ShapeDtypeStruct((2, N_KV_GROUPS, t // CMP_STRIDE, CMP_STRIDE * HEAD_DIM), _BF16),
        grid=(t // tm, 2),
        in_specs=[pl.BlockSpec((tm, d), lambda i, j: (i, 0)),
                  pl.BlockSpec((d, HEAD_TILE), lambda i, j: (0, j + CMP_COL_BLOCK)),
                  tab, tab, tab],
        out_specs=pl.BlockSpec((1, N_KV_GROUPS, tm // CMP_STRIDE, CMP_STRIDE * HEAD_DIM),
                               lambda i, j: (j, 0, i, 0)),
        scratch_shapes=[pltpu.VMEM((N_KV_GROUPS, tm, HEAD_DIM), _F32)],
        compiler_params=_params("parallel", "arbitrary"),
        name="cmp_proj",
    )(h, w_in_bf16, *tables)


def _gate_kernel(h_ref, w_ref, o_ref):
    o_ref[...] = _sigmoid(_dot(h_ref[...], w_ref[...]))


def _gate_proj(h, w_gate_bf16):
    t, d = h.shape
    n = w_gate_bf16.shape[1]
    return pl.pallas_call(
        _gate_kernel,
        out_shape=jax.ShapeDtypeStruct((t, n), _F32),
        grid=(t // ROW_TILE,),
        in_specs=[pl.BlockSpec((ROW_TILE, d), lambda i: (i, 0)),
                  pl.BlockSpec((d, n), lambda i: (0, 0))],
        out_specs=pl.BlockSpec((ROW_TILE, n), lambda i: (i, 0)),
        compiler_params=_params("parallel"),
        name="gate_proj",
    )(h, w_gate_bf16)


def _compress_kernel(kv_ref, pos_ref, w1_ref, w2_ref, o_ref):
    half = CMP_STRIDE * HEAD_DIM
    c = kv_ref[0, 0].astype(_F32)
    n_chunk = c.shape[0]
    top = (c + pos_ref[0, 0]).astype(_BF16)
    bot = (c + pos_ref[0, 1]).astype(_BF16)
    u = _dot(top, w1_ref[0, :half, :])
    v = _dot(bot, w1_ref[0, half:, :])
    hidden = u + pltpu.roll(v, n_chunk - 1, 0)
    act = jax.nn.gelu(hidden).astype(_BF16)
    o_ref[0, 0, 0] = _dot(act, w2_ref[0]).astype(o_ref.dtype)


def _compress(cmp_flat, pos, w1, w2, batch, seq):
    n_chunk = seq // CMP_STRIDE
    return pl.pallas_call(
        _compress_kernel,
        out_shape=jax.ShapeDtypeStruct((2, batch, N_KV_GROUPS, n_chunk, HEAD_DIM), _BF16),
        grid=(2, batch, N_KV_GROUPS),
        in_specs=[pl.BlockSpec((1, 1, n_chunk, CMP_STRIDE * HEAD_DIM), lambda s, b, g: (s, g, b, 0)),
                  pl.BlockSpec((1, 2, 1, CMP_STRIDE * HEAD_DIM), lambda s, b, g: (s, 0, 0, 0)),
                  pl.BlockSpec((1, CMP_BLOCK * HEAD_DIM, CMP_HIDDEN), lambda s, b, g: (s, 0, 0)),
                  pl.BlockSpec((1, CMP_HIDDEN, HEAD_DIM), lambda s, b, g: (s, 0, 0))],
        out_specs=pl.BlockSpec((1, 1, 1, n_chunk, HEAD_DIM), lambda s, b, g: (s, b, g, 0, 0)),
        compiler_params=_params("parallel", "parallel", "parallel"),
        name="compress",
    )(cmp_flat, pos, w1, w2)


def _conv_kernel(cur_ref, halo_ref, dw_ref, dwb_ref, g_ref, b_ref, o_ref, ext_ref, sh_ref, acc_ref):
    i = pl.program_id(1)
    ts = cur_ref.shape[0]
    ext_ref[:CONV_HALO, :] = jnp.where(i > 0, halo_ref[...], 0.0)
    ext_ref[CONV_HALO:, :] = cur_ref[...]
    n_sh = sh_ref.shape[1]
    for b in range(1, SUBLANES):
        sh_ref[b - 1] = ext_ref[b:b + n_sh, :]
    off = CONV_HALO - (DW_TAPS - 1)

    def chunk(c, carry):
        r0 = pl.multiple_of(c * CONV_ROWS, CONV_ROWS)
        y = None
        for k in range(DW_TAPS):
            a, b = divmod(off + k, SUBLANES)
            src = ext_ref if b == 0 else sh_ref.at[b - 1]
            term = src[pl.ds(r0 + SUBLANES * a, CONV_ROWS), :] * dw_ref[k:k + 1, :]
            y = term if y is None else y + term
        acc_ref[pl.ds(r0, CONV_ROWS), :] = y
        return carry

    lax.fori_loop(0, ts // CONV_ROWS, chunk, 0)
    y = acc_ref[...] + dwb_ref[...]
    mu = jnp.mean(y, axis=-1, keepdims=True)
    yc = y - mu
    var = jnp.mean(yc * yc, axis=-1, keepdims=True)
    z = yc * lax.rsqrt(var + NORM_EPS) * g_ref[...] + b_ref[...]
    o_ref[...] = (z * _sigmoid(z)).astype(o_ref.dtype)


def _conv(glu, dw, dw_b, ln_g, ln_b, batch, seq):
    t, c = glu.shape
    ts = CONV_TILE
    nt = seq // ts
    hb = ts // CONV_HALO
    dw_pad = jnp.concatenate([dw, jnp.zeros((CONV_HALO - DW_TAPS, c), _F32)], axis=0)
    vec = pl.BlockSpec((1, c), lambda b, i: (0, 0))
    return pl.pallas_call(
        _conv_kernel,
        out_shape=jax.ShapeDtypeStruct((t, c), _BF16),
        grid=(batch, nt),
        in_specs=[pl.BlockSpec((ts, c), lambda b, i: (b * nt + i, 0)),
                  pl.BlockSpec((CONV_HALO, c), lambda b, i: (jnp.maximum((b * nt + i) * hb - 1, 0), 0)),
                  pl.BlockSpec((CONV_HALO, c), lambda b, i: (0, 0)),
                  vec, vec, vec],
        out_specs=pl.BlockSpec((ts, c), lambda b, i: (b * nt + i, 0)),
        scratch_shapes=[pltpu.VMEM((ts + CONV_HALO, c), _F32),
                        pltpu.VMEM((SUBLANES - 1, ts + CONV_HALO - SUBLANES, c), _F32),
                        pltpu.VMEM((ts, c), _F32)],
        compiler_params=_params("parallel", "arbitrary"),
        name="conv",
    )(glu, glu, dw_pad, dw_b.reshape(1, c), ln_g.reshape(1, c), ln_b.reshape(1, c))


def _importance_matrix(n_cmp_pad, n_blk):
    r_sel, r_cmp = SEL_BLOCK // CMP_STRIDE, CMP_BLOCK // CMP_STRIDE
    a = np.zeros((n_blk, n_cmp_pad), np.float32)
    for j in range(n_blk):
        for m in range(r_sel):
            for n in range(r_cmp):
                c = r_sel * j + m - n
                if 0 <= c < n_cmp_pad - 1:
                    a[j, c] += 1.0
    return a


def _expand_matrix(n_blk, seq):
    e = np.zeros((LANES, seq), np.float32)
    for j in range(n_blk):
        e[j, j * SEL_BLOCK:(j + 1) * SEL_BLOCK] = 1.0
    return e


def _nsa_step(n_active, q_ref, kc_ref, vc_ref, ks_ref, vs_ref, kw_ref, vw_ref, gate_ref, amat_ref, emat_ref,
              o_ref):
    tq_n = q_ref.shape[2]
    n_cmp_pad = kc_ref.shape[3]
    n_blk = amat_ref.shape[0]
    rows = GROUP_SIZE * tq_n
    start = pl.program_id(2) * tq_n
    q = q_ref[0].reshape(rows, HEAD_DIM)
    tq = start + lax.broadcasted_iota(jnp.int32, (tq_n, 1), 0)

    span = WINDOW + tq_n
    kstart = pl.multiple_of(jnp.maximum(start - WINDOW, 0), tq_n)
    kpos = kstart + lax.broadcasted_iota(jnp.int32, (1, span), 1)
    wbias = jnp.where((kpos <= tq) & (tq - kpos < WINDOW), 0.0, NEG_INF)
    sw = _dot_nt(q, kw_ref[0, 0, pl.ds(kstart, span), :]).reshape(GROUP_SIZE, tq_n, span) + wbias[None]

    cidx = lax.broadcasted_iota(jnp.int32, (1, n_cmp_pad), 1)
    cvalid = (cidx * CMP_STRIDE + (CMP_BLOCK - 1) <= tq) & (cidx < n_cmp_pad - 1)
    cbias = jnp.where(cvalid, 0.0, NEG_INF)
    s = _dot_nt(q, kc_ref[0, 0, 0]).reshape(GROUP_SIZE, tq_n, n_cmp_pad) + cbias[None]
    e_w = jnp.exp2(sw - jnp.max(sw, axis=-1, keepdims=True)).astype(_BF16)
    e = jnp.exp2(s - jnp.max(s, axis=-1, keepdims=True))
    inv_l = jnp.where((tq >= CMP_BLOCK - 1)[None], 1.0 / jnp.sum(e, axis=-1, keepdims=True), 0.0)
    p = e * inv_l
    vw_aug = jnp.concatenate([vw_ref[0, 0, pl.ds(kstart, span), :], jnp.ones((span, HEAD_DIM), _BF16)], axis=1)
    win = _dot(e_w.reshape(rows, span), vw_aug).reshape(GROUP_SIZE, tq_n, 2 * HEAD_DIM)
    o_cmp = _dot(p.reshape(rows, n_cmp_pad).astype(_BF16), vc_ref[0, 0, 0])

    imp = lax.dot_general(amat_ref[...], jnp.sum(p, axis=0), (((1,), (1,)), ((), ())),
                          preferred_element_type=_F32, precision=lax.Precision.HIGHEST)
    tq_row = start + lax.broadcasted_iota(jnp.int32, (1, tq_n), 1)
    blk = lax.broadcasted_iota(jnp.int32, (n_blk, 1), 0)
    cur = tq_row // SEL_BLOCK
    forced = (blk == 0) | ((blk <= cur) & (blk > cur - N_FORCED_LOCAL))
    imp = jnp.where(forced, imp + FORCE_BONUS, imp)
    imp = jnp.where(blk * SEL_BLOCK <= tq_row, imp, NEG_INF)
    rank = jnp.zeros((n_blk, tq_n), _F32)
    for i in range(n_blk):
        row = imp[i:i + 1, :]
        before = (row > imp) | ((row == imp) & (blk > i))
        rank = rank + jnp.where(before, 1.0, 0.0)
    sel_t = jnp.where(rank < float(min(N_SELECT, n_blk)), 1.0, 0.0)
    sel = jnp.concatenate([sel_t, jnp.zeros((LANES - n_blk, tq_n), _F32)], axis=0).T.astype(_BF16)

    ones_c = jnp.ones((SEL_CHUNK, HEAD_DIM), _BF16)

    def scores(c):
        return _dot_nt(q, ks_ref[0, 0, c * SEL_CHUNK:(c + 1) * SEL_CHUNK, :]).reshape(GROUP_SIZE, tq_n, SEL_CHUNK)

    m_run = None
    acc = None
    sc_next = scores(0)
    for c in range(n_active):
        lo = c * SEL_CHUNK
        sc_raw = sc_next
        if c + 1 < n_active:
            sc_next = scores(c + 1)
        chosen = _dot(sel, emat_ref[:, lo:lo + SEL_CHUNK]) > 0.5
        if c == n_active - 1:
            kpos = lo + lax.broadcasted_iota(jnp.int32, (1, SEL_CHUNK), 1)
            chosen = chosen & (kpos <= tq)
        sc = sc_raw + jnp.where(chosen, 0.0, NEG_INF)[None]
        m_c = jnp.max(sc, axis=-1, keepdims=True)
        m_new = m_c if c == 0 else jnp.maximum(m_run, m_c)
        e_c = jnp.exp2(sc - m_new).astype(_BF16)
        v_aug = jnp.concatenate([vs_ref[0, 0, lo:lo + SEL_CHUNK, :], ones_c], axis=1)
        pv = _dot(e_c.reshape(rows, SEL_CHUNK), v_aug).reshape(GROUP_SIZE, tq_n, 2 * HEAD_DIM)
        acc = pv if c == 0 else jnp.exp2(m_run - m_new) * acc + pv
        m_run = m_new

    o_cmp = o_cmp.reshape(GROUP_SIZE, tq_n, HEAD_DIM)
    gates = gate_ref[0]
    for r in range(GROUP_SIZE):
        g_cmp = gates[:, r:r + 1]
        g_slc = gates[:, GROUP_SIZE + r:GROUP_SIZE + r + 1] / acc[r, :, HEAD_DIM:]
        g_win = gates[:, 2 * GROUP_SIZE + r:2 * GROUP_SIZE + r + 1] / win[r, :, HEAD_DIM:]
        o = g_cmp * o_cmp[r] + g_slc * acc[r, :, :HEAD_DIM] + g_win * win[r, :, :HEAD_DIM]
        o_ref[:, r * HEAD_DIM:(r + 1) * HEAD_DIM] = o.astype(o_ref.dtype)


def _nsa_kernel(q_ref, kc_ref, vc_ref, ks_ref, vs_ref, kw_ref, vw_ref, gate_ref, amat_ref, emat_ref, o_ref):
    tq_n = q_ref.shape[2]
    seq = ks_ref.shape[2]
    n_active = (pl.program_id(2) * tq_n + tq_n - 1) // SEL_CHUNK + 1
    for k in range(1, seq // SEL_CHUNK + 1):
        @pl.when(n_active == k)
        def _(k=k):
            _nsa_step(k, q_ref, kc_ref, vc_ref, ks_ref, vs_ref, kw_ref, vw_ref, gate_ref, amat_ref, emat_ref,
                      o_ref)


def _nsa(heads, cmp_kv, gates, batch, seq):
    t = heads.shape[2]
    tq = Q_TILE
    nq = seq // tq
    n_chunk = cmp_kv.shape[3]
    n_blk = seq // SEL_BLOCK
    assert n_chunk == LANES and n_blk <= LANES and seq % SEL_CHUNK == 0 and WINDOW % tq == 0
    amat = jnp.asarray(_importance_matrix(n_chunk, n_blk))
    emat = jnp.asarray(_expand_matrix(n_blk, seq), dtype=_BF16)

    def kv_spec(slab):
        return pl.BlockSpec((1, 1, seq, HEAD_DIM), lambda b, g, i: (slab, g, b, 0))

    def cmp_spec(s):
        return pl.BlockSpec((1, 1, 1, n_chunk, HEAD_DIM), lambda b, g, i: (s, b, g, 0, 0))

    return pl.pallas_call(
        _nsa_kernel,
        out_shape=jax.ShapeDtypeStruct((t, Q_COLS), _BF16),
        grid=(batch, N_KV_GROUPS, nq),
        in_specs=[pl.BlockSpec((1, GROUP_SIZE, tq, HEAD_DIM), lambda b, g, i: (g, 0, b * nq + i, 0)),
                  cmp_spec(0), cmp_spec(1),
                  kv_spec(N_Q_SLABS), kv_spec(N_Q_SLABS + 1),
                  kv_spec(N_Q_SLABS + 2), kv_spec(N_Q_SLABS + 3),
                  pl.BlockSpec((1, tq, 3 * GROUP_SIZE), lambda b, g, i: (g, b * nq + i, 0)),
                  pl.BlockSpec((n_blk, n_chunk), lambda b, g, i: (0, 0)),
                  pl.BlockSpec((LANES, seq), lambda b, g, i: (0, 0))],
        out_specs=pl.BlockSpec((tq, HEAD_TILE), lambda b, g, i: (b * nq + i, g)),
        compiler_params=_params("parallel", "parallel", "arbitrary"),
        name="nsa",
    )(heads, cmp_kv, cmp_kv, heads, heads, heads, heads, gates, amat, emat)


def _mix_kernel(c_ref, o_ref, h_ref, wpc_ref, wpn_ref, wmc_ref, wmn_ref, bc_ref, bn_ref, out_ref):
    h = h_ref[...]
    g_conv = _sigmoid(_dot(h, wmc_ref[...]) + bc_ref[...])
    g_nsa = _sigmoid(_dot(h, wmn_ref[...]) + bn_ref[...])
    y_conv = _dot(c_ref[...], wpc_ref[...])
    y_nsa = _dot(o_ref[...], wpn_ref[...])
    out_ref[...] = (g_conv * y_conv + g_nsa * y_nsa).astype(out_ref.dtype)


def _mix(c, o, h, wpc, wpn, wm, bm):
    t, d = h.shape
    tm, tn = MIX_TILE, MIX_COLS
    nb = d // tn
    bm2 = bm.reshape(1, 2 * d)
    return pl.pallas_call(
        _mix_kernel,
        out_shape=jax.ShapeDtypeStruct((t, d), _BF16),
        grid=(t // tm, nb),
        in_specs=[pl.BlockSpec((tm, CONV_CHANNELS), lambda i, j: (i, 0)),
                  pl.BlockSpec((tm, Q_COLS), lambda i, j: (i, 0)),
                  pl.BlockSpec((tm, d), lambda i, j: (i, 0)),
                  pl.BlockSpec((CONV_CHANNELS, tn), lambda i, j: (0, j)),
                  pl.BlockSpec((Q_COLS, tn), lambda i, j: (0, j)),
                  pl.BlockSpec((d, tn), lambda i, j: (0, j)),
                  pl.BlockSpec((d, tn), lambda i, j: (0, j + nb)),
                  pl.BlockSpec((1, tn), lambda i, j: (0, j)),
                  pl.BlockSpec((1, tn), lambda i, j: (0, j + nb))],
        out_specs=pl.BlockSpec((tm, tn), lambda i, j: (i, j)),
        compiler_params=_params("parallel", "arbitrary"),
        name="mix",
    )(c, o, h, wpc, wpn, wm, wm, bm2, bm2)


def _out_router_kernel(mix_ref, x_ref, wout_ref, g_ref, wrh_ref, wrl_ref, br_ref, x1_ref, h2_ref, route_ref):
    x1 = x_ref[...] + _dot(mix_ref[...], wout_ref[...])
    x1_ref[...] = x1
    ms = jnp.mean(x1 * x1, axis=-1, keepdims=True)
    h2 = x1 * lax.rsqrt(ms + NORM_EPS) * g_ref[...]
    h2_ref[...] = h2
    h_hi = h2.astype(_BF16)
    h_lo = (h2 - h_hi.astype(_F32)).astype(_BF16)
    logits = (_dot(h_hi, wrh_ref[...]) + _dot(h_lo, wrh_ref[...]) + _dot(h_hi, wrl_ref[...])
              + br_ref[...])
    lane = lax.broadcasted_iota(jnp.int32, (1, LANES), 1)
    lane_f = lane.astype(_F32)
    big = float(LANES)
    gl = jnp.where(lane < N_GROUPS, logits, NEG_INF)
    gmax = jnp.max(gl, axis=-1, keepdims=True)
    p_g = 1.0 / jnp.sum(jnp.where(lane < N_GROUPS, jnp.exp(gl - gmax), 0.0), axis=-1, keepdims=True)
    g_sel = jnp.min(jnp.where(gl == gmax, lane_f, big), axis=-1, keepdims=True)
    lo = N_GROUPS + g_sel * EXPERTS_PER_GROUP
    in_grp = (lane_f >= lo) & (lane_f < lo + EXPERTS_PER_GROUP)
    el = jnp.where(in_grp, logits, NEG_INF)
    v0 = jnp.max(el, axis=-1, keepdims=True)
    i0 = jnp.min(jnp.where(el == v0, lane_f, big), axis=-1, keepdims=True)
    el2 = jnp.where(lane_f == i0, NEG_INF, el)
    v1 = jnp.max(el2, axis=-1, keepdims=True)
    i1 = jnp.min(jnp.where(el2 == v1, lane_f, big), axis=-1, keepdims=True)
    t1 = jnp.exp(v1 - v0)
    w0 = p_g / (1.0 + t1)
    w1 = p_g * t1 / (1.0 + t1)
    route = jnp.where(lane == 0, i0 - N_GROUPS, 0.0)
    route = jnp.where(lane == 1, i1 - N_GROUPS, route)
    route = jnp.where(lane == 2, w0, route)
    route = jnp.where(lane == 3, w1, route)
    route_ref[...] = route


def _out_router(mixed, x, w_out, ffn_g, w_route, b_route):
    t, d = x.shape
    tm = OUT_TILE
    row = pl.BlockSpec((tm, d), lambda i: (i, 0))
    w_route_hi = w_route.astype(_BF16)
    w_route_lo = (w_route - w_route_hi.astype(_F32)).astype(_BF16)
    return pl.pallas_call(
        _out_router_kernel,
        out_shape=(jax.ShapeDtypeStruct((t, d), _F32),
                   jax.ShapeDtypeStruct((t, d), _F32),
                   jax.ShapeDtypeStruct((t, LANES), _F32)),
        grid=(t // tm,),
        in_specs=[row, row,
                  pl.BlockSpec((d, d), lambda i: (0, 0), pipeline_mode=pl.Buffered(1)),
                  pl.BlockSpec((1, d), lambda i: (0, 0)),
                  pl.BlockSpec((d, LANES), lambda i: (0, 0)),
                  pl.BlockSpec((d, LANES), lambda i: (0, 0)),
                  pl.BlockSpec((1, LANES), lambda i: (0, 0))],
        out_specs=(row, row, pl.BlockSpec((tm, LANES), lambda i: (i, 0))),
        compiler_params=_params("parallel"),
        name="out_router",
    )(mixed, x, w_out, ffn_g.reshape(1, d), w_route_hi, w_route_lo, b_route)


def _moe_kernel(be_ref, tok_ref, nb_ref, h2_hbm, w1_ref, w3_ref, w2_ref, y_ref, xbuf, w1b, w3b, w2b, sem):
    i = pl.program_id(0)
    nb = nb_ref[0]
    slot = i % 2
    bm = xbuf.shape[1]

    def row_copy(tok, r, sl):
        return pltpu.make_async_copy(h2_hbm.at[pl.ds(tok, 1), :], xbuf.at[sl, pl.ds(r, 1), :], sem.at[sl])

    @pl.when(i == 0)
    def _():
        def body(r, carry):
            row_copy(tok_ref[r], r, 0).start()
            return carry
        lax.fori_loop(0, bm, body, 0, unroll=DMA_UNROLL)

    for par in range(2):
        @pl.when((i + 1 < nb) & (slot == par))
        def _(par=par):
            base = (i + 1) * bm
            for r in range(bm):
                row_copy(tok_ref[base + r], r, 1 - par).start()

    @pl.when(i < nb)
    def _():
        pltpu.make_async_copy(h2_hbm.at[pl.ds(0, bm), :], xbuf.at[slot], sem.at[slot]).wait()

        changed = (i == 0) | (be_ref[i] != be_ref[jnp.maximum(i - 1, 0)])

        @pl.when(changed)
        def _():
            w1b[...] = w1_ref[0].astype(_BF16)
            w3b[...] = w3_ref[0].astype(_BF16)
            w2b[...] = w2_ref[0].astype(_BF16)

        x = xbuf[slot].astype(_BF16)
        a = _dot(x, w1b[...])
        b = _dot(x, w3b[...])
        act = (a * _sigmoid(a) * b).astype(_BF16)
        y_ref[...] = _dot(act, w2b[...])

    @pl.when(i >= nb)
    def _():
        y_ref[...] = jnp.zeros(y_ref.shape, y_ref.dtype)


def _moe(h2, tables, w1, w3, w2):
    t, d = h2.shape
    bm = MOE_ROWS
    block_e, row_tok, n_used = tables
    n_blocks = block_e.shape[0]
    ff = w1.shape[2]
    grid_spec = pltpu.PrefetchScalarGridSpec(
        num_scalar_prefetch=3,
        grid=(n_blocks,),
        in_specs=[pl.BlockSpec(memory_space=pl.ANY),
                  pl.BlockSpec((1, d, ff), lambda i, be, tok, nb: (be[i], 0, 0)),
                  pl.BlockSpec((1, d, ff), lambda i, be, tok, nb: (be[i], 0, 0)),
                  pl.BlockSpec((1, ff, d), lambda i, be, tok, nb: (be[i], 0, 0))],
        out_specs=pl.BlockSpec((bm, d), lambda i, be, tok, nb: (i, 0)),
        scratch_shapes=[pltpu.VMEM((2, bm, d), _F32),
                        pltpu.VMEM((d, ff), _BF16),
                        pltpu.VMEM((d, ff), _BF16),
                        pltpu.VMEM((ff, d), _BF16),
                        pltpu.SemaphoreType.DMA((2,))],
    )
    return pl.pallas_call(
        _moe_kernel,
        out_shape=jax.ShapeDtypeStruct((n_blocks * bm, d), _F32),
        grid_spec=grid_spec,
        compiler_params=_params("arbitrary", disable_bounds_checks=True),
        name="moe_experts",
    )(block_e, row_tok, n_used, h2, w1, w3, w2)


def _final_kernel(pos_ref, y_hbm, x1_ref, route_ref, g_ref, o_ref, ybuf, sem):
    i = pl.program_id(0)
    n = pl.num_programs(0)
    tm = x1_ref.shape[0]

    def row_copy(p, r, k, sl):
        return pltpu.make_async_copy(y_hbm.at[pl.ds(p, 1), :], ybuf.at[sl, k, pl.ds(r, 1), :], sem.at[sl])

    @pl.when(i == 0)
    def _():
        def body(r, carry):
            for k in range(TOP_K):
                row_copy(pos_ref[r * TOP_K + k], r, k, 0).start()
            return carry
        lax.fori_loop(0, tm, body, 0, unroll=DMA_UNROLL)

    def step(par, issue_next):
        if issue_next:
            base = (i + 1) * tm * TOP_K
            for r in range(tm):
                for k in range(TOP_K):
                    row_copy(pos_ref[base + r * TOP_K + k], r, k, 1 - par).start()
        for k in range(TOP_K):
            pltpu.make_async_copy(y_hbm.at[pl.ds(0, tm), :], ybuf.at[par, k], sem.at[par]).wait()
        route = route_ref[...]
        x = x1_ref[...] + route[:, 2:3] * ybuf[par, 0] + route[:, 3:4] * ybuf[par, 1]
        ms = jnp.mean(x * x, axis=-1, keepdims=True)
        o_ref[...] = x * lax.rsqrt(ms + NORM_EPS) * g_ref[...]

    for par in range(2):
        @pl.when((i % 2 == par) & (i + 1 < n))
        def _(par=par):
            step(par, True)

        @pl.when((i % 2 == par) & (i + 1 == n))
        def _(par=par):
            step(par, False)


def _final(pos, y_rows, x1, route, g):
    t, d = x1.shape
    tm = FINAL_TILE
    grid_spec = pltpu.PrefetchScalarGridSpec(
        num_scalar_prefetch=1,
        grid=(t // tm,),
        in_specs=[pl.BlockSpec(memory_space=pl.ANY),
                  pl.BlockSpec((tm, d), lambda i, pos: (i, 0)),
                  pl.BlockSpec((tm, LANES), lambda i, pos: (i, 0)),
                  pl.BlockSpec((1, d), lambda i, pos: (0, 0))],
        out_specs=pl.BlockSpec((tm, d), lambda i, pos: (i, 0)),
        scratch_shapes=[pltpu.VMEM((2, TOP_K, tm, d), _F32),
                        pltpu.SemaphoreType.DMA((2,))],
    )
    return pl.pallas_call(
        _final_kernel,
        out_shape=jax.ShapeDtypeStruct((t, d), _F32),
        grid_spec=grid_spec,
        compiler_params=_params("arbitrary", disable_bounds_checks=True),
        name="combine_norm",
    )(pos, y_rows, x1, route, g.reshape(1, d))


def _routing_tables(expert, n_tokens):
    bm = MOE_ROWS
    n_assign = n_tokens * TOP_K
    n_blocks = -(-(n_assign + N_EXPERTS * (bm - 1)) // bm)
    i32 = jnp.int32
    flat_e = expert.reshape(-1)
    experts = jnp.arange(N_EXPERTS, dtype=i32)
    assign = jnp.arange(n_assign, dtype=i32)
    e_sorted, order = lax.sort((flat_e, assign), num_keys=1)
    start = jnp.sum((e_sorted[:, None] < experts[None, :]).astype(i32), axis=0)
    counts = jnp.concatenate([start[1:], jnp.full((1,), n_assign, i32)]) - start
    padded = (counts + bm - 1) // bm * bm
    pend = jnp.cumsum(padded)
    pstart = pend - padded
    shift = pstart - start
    dshift = shift - jnp.concatenate([jnp.zeros((1,), i32), shift[:-1]])
    shift_sorted = jnp.sum(jnp.where(assign[:, None] >= start[None, :], dshift[None, :], 0), axis=1)
    _, pos = lax.sort((order, assign + shift_sorted), num_keys=1)
    block_row0 = jnp.arange(n_blocks, dtype=i32) * bm
    block_e = jnp.minimum(jnp.sum((pend[None, :] <= block_row0[:, None]).astype(i32), axis=1), N_EXPERTS - 1)
    onehot = block_e[:, None] == experts[None, :]

    def per_block(table):
        return jnp.sum(jnp.where(onehot, table[None, :], 0), axis=1)

    block_start = per_block(start - pstart) + block_row0
    block_end = per_block(start + counts)
    src = block_start[:, None] + jnp.arange(bm, dtype=i32)[None, :]
    valid = src < block_end[:, None]
    tok_sorted = order // TOP_K
    row_tok = jnp.where(valid, tok_sorted[jnp.clip(src, 0, n_assign - 1)], 0)
    n_used = (pend[-1] // bm).astype(i32).reshape(1)
    return (block_e, row_tok.reshape(-1).astype(i32), n_used), pos.astype(i32)


def kernel(x, attn_norm, w_in, conv_dw, conv_dw_b, conv_ln_g, conv_ln_b, cmp_pos_k, cmp_k_w1, cmp_k_w2, cmp_pos_v, cmp_v_w1, cmp_v_w2, w_proj_conv, w_proj_nsa, w_merge, b_merge, w_out, ffn_norm, w_grp, b_grp, w_exp, b_exp, exp_w1, exp_w3, exp_w2, final_norm):
    batch, seq, d = x.shape
    t = batch * seq
    assert w_in.shape[0] == 1, "the block is specified with a single layer"
    xf = x.reshape(t, d)

    w_in_b = w_in[0].astype(_BF16)
    w_gate = w_in[0][:, 2 * CONV_CHANNELS + HEAD_COLS:].reshape(d, 3, N_KV_GROUPS, GROUP_SIZE)
    w_gate = w_gate.transpose(0, 2, 1, 3).reshape(d, GATE_COLS).astype(_BF16)

    h = _rmsnorm(xf, attn_norm[0], _BF16)
    glu = _glu_proj(h, w_in_b)
    tables = _rope_tables(seq)
    heads = _heads_proj(h, w_in_b, tables, seq)
    cmp_flat = _cmp_proj(h, w_in_b, tables, seq)
    gates = _gate_proj(h, w_gate).reshape(t, N_KV_GROUPS, 3 * GROUP_SIZE).transpose(1, 0, 2)

    half = CMP_STRIDE * HEAD_DIM
    pos = jnp.stack([cmp_pos_k[0], cmp_pos_v[0]]).reshape(2, 2, 1, half)
    cw1 = jnp.stack([cmp_k_w1[0], cmp_v_w1[0]]).astype(_BF16)
    cw2 = jnp.stack([cmp_k_w2[0], cmp_v_w2[0]]).astype(_BF16)
    cmp_kv = _compress(cmp_flat, pos, cw1, cw2, batch, seq)

    c = _conv(glu, conv_dw[0], conv_dw_b[0], conv_ln_g[0], conv_ln_b[0], batch, seq)
    o = _nsa(heads, cmp_kv, gates, batch, seq)
    mixed = _mix(c, o, h, w_proj_conv[0].astype(_BF16), w_proj_nsa[0].astype(_BF16),
                 w_merge[0].astype(_BF16), b_merge[0])

    w_route = jnp.concatenate(
        [w_grp[0], w_exp[0], jnp.zeros((d, LANES - N_GROUPS - N_EXPERTS), _F32)], axis=1)
    b_route = jnp.concatenate(
        [b_grp[0], b_exp[0], jnp.zeros((LANES - N_GROUPS - N_EXPERTS,), _F32)]).reshape(1, LANES)
    x1, h2, route = _out_router(mixed, xf, w_out[0].astype(_BF16), ffn_norm[0], w_route, b_route)

    expert = route[:, :TOP_K].astype(jnp.int32)
    moe_tables, pos_rows = _routing_tables(expert, t)
    y_rows = _moe(h2, moe_tables, exp_w1[0], exp_w3[0], exp_w2[0])
    out = _final(pos_rows, y_rows, x1, route, final_norm)
    return out.reshape(batch, seq, d)
```

```python
import numpy as np
import jax
import jax.numpy as jnp
from jax import lax
from jax.experimental import pallas as pl
from jax.experimental.pallas import tpu as pltpu

D_MODEL = 2048
CONV_CHANNELS = D_MODEL // 2
DW_TAPS = 31
N_HEADS = 16
HEAD_DIM = 128
N_KV_GROUPS = 4
GROUP_SIZE = N_HEADS // N_KV_GROUPS
ROT_DIM = HEAD_DIM // 4
ROPE_THETA = 500000.0
CMP_BLOCK = 32
CMP_STRIDE = 16
CMP_HIDDEN = 256
SEL_BLOCK = 64
N_SELECT = 16
N_FORCED_LOCAL = 2
WINDOW = 512
N_GROUPS = 8
EXPERTS_PER_GROUP = 8
N_EXPERTS = N_GROUPS * EXPERTS_PER_GROUP
TOP_K = 2
EXPERT_FF = 512
Q_COLS = N_HEADS * HEAD_DIM
KV_COLS = N_KV_GROUPS * HEAD_DIM
HEAD_COLS = Q_COLS + 6 * KV_COLS
GATE_COLS = 3 * N_HEADS
NORM_EPS = 1e-6
NEG_INF = -1e30
FORCE_BONUS = 1e4
LOG2_E = 1.4426950408889634

LANES = 128
SUBLANES = 8
VMEM_LIMIT_BYTES = 56 * 1024 * 1024

ROW_TILE = 512
PROJ_TILE = 2048
MIX_TILE = 1024
MIX_COLS = 256
HEAD_TILE = GROUP_SIZE * HEAD_DIM
Q_TILE = 256
SEL_CHUNK = 512
CONV_TILE = 256
CONV_HALO = 32
CONV_ROWS = 16
OUT_TILE = 512
MOE_ROWS = 512
FINAL_TILE = 128
DMA_UNROLL = 8

_BF16 = jnp.bfloat16
_F32 = jnp.float32


def _params(*sem, **kw):
    return pltpu.CompilerParams(dimension_semantics=sem, vmem_limit_bytes=VMEM_LIMIT_BYTES, **kw)


def _dot(a, b):
    return jnp.dot(a, b, preferred_element_type=_F32)


def _dot_nt(a, b):
    return lax.dot_general(a, b, (((1,), (1,)), ((), ())), preferred_element_type=_F32)


def _sigmoid(x):
    return 1.0 / (1.0 + jnp.exp(-x))


def _rmsnorm_kernel(x_ref, g_ref, o_ref):
    x = x_ref[...]
    ms = jnp.mean(x * x, axis=-1, keepdims=True)
    o_ref[...] = (x * lax.rsqrt(ms + NORM_EPS) * g_ref[...]).astype(o_ref.dtype)


def _rmsnorm(x, g, out_dtype):
    t, d = x.shape
    return pl.pallas_call(
        _rmsnorm_kernel,
        out_shape=jax.ShapeDtypeStruct((t, d), out_dtype),
        grid=(t // ROW_TILE,),
        in_specs=[pl.BlockSpec((ROW_TILE, d), lambda i: (i, 0)),
                  pl.BlockSpec((1, d), lambda i: (0, 0))],
        out_specs=pl.BlockSpec((ROW_TILE, d), lambda i: (i, 0)),
        compiler_params=_params("parallel"),
        name="rmsnorm",
    )(x, g.reshape(1, d))


def _glu_kernel(h_ref, wa_ref, wb_ref, o_ref):
    h = h_ref[...]
    a = _dot(h, wa_ref[...])
    b = _dot(h, wb_ref[...])
    o_ref[...] = a * _sigmoid(b)


def _glu_proj(h, w_in_bf16):
    t, d = h.shape
    tn = HEAD_TILE
    nb = CONV_CHANNELS // tn
    return pl.pallas_call(
        _glu_kernel,
        out_shape=jax.ShapeDtypeStruct((t, CONV_CHANNELS), _F32),
        grid=(t // PROJ_TILE, nb),
        in_specs=[pl.BlockSpec((PROJ_TILE, d), lambda i, j: (i, 0)),
                  pl.BlockSpec((d, tn), lambda i, j: (0, j)),
                  pl.BlockSpec((d, tn), lambda i, j: (0, j + nb))],
        out_specs=pl.BlockSpec((PROJ_TILE, tn), lambda i, j: (i, j)),
        compiler_params=_params("parallel", "arbitrary"),
        name="glu_proj",
    )(h, w_in_bf16, w_in_bf16)


N_Q_SLABS = Q_COLS // HEAD_TILE
N_HEAD_SLABS = N_Q_SLABS + 4
Q_COL_BLOCK = 2 * CONV_CHANNELS // HEAD_TILE
CMP_COL_BLOCK = Q_COL_BLOCK + N_Q_SLABS
KV_COL_BLOCK = CMP_COL_BLOCK + 2


def _rope_tables(seq):
    pos = jnp.arange(seq, dtype=_F32)
    inv = ROPE_THETA ** (-jnp.arange(0, ROT_DIM, 2, dtype=_F32) / ROT_DIM)
    ang = pos[:, None] * inv[None, :]
    cos, sin = jnp.cos(ang), jnp.sin(ang)
    half = ROT_DIM // 2
    ones = jnp.ones((seq, HEAD_DIM - ROT_DIM), _F32)
    zeros_h = jnp.zeros((seq, half), _F32)
    zeros_r = jnp.zeros((seq, HEAD_DIM - ROT_DIM), _F32)
    c = jnp.concatenate([cos, cos, ones], axis=-1)
    s_lo = jnp.concatenate([zeros_h, sin, zeros_r], axis=-1)
    s_hi = jnp.concatenate([-sin, zeros_h, zeros_r], axis=-1)
    q_scale = HEAD_DIM ** -0.5 * LOG2_E
    zeros = jnp.zeros_like(c)
    return (jnp.stack([c * q_scale, c, jnp.ones_like(c)]),
            jnp.stack([s_lo * q_scale, s_lo, zeros]),
            jnp.stack([s_hi * q_scale, s_hi, zeros]))


ROPE_Q, ROPE_K, ROPE_NONE = 0, 1, 2


def _rotate_head(xs, c, slo, shi):
    half = ROT_DIM // 2
    return xs * c + pltpu.roll(xs, half, 1) * slo + pltpu.roll(xs, HEAD_DIM - half, 1) * shi


def _heads_kernel(h_ref, w_ref, c_ref, slo_ref, shi_ref, o_ref):
    acc = _dot(h_ref[...], w_ref[...])
    c, slo, shi = c_ref[0], slo_ref[0], shi_ref[0]
    for hd in range(GROUP_SIZE):
        r = _rotate_head(acc[:, hd * HEAD_DIM:(hd + 1) * HEAD_DIM], c, slo, shi)
        o_ref[0, hd] = r.astype(o_ref.dtype)


def _heads_proj(h, w_in_bf16, tables, seq):
    t, d = h.shape
    tm = PROJ_TILE
    spt = seq // tm

    def variant(j):
        return jnp.where(j < N_Q_SLABS, ROPE_Q, jnp.where(j % 2 == 0, ROPE_K, ROPE_NONE))

    tab = pl.BlockSpec((1, tm, HEAD_DIM), lambda i, j: (variant(j), i % spt, 0))

    def w_block(i, j):
        return (0, jnp.where(j < N_Q_SLABS, j + Q_COL_BLOCK, j - N_Q_SLABS + KV_COL_BLOCK))

    return pl.pallas_call(
        _heads_kernel,
        out_shape=jax.ShapeDtypeStruct((N_HEAD_SLABS, GROUP_SIZE, t, HEAD_DIM), _BF16),
        grid=(t // tm, N_HEAD_SLABS),
        in_specs=[pl.BlockSpec((tm, d), lambda i, j: (i, 0)),
                  pl.BlockSpec((d, HEAD_TILE), w_block),
                  tab, tab, tab],
        out_specs=pl.BlockSpec((1, GROUP_SIZE, tm, HEAD_DIM), lambda i, j: (j, 0, i, 0)),
        compiler_params=_params("parallel", "arbitrary"),
        name="heads_proj",
    )(h, w_in_bf16, *tables)


def _cmp_proj_kernel(h_ref, w_ref, c_ref, slo_ref, shi_ref, o_ref, stage_ref):
    acc = _dot(h_ref[...], w_ref[...])
    c, slo, shi = c_ref[0], slo_ref[0], shi_ref[0]
    for g in range(N_KV_GROUPS):
        stage_ref[g] = _rotate_head(acc[:, g * HEAD_DIM:(g + 1) * HEAD_DIM], c, slo, shi)

    n_chunk = stage_ref.shape[1] // CMP_STRIDE
    for g in range(N_KV_GROUPS):
        for l in range(CMP_STRIDE):
            rows = stage_ref[g, pl.ds(l, n_chunk, stride=CMP_STRIDE), :]
            o_ref[0, g, :, l * HEAD_DIM:(l + 1) * HEAD_DIM] = rows.astype(o_ref.dtype)


def _cmp_proj(h, w_in_bf16, tables, seq):
    t, d = h.shape
    tm = PROJ_TILE
    spt = seq // tm
    tab = pl.BlockSpec((1, tm, HEAD_DIM), lambda i, j: (jnp.where(j == 0, ROPE_K, ROPE_NONE), i % spt, 0))
    return pl.pallas_call(
        _cmp_proj_kernel,
        out_shape=jax.ShapeDtypeStruct((2, N_KV_GROUPS, t // CMP_STRIDE, CMP_STRIDE * HEAD_DIM), _BF16),
        grid=(t // tm, 2),
        in_specs=[pl.BlockSpec((tm, d), lambda i, j: (i, 0)),
                  pl.BlockSpec((d, HEAD_TILE), lambda i, j: (0, j + CMP_COL_BLOCK)),
                  tab, tab, tab],
        out_specs=pl.BlockSpec((1, N_KV_GROUPS, tm // CMP_STRIDE, CMP_STRIDE * HEAD_DIM),
                               lambda i, j: (j, 0, i, 0)),
        scratch_shapes=[pltpu.VMEM((N_KV_GROUPS, tm, HEAD_DIM), _F32)],
        compiler_params=_params("parallel", "arbitrary"),
        name="cmp_proj",
    )(h, w_in_bf16, *tables)


def _gate_kernel(h_ref, w_ref, o_ref):
    o_ref[...] = _sigmoid(_dot(h_ref[...], w_ref[...]))


def _gate_proj(h, w_gate_bf16):
    t, d = h.shape
    n = w_gate_bf16.shape[1]
    return pl.pallas_call(
        _gate_kernel,
        out_shape=jax.ShapeDtypeStruct((t, n), _F32),
        grid=(t // ROW_TILE,),
        in_specs=[pl.BlockSpec((ROW_TILE, d), lambda i: (i, 0)),
                  pl.BlockSpec((d, n), lambda i: (0, 0))],
        out_specs=pl.BlockSpec((ROW_TILE, n), lambda i: (i, 0)),
        compiler_params=_params("parallel"),
        name="gate_proj",
    )(h, w_gate_bf16)


def _compress_kernel(kv_ref, pos_ref, w1_ref, w2_ref, o_ref):
    half = CMP_STRIDE * HEAD_DIM
    c = kv_ref[0, 0].astype(_F32)
    n_chunk = c.shape[0]
    top = (c + pos_ref[0, 0]).astype(_BF16)
    bot = (c + pos_ref[0, 1]).astype(_BF16)
    u = _dot(top, w1_ref[0, :half, :])
    v = _dot(bot, w1_ref[0, half:, :])
    hidden = u + pltpu.roll(v, n_chunk - 1, 0)
    act = jax.nn.gelu(hidden).astype(_BF16)
    o_ref[0, 0, 0] = _dot(act, w2_ref[0]).astype(o_ref.dtype)


def _compress(cmp_flat, pos, w1, w2, batch, seq):
    n_chunk = seq // CMP_STRIDE
    return pl.pallas_call(
        _compress_kernel,
        out_shape=jax.ShapeDtypeStruct((2, batch, N_KV_GROUPS, n_chunk, HEAD_DIM), _BF16),
        grid=(2, batch, N_KV_GROUPS),
        in_specs=[pl.BlockSpec((1, 1, n_chunk, CMP_STRIDE * HEAD_DIM), lambda s, b, g: (s, g, b, 0)),
                  pl.BlockSpec((1, 2, 1, CMP_STRIDE * HEAD_DIM), lambda s, b, g: (s, 0, 0, 0)),
                  pl.BlockSpec((1, CMP_BLOCK * HEAD_DIM, CMP_HIDDEN), lambda s, b, g: (s, 0, 0)),
                  pl.BlockSpec((1, CMP_HIDDEN, HEAD_DIM), lambda s, b, g: (s, 0, 0))],
        out_specs=pl.BlockSpec((1, 1, 1, n_chunk, HEAD_DIM), lambda s, b, g: (s, b, g, 0, 0)),
        compiler_params=_params("parallel", "parallel", "parallel"),
        name="compress",
    )(cmp_flat, pos, w1, w2)


def _conv_kernel(cur_ref, halo_ref, dw_ref, dwb_ref, g_ref, b_ref, o_ref, ext_ref, sh_ref, acc_ref):
    i = pl.program_id(1)
    ts = cur_ref.shape[0]
    ext_ref[:CONV_HALO, :] = jnp.where(i > 0, halo_ref[...], 0.0)
    ext_ref[CONV_HALO:, :] = cur_ref[...]
    n_sh = sh_ref.shape[1]
    for b in range(1, SUBLANES):
        sh_ref[b - 1] = ext_ref[b:b + n_sh, :]
    off = CONV_HALO - (DW_TAPS - 1)

    def chunk(c, carry):
        r0 = pl.multiple_of(c * CONV_ROWS, CONV_ROWS)
        y = None
        for k in range(DW_TAPS):
            a, b = divmod(off + k, SUBLANES)
            src = ext_ref if b == 0 else sh_ref.at[b - 1]
            term = src[pl.ds(r0 + SUBLANES * a, CONV_ROWS), :] * dw_ref[k:k + 1, :]
            y = term if y is None else y + term
        acc_ref[pl.ds(r0, CONV_ROWS), :] = y
        return carry

    lax.fori_loop(0, ts // CONV_ROWS, chunk, 0)
    y = acc_ref[...] + dwb_ref[...]
    mu = jnp.mean(y, axis=-1, keepdims=True)
    yc = y - mu
    var = jnp.mean(yc * yc, axis=-1, keepdims=True)
    z = yc * lax.rsqrt(var + NORM_EPS) * g_ref[...] + b_ref[...]
    o_ref[...] = (z * _sigmoid(z)).astype(o_ref.dtype)


def _conv(glu, dw, dw_b, ln_g, ln_b, batch, seq):
    t, c = glu.shape
    ts = CONV_TILE
    nt = seq // ts
    hb = ts // CONV_HALO
    dw_pad = jnp.concatenate([dw, jnp.zeros((CONV_HALO - DW_TAPS, c), _F32)], axis=0)
    vec = pl.BlockSpec((1, c), lambda b, i: (0, 0))
    return pl.pallas_call(
        _conv_kernel,
        out_shape=jax.ShapeDtypeStruct((t, c), _BF16),
        grid=(batch, nt),
        in_specs=[pl.BlockSpec((ts, c), lambda b, i: (b * nt + i, 0)),
                  pl.BlockSpec((CONV_HALO, c), lambda b, i: (jnp.maximum((b * nt + i) * hb - 1, 0), 0)),
                  pl.BlockSpec((CONV_HALO, c), lambda b, i: (0, 0)),
                  vec, vec, vec],
        out_specs=pl.BlockSpec((ts, c), lambda b, i: (b * nt + i, 0)),
        scratch_shapes=[pltpu.VMEM((ts + CONV_HALO, c), _F32),
                        pltpu.VMEM((SUBLANES - 1, ts + CONV_HALO - SUBLANES, c), _F32),
                        pltpu.VMEM((ts, c), _F32)],
        compiler_params=_params("parallel", "arbitrary"),
        name="conv",
    )(glu, glu, dw_pad, dw_b.reshape(1, c), ln_g.reshape(1, c), ln_b.reshape(1, c))


def _importance_matrix(n_cmp_pad, n_blk):
    r_sel, r_cmp = SEL_BLOCK // CMP_STRIDE, CMP_BLOCK // CMP_STRIDE
    a = np.zeros((n_blk, n_cmp_pad), np.float32)
    for j in range(n_blk):
        for m in range(r_sel):
            for n in range(r_cmp):
                c = r_sel * j + m - n
                if 0 <= c < n_cmp_pad - 1:
                    a[j, c] += 1.0
    return a


def _expand_matrix(n_blk, seq):
    e = np.zeros((LANES, seq), np.float32)
    for j in range(n_blk):
        e[j, j * SEL_BLOCK:(j + 1) * SEL_BLOCK] = 1.0
    return e


def _nsa_step(n_active, q_ref, kc_ref, vc_ref, ks_ref, vs_ref, kw_ref, vw_ref, gate_ref, amat_ref, emat_ref,
              o_ref):
    tq_n = q_ref.shape[2]
    n_cmp_pad = kc_ref.shape[3]
    n_blk = amat_ref.shape[0]
    rows = GROUP_SIZE * tq_n
    start = pl.program_id(2) * tq_n
    q = q_ref[0].reshape(rows, HEAD_DIM)
    tq = start + lax.broadcasted_iota(jnp.int32, (tq_n, 1), 0)

    span = WINDOW + tq_n
    kstart = pl.multiple_of(jnp.maximum(start - WINDOW, 0), tq_n)
    kpos = kstart + lax.broadcasted_iota(jnp.int32, (1, span), 1)
    wbias = jnp.where((kpos <= tq) & (tq - kpos < WINDOW), 0.0, NEG_INF)
    sw = _dot_nt(q, kw_ref[0, 0, pl.ds(kstart, span), :]).reshape(GROUP_SIZE, tq_n, span) + wbias[None]

    cidx = lax.broadcasted_iota(jnp.int32, (1, n_cmp_pad), 1)
    cvalid = (cidx * CMP_STRIDE + (CMP_BLOCK - 1) <= tq) & (cidx < n_cmp_pad - 1)
    cbias = jnp.where(cvalid, 0.0, NEG_INF)
    s = _dot_nt(q, kc_ref[0, 0, 0]).reshape(GROUP_SIZE, tq_n, n_cmp_pad) + cbias[None]
    e_w = jnp.exp2(sw - jnp.max(sw, axis=-1, keepdims=True)).astype(_BF16)
    e = jnp.exp2(s - jnp.max(s, axis=-1, keepdims=True))
    inv_l = jnp.where((tq >= CMP_BLOCK - 1)[None], 1.0 / jnp.sum(e, axis=-1, keepdims=True), 0.0)
    p = e * inv_l
    vw_aug = jnp.concatenate([vw_ref[0, 0, pl.ds(kstart, span), :], jnp.ones((span, HEAD_DIM), _BF16)], axis=1)
    win = _dot(e_w.reshape(rows, span), vw_aug).reshape(GROUP_SIZE, tq_n, 2 * HEAD_DIM)
    o_cmp = _dot(p.reshape(rows, n_cmp_pad).astype(_BF16), vc_ref[0, 0, 0])

    imp = lax.dot_general(amat_ref[...], jnp.sum(p, axis=0), (((1,), (1,)), ((), ())),
                          preferred_element_type=_F32, precision=lax.Precision.HIGHEST)
    tq_row = start + lax.broadcasted_iota(jnp.int32, (1, tq_n), 1)
    blk = lax.broadcasted_iota(jnp.int32, (n_blk, 1), 0)
    cur = tq_row // SEL_BLOCK
    forced = (blk == 0) | ((blk <= cur) & (blk > cur - N_FORCED_LOCAL))
    imp = jnp.where(forced, imp + FORCE_BONUS, imp)
    imp = jnp.where(blk * SEL_BLOCK <= tq_row, imp, NEG_INF)
    rank = jnp.zeros((n_blk, tq_n), _F32)
    for i in range(n_blk):
        row = imp[i:i + 1, :]
        before = (row > imp) | ((row == imp) & (blk > i))
        rank = rank + jnp.where(before, 1.0, 0.0)
    sel_t = jnp.where(rank < float(min(N_SELECT, n_blk)), 1.0, 0.0)
    sel = jnp.concatenate([sel_t, jnp.zeros((LANES - n_blk, tq_n), _F32)], axis=0).T.astype(_BF16)

    ones_c = jnp.ones((SEL_CHUNK, HEAD_DIM), _BF16)

    def scores(c):
        return _dot_nt(q, ks_ref[0, 0, c * SEL_CHUNK:(c + 1) * SEL_CHUNK, :]).reshape(GROUP_SIZE, tq_n, SEL_CHUNK)

    m_run = None
    acc = None
    sc_next = scores(0)
    for c in range(n_active):
        lo = c * SEL_CHUNK
        sc_raw = sc_next
        if c + 1 < n_active:
            sc_next = scores(c + 1)
        chosen = _dot(sel, emat_ref[:, lo:lo + SEL_CHUNK]) > 0.5
        if c == n_active - 1:
            kpos = lo + lax.broadcasted_iota(jnp.int32, (1, SEL_CHUNK), 1)
            chosen = chosen & (kpos <= tq)
        sc = sc_raw + jnp.where(chosen, 0.0, NEG_INF)[None]
        m_c = jnp.max(sc, axis=-1, keepdims=True)
        m_new = m_c if c == 0 else jnp.maximum(m_run, m_c)
        e_c = jnp.exp2(sc - m_new).astype(_BF16)
        v_aug = jnp.concatenate([vs_ref[0, 0, lo:lo + SEL_CHUNK, :], ones_c], axis=1)
        pv = _dot(e_c.reshape(rows, SEL_CHUNK), v_aug).reshape(GROUP_SIZE, tq_n, 2 * HEAD_DIM)
        acc = pv if c == 0 else jnp.exp2(m_run - m_new) * acc + pv
        m_run = m_new

    o_cmp = o_cmp.reshape(GROUP_SIZE, tq_n, HEAD_DIM)
    gates = gate_ref[0]
    for r in range(GROUP_SIZE):
        g_cmp = gates[:, r:r + 1]
        g_slc = gates[:, GROUP_SIZE + r:GROUP_SIZE + r + 1] / acc[r, :, HEAD_DIM:]
        g_win = gates[:, 2 * GROUP_SIZE + r:2 * GROUP_SIZE + r + 1] / win[r, :, HEAD_DIM:]
        o = g_cmp * o_cmp[r] + g_slc * acc[r, :, :HEAD_DIM] + g_win * win[r, :, :HEAD_DIM]
        o_ref[:, r * HEAD_DIM:(r + 1) * HEAD_DIM] = o.astype(o_ref.dtype)


def _nsa_kernel(q_ref, kc_ref, vc_ref, ks_ref, vs_ref, kw_ref, vw_ref, gate_ref, amat_ref, emat_ref, o_ref):
    tq_n = q_ref.shape[2]
    seq = ks_ref.shape[2]
    n_active = (pl.program_id(2) * tq_n + tq_n - 1) // SEL_CHUNK + 1
    for k in range(1, seq // SEL_CHUNK + 1):
        @pl.when(n_active == k)
        def _(k=k):
            _nsa_step(k, q_ref, kc_ref, vc_ref, ks_ref, vs_ref, kw_ref, vw_ref, gate_ref, amat_ref, emat_ref,
                      o_ref)


def _nsa(heads, cmp_kv, gates, batch, seq):
    t = heads.shape[2]
    tq = Q_TILE
    nq = seq // tq
    n_chunk = cmp_kv.shape[3]
    n_blk = seq // SEL_BLOCK
    assert n_chunk == LANES and n_blk <= LANES and seq % SEL_CHUNK == 0 and WINDOW % tq == 0
    amat = jnp.asarray(_importance_matrix(n_chunk, n_blk))
    emat = jnp.asarray(_expand_matrix(n_blk, seq), dtype=_BF16)

    def kv_spec(slab):
        return pl.BlockSpec((1, 1, seq, HEAD_DIM), lambda b, g, i: (slab, g, b, 0))

    def cmp_spec(s):
        return pl.BlockSpec((1, 1, 1, n_chunk, HEAD_DIM), lambda b, g, i: (s, b, g, 0, 0))

    return pl.pallas_call(
        _nsa_kernel,
        out_shape=jax.ShapeDtypeStruct((t, Q_COLS), _BF16),
        grid=(batch, N_KV_GROUPS, nq),
        in_specs=[pl.BlockSpec((1, GROUP_SIZE, tq, HEAD_DIM), lambda b, g, i: (g, 0, b * nq + i, 0)),
                  cmp_spec(0), cmp_spec(1),
                  kv_spec(N_Q_SLABS), kv_spec(N_Q_SLABS + 1),
                  kv_spec(N_Q_SLABS + 2), kv_spec(N_Q_SLABS + 3),
                  pl.BlockSpec((1, tq, 3 * GROUP_SIZE), lambda b, g, i: (g, b * nq + i, 0)),
                  pl.BlockSpec((n_blk, n_chunk), lambda b, g, i: (0, 0)),
                  pl.BlockSpec((LANES, seq), lambda b, g, i: (0, 0))],
        out_specs=pl.BlockSpec((tq, HEAD_TILE), lambda b, g, i: (b * nq + i, g)),
        compiler_params=_params("parallel", "parallel", "arbitrary"),
        name="nsa",
    )(heads, cmp_kv, cmp_kv, heads, heads, heads, heads, gates, amat, emat)


def _mix_kernel(c_ref, o_ref, h_ref, wpc_ref, wpn_ref, wmc_ref, wmn_ref, bc_ref, bn_ref, out_ref):
    h = h_ref[...]
    g_conv = _sigmoid(_dot(h, wmc_ref[...]) + bc_ref[...])
    g_nsa = _sigmoid(_dot(h, wmn_ref[...]) + bn_ref[...])
    y_conv = _dot(c_ref[...], wpc_ref[...])
    y_nsa = _dot(o_ref[...], wpn_ref[...])
    out_ref[...] = (g_conv * y_conv + g_nsa * y_nsa).astype(out_ref.dtype)


def _mix(c, o, h, wpc, wpn, wm, bm):
    t, d = h.shape
    tm, tn = MIX_TILE, MIX_COLS
    nb = d // tn
    bm2 = bm.reshape(1, 2 * d)
    return pl.pallas_call(
        _mix_kernel,
        out_shape=jax.ShapeDtypeStruct((t, d), _BF16),
        grid=(t // tm, nb),
        in_specs=[pl.BlockSpec((tm, CONV_CHANNELS), lambda i, j: (i, 0)),
                  pl.BlockSpec((tm, Q_COLS), lambda i, j: (i, 0)),
                  pl.BlockSpec((tm, d), lambda i, j: (i, 0)),
                  pl.BlockSpec((CONV_CHANNELS, tn), lambda i, j: (0, j)),
                  pl.BlockSpec((Q_COLS, tn), lambda i, j: (0, j)),
                  pl.BlockSpec((d, tn), lambda i, j: (0, j)),
                  pl.BlockSpec((d, tn), lambda i, j: (0, j + nb)),
                  pl.BlockSpec((1, tn), lambda i, j: (0, j)),
                  pl.BlockSpec((1, tn), lambda i, j: (0, j + nb))],
        out_specs=pl.BlockSpec((tm, tn), lambda i, j: (i, j)),
        compiler_params=_params("parallel", "arbitrary"),
        name="mix",
    )(c, o, h, wpc, wpn, wm, wm, bm2, bm2)


def _out_router_kernel(mix_ref, x_ref, wout_ref, g_ref, wrh_ref, wrl_ref, br_ref, x1_ref, h2_ref, route_ref):
    x1 = x_ref[...] + _dot(mix_ref[...], wout_ref[...])
    x1_ref[...] = x1
    ms = jnp.mean(x1 * x1, axis=-1, keepdims=True)
    h2 = x1 * lax.rsqrt(ms + NORM_EPS) * g_ref[...]
    h2_ref[...] = h2
    h_hi = h2.astype(_BF16)
    h_lo = (h2 - h_hi.astype(_F32)).astype(_BF16)
    logits = (_dot(h_hi, wrh_ref[...]) + _dot(h_lo, wrh_ref[...]) + _dot(h_hi, wrl_ref[...])
              + br_ref[...])
    lane = lax.broadcasted_iota(jnp.int32, (1, LANES), 1)
    lane_f = lane.astype(_F32)
    big = float(LANES)
    gl = jnp.where(lane < N_GROUPS, logits, NEG_INF)
    gmax = jnp.max(gl, axis=-1, keepdims=True)
    p_g = 1.0 / jnp.sum(jnp.where(lane < N_GROUPS, jnp.exp(gl - gmax), 0.0), axis=-1, keepdims=True)
    g_sel = jnp.min(jnp.where(gl == gmax, lane_f, big), axis=-1, keepdims=True)
    lo = N_GROUPS + g_sel * EXPERTS_PER_GROUP
    in_grp = (lane_f >= lo) & (lane_f < lo + EXPERTS_PER_GROUP)
    el = jnp.where(in_grp, logits, NEG_INF)
    v0 = jnp.max(el, axis=-1, keepdims=True)
    i0 = jnp.min(jnp.where(el == v0, lane_f, big), axis=-1, keepdims=True)
    el2 = jnp.where(lane_f == i0, NEG_INF, el)
    v1 = jnp.max(el2, axis=-1, keepdims=True)
    i1 = jnp.min(jnp.where(el2 == v1, lane_f, big), axis=-1, keepdims=True)
    t1 = jnp.exp(v1 - v0)
    w0 = p_g / (1.0 + t1)
    w1 = p_g * t1 / (1.0 + t1)
    route = jnp.where(lane == 0, i0 - N_GROUPS, 0.0)
    route = jnp.where(lane == 1, i1 - N_GROUPS, route)
    route = jnp.where(lane == 2, w0, route)
    route = jnp.where(lane == 3, w1, route)
    route_ref[...] = route


def _out_router(mixed, x, w_out, ffn_g, w_route, b_route):
    t, d = x.shape
    tm = OUT_TILE
    row = pl.BlockSpec((tm, d), lambda i: (i, 0))
    w_route_hi = w_route.astype(_BF16)
    w_route_lo = (w_route - w_route_hi.astype(_F32)).astype(_BF16)
    return pl.pallas_call(
        _out_router_kernel,
        out_shape=(jax.ShapeDtypeStruct((t, d), _F32),
                   jax.ShapeDtypeStruct((t, d), _F32),
                   jax.ShapeDtypeStruct((t, LANES), _F32)),
        grid=(t // tm,),
        in_specs=[row, row,
                  pl.BlockSpec((d, d), lambda i: (0, 0), pipeline_mode=pl.Buffered(1)),
                  pl.BlockSpec((1, d), lambda i: (0, 0)),
                  pl.BlockSpec((d, LANES), lambda i: (0, 0)),
                  pl.BlockSpec((d, LANES), lambda i: (0, 0)),
                  pl.BlockSpec((1, LANES), lambda i: (0, 0))],
        out_specs=(row, row, pl.BlockSpec((tm, LANES), lambda i: (i, 0))),
        compiler_params=_params("parallel"),
        name="out_router",
    )(mixed, x, w_out, ffn_g.reshape(1, d), w_route_hi, w_route_lo, b_route)


def _moe_kernel(be_ref, half_ref, tok_ref, nb_ref, h2_hbm, w1_ref, w3_ref, w2_ref, y_ref, xbuf, w1b, w3b, w2b, sem):
    i = pl.program_id(0)
    nb = nb_ref[0]
    slot = i % 2
    bm = xbuf.shape[1]
    hb = bm // 2

    def row_copy(tok, r, sl):
        return pltpu.make_async_copy(h2_hbm.at[pl.ds(tok, 1), :], xbuf.at[sl, pl.ds(r, 1), :], sem.at[sl])

    def rows_wait(sl, n):
        pltpu.make_async_copy(h2_hbm.at[pl.ds(0, n), :], xbuf.at[sl, pl.ds(0, n), :], sem.at[sl]).wait()

    def ffn(x):
        a = _dot(x, w1b[...])
        b = _dot(x, w3b[...])
        act = (a * _sigmoid(a) * b).astype(_BF16)
        return _dot(act, w2b[...])

    @pl.when(i == 0)
    def _():
        def body(r, carry):
            row_copy(tok_ref[r], r, 0).start()
            return carry
        lax.fori_loop(0, jnp.where(half_ref[0] == 1, hb, bm), body, 0)

    for par in range(2):
        for n_rows, is_half in ((bm, 0), (hb, 1)):
            @pl.when((i + 1 < nb) & (slot == par) & (half_ref[jnp.minimum(i + 1, nb - 1)] == is_half))
            def _(par=par, n_rows=n_rows):
                base = (i + 1) * bm
                for r in range(n_rows):
                    row_copy(tok_ref[base + r], r, 1 - par).start()

    @pl.when(i < nb)
    def _():
        half = half_ref[i] == 1

        @pl.when(half)
        def _():
            rows_wait(slot, hb)

        @pl.when(jnp.logical_not(half))
        def _():
            rows_wait(slot, bm)

        changed = (i == 0) | (be_ref[i] != be_ref[jnp.maximum(i - 1, 0)])

        @pl.when(changed)
        def _():
            w1b[...] = w1_ref[0].astype(_BF16)
            w3b[...] = w3_ref[0].astype(_BF16)
            w2b[...] = w2_ref[0].astype(_BF16)

        @pl.when(half)
        def _():
            y_ref[:hb, :] = ffn(xbuf[slot, :hb, :].astype(_BF16))
            y_ref[hb:, :] = jnp.zeros((bm - hb, y_ref.shape[1]), y_ref.dtype)

        @pl.when(jnp.logical_not(half))
        def _():
            y_ref[...] = ffn(xbuf[slot].astype(_BF16))

    @pl.when(i >= nb)
    def _():
        y_ref[...] = jnp.zeros(y_ref.shape, y_ref.dtype)


def _moe(h2, tables, w1, w3, w2):
    t, d = h2.shape
    bm = MOE_ROWS
    block_e, block_half, row_tok, n_used = tables
    n_blocks = block_e.shape[0]
    ff = w1.shape[2]
    grid_spec = pltpu.PrefetchScalarGridSpec(
        num_scalar_prefetch=4,
        grid=(n_blocks,),
        in_specs=[pl.BlockSpec(memory_space=pl.ANY),
                  pl.BlockSpec((1, d, ff), lambda i, be, *_: (be[i], 0, 0)),
                  pl.BlockSpec((1, d, ff), lambda i, be, *_: (be[i], 0, 0)),
                  pl.BlockSpec((1, ff, d), lambda i, be, *_: (be[i], 0, 0))],
        out_specs=pl.BlockSpec((bm, d), lambda i, *_: (i, 0)),
        scratch_shapes=[pltpu.VMEM((2, bm, d), _F32),
                        pltpu.VMEM((d, ff), _BF16),
                        pltpu.VMEM((d, ff), _BF16),
                        pltpu.VMEM((ff, d), _BF16),
                        pltpu.SemaphoreType.DMA((2,))],
    )
    return pl.pallas_call(
        _moe_kernel,
        out_shape=jax.ShapeDtypeStruct((n_blocks * bm, d), _F32),
        grid_spec=grid_spec,
        compiler_params=_params("arbitrary", disable_bounds_checks=True),
        name="moe_experts",
    )(block_e, block_half, row_tok, n_used, h2, w1, w3, w2)


def _final_kernel(pos_ref, y_hbm, x1_ref, route_ref, g_ref, o_ref, ybuf, sem):
    i = pl.program_id(0)
    n = pl.num_programs(0)
    tm = x1_ref.shape[0]

    def row_copy(p, r, k, sl):
        return pltpu.make_async_copy(y_hbm.at[pl.ds(p, 1), :], ybuf.at[sl, k, pl.ds(r, 1), :], sem.at[sl])

    @pl.when(i == 0)
    def _():
        def body(r, carry):
            for k in range(TOP_K):
                row_copy(pos_ref[r * TOP_K + k], r, k, 0).start()
            return carry
        lax.fori_loop(0, tm, body, 0, unroll=DMA_UNROLL)

    def step(par, issue_next):
        if issue_next:
            base = (i + 1) * tm * TOP_K
            for r in range(tm):
                for k in range(TOP_K):
                    row_copy(pos_ref[base + r * TOP_K + k], r, k, 1 - par).start()
        for k in range(TOP_K):
            pltpu.make_async_copy(y_hbm.at[pl.ds(0, tm), :], ybuf.at[par, k], sem.at[par]).wait()
        route = route_ref[...]
        x = x1_ref[...] + route[:, 2:3] * ybuf[par, 0] + route[:, 3:4] * ybuf[par, 1]
        ms = jnp.mean(x * x, axis=-1, keepdims=True)
        o_ref[...] = x * lax.rsqrt(ms + NORM_EPS) * g_ref[...]

    for par in range(2):
        @pl.when((i % 2 == par) & (i + 1 < n))
        def _(par=par):
            step(par, True)

        @pl.when((i % 2 == par) & (i + 1 == n))
        def _(par=par):
            step(par, False)


def _final(pos, y_rows, x1, route, g):
    t, d = x1.shape
    tm = FINAL_TILE
    grid_spec = pltpu.PrefetchScalarGridSpec(
        num_scalar_prefetch=1,
        grid=(t // tm,),
        in_specs=[pl.BlockSpec(memory_space=pl.ANY),
                  pl.BlockSpec((tm, d), lambda i, pos: (i, 0)),
                  pl.BlockSpec((tm, LANES), lambda i, pos: (i, 0)),
                  pl.BlockSpec((1, d), lambda i, pos: (0, 0))],
        out_specs=pl.BlockSpec((tm, d), lambda i, pos: (i, 0)),
        scratch_shapes=[pltpu.VMEM((2, TOP_K, tm, d), _F32),
                        pltpu.SemaphoreType.DMA((2,))],
    )
    return pl.pallas_call(
        _final_kernel,
        out_shape=jax.ShapeDtypeStruct((t, d), _F32),
        grid_spec=grid_spec,
        compiler_params=_params("arbitrary", disable_bounds_checks=True),
        name="combine_norm",
    )(pos, y_rows, x1, route, g.reshape(1, d))


def _routing_tables(expert, n_tokens):
    bm = MOE_ROWS
    n_assign = n_tokens * TOP_K
    n_blocks = -(-(n_assign + N_EXPERTS * (bm - 1)) // bm)
    i32 = jnp.int32
    flat_e = expert.reshape(-1)
    experts = jnp.arange(N_EXPERTS, dtype=i32)
    assign = jnp.arange(n_assign, dtype=i32)
    e_sorted, order = lax.sort((flat_e, assign), num_keys=1)
    start = jnp.sum((e_sorted[:, None] < experts[None, :]).astype(i32), axis=0)
    counts = jnp.concatenate([start[1:], jnp.full((1,), n_assign, i32)]) - start
    padded = (counts + bm - 1) // bm * bm
    pend = jnp.cumsum(padded)
    pstart = pend - padded
    shift = pstart - start
    dshift = shift - jnp.concatenate([jnp.zeros((1,), i32), shift[:-1]])
    shift_sorted = jnp.sum(jnp.where(assign[:, None] >= start[None, :], dshift[None, :], 0), axis=1)
    _, pos = lax.sort((order, assign + shift_sorted), num_keys=1)
    block_row0 = jnp.arange(n_blocks, dtype=i32) * bm
    block_e = jnp.minimum(jnp.sum((pend[None, :] <= block_row0[:, None]).astype(i32), axis=1), N_EXPERTS - 1)
    onehot = block_e[:, None] == experts[None, :]

    def per_block(table):
        return jnp.sum(jnp.where(onehot, table[None, :], 0), axis=1)

    block_start = per_block(start - pstart) + block_row0
    block_end = per_block(start + counts)
    src = block_start[:, None] + jnp.arange(bm, dtype=i32)[None, :]
    valid = src < block_end[:, None]
    tok_sorted = order // TOP_K
    row_tok = jnp.where(valid, tok_sorted[jnp.clip(src, 0, n_assign - 1)], 0)
    block_half = (block_end - block_start <= bm // 2).astype(i32)
    n_used = (pend[-1] // bm).astype(i32).reshape(1)
    return (block_e, block_half, row_tok.reshape(-1).astype(i32), n_used), pos.astype(i32)


def kernel(x, attn_norm, w_in, conv_dw, conv_dw_b, conv_ln_g, conv_ln_b, cmp_pos_k, cmp_k_w1, cmp_k_w2, cmp_pos_v, cmp_v_w1, cmp_v_w2, w_proj_conv, w_proj_nsa, w_merge, b_merge, w_out, ffn_norm, w_grp, b_grp, w_exp, b_exp, exp_w1, exp_w3, exp_w2, final_norm):
    batch, seq, d = x.shape
    t = batch * seq
    assert w_in.shape[0] == 1, "the block is specified with a single layer"
    xf = x.reshape(t, d)

    w_in_b = w_in[0].astype(_BF16)
    w_gate = w_in[0][:, 2 * CONV_CHANNELS + HEAD_COLS:].reshape(d, 3, N_KV_GROUPS, GROUP_SIZE)
    w_gate = w_gate.transpose(0, 2, 1, 3).reshape(d, GATE_COLS).astype(_BF16)

    h = _rmsnorm(xf, attn_norm[0], _BF16)
    glu = _glu_proj(h, w_in_b)
    tables = _rope_tables(seq)
    heads = _heads_proj(h, w_in_b, tables, seq)
    cmp_flat = _cmp_proj(h, w_in_b, tables, seq)
    gates = _gate_proj(h, w_gate).reshape(t, N_KV_GROUPS, 3 * GROUP_SIZE).transpose(1, 0, 2)

    half = CMP_STRIDE * HEAD_DIM
    pos = jnp.stack([cmp_pos_k[0], cmp_pos_v[0]]).reshape(2, 2, 1, half)
    cw1 = jnp.stack([cmp_k_w1[0], cmp_v_w1[0]]).astype(_BF16)
    cw2 = jnp.stack([cmp_k_w2[0], cmp_v_w2[0]]).astype(_BF16)
    cmp_kv = _compress(cmp_flat, pos, cw1, cw2, batch, seq)

    c = _conv(glu, conv_dw[0], conv_dw_b[0], conv_ln_g[0], conv_ln_b[0], batch, seq)
    o = _nsa(heads, cmp_kv, gates, batch, seq)
    mixed = _mix(c, o, h, w_proj_conv[0].astype(_BF16), w_proj_nsa[0].astype(_BF16),
                 w_merge[0].astype(_BF16), b_merge[0])

    w_route = jnp.concatenate(
        [w_grp[0], w_exp[0], jnp.zeros((d, LANES - N_GROUPS - N_EXPERTS), _F32)], axis=1)
    b_route = jnp.concatenate(
        [b_grp[0], b_exp[0], jnp.zeros((LANES - N_GROUPS - N_EXPERTS,), _F32)]).reshape(1, LANES)
    x1, h2, route = _out_router(mixed, xf, w_out[0].astype(_BF16), ffn_norm[0], w_route, b_route)

    expert = route[:, :TOP_K].astype(jnp.int32)
    moe_tables, pos_rows = _routing_tables(expert, t)
    y_rows = _moe(h2, moe_tables, exp_w1[0], exp_w3[0], exp_w2[0])
    out = _final(pos_rows, y_rows, x1, route, final_norm)
    return out.reshape(batch, seq, d)
```

```python
import numpy as np
import jax
import jax.numpy as jnp
from jax import lax
from jax.experimental import pallas as pl
from jax.experimental.pallas import tpu as pltpu

D_MODEL = 2048
CONV_CHANNELS = D_MODEL // 2
DW_TAPS = 31
N_HEADS = 16
HEAD_DIM = 128
N_KV_GROUPS = 4
GROUP_SIZE = N_HEADS // N_KV_GROUPS
ROT_DIM = HEAD_DIM // 4
ROPE_THETA = 500000.0
CMP_BLOCK = 32
CMP_STRIDE = 16
CMP_HIDDEN = 256
SEL_BLOCK = 64
N_SELECT = 16
N_FORCED_LOCAL = 2
WINDOW = 512
N_GROUPS = 8
EXPERTS_PER_GROUP = 8
N_EXPERTS = N_GROUPS * EXPERTS_PER_GROUP
TOP_K = 2
EXPERT_FF = 512
Q_COLS = N_HEADS * HEAD_DIM
KV_COLS = N_KV_GROUPS * HEAD_DIM
HEAD_COLS = Q_COLS + 6 * KV_COLS
GATE_COLS = 3 * N_HEADS
NORM_EPS = 1e-6
NEG_INF = -1e30
FORCE_BONUS = 1e4
LOG2_E = 1.4426950408889634

LANES = 128
SUBLANES = 8
VMEM_LIMIT_BYTES = 56 * 1024 * 1024

ROW_TILE = 512
PROJ_TILE = 2048
MIX_TILE = 1024
MIX_COLS = 256
HEAD_TILE = GROUP_SIZE * HEAD_DIM
Q_TILE = 256
SEL_CHUNK = 512
CONV_TILE = 256
CONV_HALO = 32
CONV_ROWS = 16
OUT_TILE = 512
MOE_ROWS = 512
FINAL_TILE = 128
DMA_UNROLL = 8

_BF16 = jnp.bfloat16
_F32 = jnp.float32


def _params(*sem, **kw):
    return pltpu.CompilerParams(dimension_semantics=sem, vmem_limit_bytes=VMEM_LIMIT_BYTES, **kw)


def _dot(a, b):
    return jnp.dot(a, b, preferred_element_type=_F32)


def _dot_nt(a, b):
    return lax.dot_general(a, b, (((1,), (1,)), ((), ())), preferred_element_type=_F32)


def _sigmoid(x):
    return 1.0 / (1.0 + jnp.exp(-x))


def _rmsnorm_kernel(x_ref, g_ref, o_ref):
    x = x_ref[...]
    ms = jnp.mean(x * x, axis=-1, keepdims=True)
    o_ref[...] = (x * lax.rsqrt(ms + NORM_EPS) * g_ref[...]).astype(o_ref.dtype)


def _rmsnorm(x, g, out_dtype):
    t, d = x.shape
    return pl.pallas_call(
        _rmsnorm_kernel,
        out_shape=jax.ShapeDtypeStruct((t, d), out_dtype),
        grid=(t // ROW_TILE,),
        in_specs=[pl.BlockSpec((ROW_TILE, d), lambda i: (i, 0)),
                  pl.BlockSpec((1, d), lambda i: (0, 0))],
        out_specs=pl.BlockSpec((ROW_TILE, d), lambda i: (i, 0)),
        compiler_params=_params("parallel"),
        name="rmsnorm",
    )(x, g.reshape(1, d))


def _glu_kernel(h_ref, wa_ref, wb_ref, o_ref):
    h = h_ref[...]
    a = _dot(h, wa_ref[...])
    b = _dot(h, wb_ref[...])
    o_ref[...] = a * _sigmoid(b)


def _glu_proj(h, w_in_bf16):
    t, d = h.shape
    tn = HEAD_TILE
    nb = CONV_CHANNELS // tn
    return pl.pallas_call(
        _glu_kernel,
        out_shape=jax.ShapeDtypeStruct((t, CONV_CHANNELS), _F32),
        grid=(t // PROJ_TILE, nb),
        in_specs=[pl.BlockSpec((PROJ_TILE, d), lambda i, j: (i, 0)),
                  pl.BlockSpec((d, tn), lambda i, j: (0, j)),
                  pl.BlockSpec((d, tn), lambda i, j: (0, j + nb))],
        out_specs=pl.BlockSpec((PROJ_TILE, tn), lambda i, j: (i, j)),
        compiler_params=_params("parallel", "arbitrary"),
        name="glu_proj",
    )(h, w_in_bf16, w_in_bf16)


N_Q_SLABS = Q_COLS // HEAD_TILE
N_HEAD_SLABS = N_Q_SLABS + 4
Q_COL_BLOCK = 2 * CONV_CHANNELS // HEAD_TILE
CMP_COL_BLOCK = Q_COL_BLOCK + N_Q_SLABS
KV_COL_BLOCK = CMP_COL_BLOCK + 2


def _rope_tables(seq):
    pos = jnp.arange(seq, dtype=_F32)
    inv = ROPE_THETA ** (-jnp.arange(0, ROT_DIM, 2, dtype=_F32) / ROT_DIM)
    ang = pos[:, None] * inv[None, :]
    cos, sin = jnp.cos(ang), jnp.sin(ang)
    half = ROT_DIM // 2
    ones = jnp.ones((seq, HEAD_DIM - ROT_DIM), _F32)
    zeros_h = jnp.zeros((seq, half), _F32)
    zeros_r = jnp.zeros((seq, HEAD_DIM - ROT_DIM), _F32)
    c = jnp.concatenate([cos, cos, ones], axis=-1)
    s_lo = jnp.concatenate([zeros_h, sin, zeros_r], axis=-1)
    s_hi = jnp.concatenate([-sin, zeros_h, zeros_r], axis=-1)
    q_scale = HEAD_DIM ** -0.5 * LOG2_E
    zeros = jnp.zeros_like(c)
    return (jnp.stack([c * q_scale, c, jnp.ones_like(c)]),
            jnp.stack([s_lo * q_scale, s_lo, zeros]),
            jnp.stack([s_hi * q_scale, s_hi, zeros]))


ROPE_Q, ROPE_K, ROPE_NONE = 0, 1, 2


def _rotate_head(xs, c, slo, shi):
    half = ROT_DIM // 2
    return xs * c + pltpu.roll(xs, half, 1) * slo + pltpu.roll(xs, HEAD_DIM - half, 1) * shi


def _heads_kernel(h_ref, w_ref, c_ref, slo_ref, shi_ref, o_ref):
    acc = _dot(h_ref[...], w_ref[...])
    c, slo, shi = c_ref[0], slo_ref[0], shi_ref[0]
    for hd in range(GROUP_SIZE):
        r = _rotate_head(acc[:, hd * HEAD_DIM:(hd + 1) * HEAD_DIM], c, slo, shi)
        o_ref[0, hd] = r.astype(o_ref.dtype)


def _heads_proj(h, w_in_bf16, tables, seq):
    t, d = h.shape
    tm = PROJ_TILE
    spt = seq // tm

    def variant(j):
        return jnp.where(j < N_Q_SLABS, ROPE_Q, jnp.where(j % 2 == 0, ROPE_K, ROPE_NONE))

    tab = pl.BlockSpec((1, tm, HEAD_DIM), lambda i, j: (variant(j), i % spt, 0))

    def w_block(i, j):
        return (0, jnp.where(j < N_Q_SLABS, j + Q_COL_BLOCK, j - N_Q_SLABS + KV_COL_BLOCK))

    return pl.pallas_call(
        _heads_kernel,
        out_shape=jax.ShapeDtypeStruct((N_HEAD_SLABS, GROUP_SIZE, t, HEAD_DIM), _BF16),
        grid=(t // tm, N_HEAD_SLABS),
        in_specs=[pl.BlockSpec((tm, d), lambda i, j: (i, 0)),
                  pl.BlockSpec((d, HEAD_TILE), w_block),
                  tab, tab, tab],
        out_specs=pl.BlockSpec((1, GROUP_SIZE, tm, HEAD_DIM), lambda i, j: (j, 0, i, 0)),
        compiler_params=_params("parallel", "arbitrary"),
        name="heads_proj",
    )(h, w_in_bf16, *tables)


def _cmp_proj_kernel(h_ref, w_ref, c_ref, slo_ref, shi_ref, o_ref, stage_ref):
    acc = _dot(h_ref[...], w_ref[...])
    c, slo, shi = c_ref[0], slo_ref[0], shi_ref[0]
    for g in range(N_KV_GROUPS):
        stage_ref[g] = _rotate_head(acc[:, g * HEAD_DIM:(g + 1) * HEAD_DIM], c, slo, shi)

    n_chunk = stage_ref.shape[1] // CMP_STRIDE
    for g in range(N_KV_GROUPS):
        for l in range(CMP_STRIDE):
            rows = stage_ref[g, pl.ds(l, n_chunk, stride=CMP_STRIDE), :]
            o_ref[0, g, :, l * HEAD_DIM:(l + 1) * HEAD_DIM] = rows.astype(o_ref.dtype)


def _cmp_proj(h, w_in_bf16, tables, seq):
    t, d = h.shape
    tm = PROJ_TILE
    spt = seq // tm
    tab = pl.BlockSpec((1, tm, HEAD_DIM), lambda i, j: (jnp.where(j == 0, ROPE_K, ROPE_NONE), i % spt, 0))
    return pl.pallas_call(
        _cmp_proj_kernel,
        out_shape=jax.ShapeDtypeStruct((2, N_KV_GROUPS, t // CMP_STRIDE, CMP_STRIDE * HEAD_DIM), _BF16),
        grid=(t // tm, 2),
        in_specs=[pl.BlockSpec((tm, d), lambda i, j: (i, 0)),
                  pl.BlockSpec((d, HEAD_TILE), lambda i, j: (0, j + CMP_COL_BLOCK)),
                  tab, tab, tab],
        out_specs=pl.BlockSpec((1, N_KV_GROUPS, tm // CMP_STRIDE, CMP_STRIDE * HEAD_DIM),
                               lambda i, j: (j, 0, i, 0)),
        scratch_shapes=[pltpu.VMEM((N_KV_GROUPS, tm, HEAD_DIM), _F32)],
        compiler_params=_params("parallel", "arbitrary"),
        name="cmp_proj",
    )(h, w_in_bf16, *tables)


def _gate_kernel(h_ref, w_ref, o_ref):
    o_ref[...] = _sigmoid(_dot(h_ref[...], w_ref[...]))


def _gate_proj(h, w_gate_bf16):
    t, d = h.shape
    n = w_gate_bf16.shape[1]
    return pl.pallas_call(
        _gate_kernel,
        out_shape=jax.ShapeDtypeStruct((t, n), _F32),
        grid=(t // ROW_TILE,),
        in_specs=[pl.BlockSpec((ROW_TILE, d), lambda i: (i, 0)),
                  pl.BlockSpec((d, n), lambda i: (0, 0))],
        out_specs=pl.BlockSpec((ROW_TILE, n), lambda i: (i, 0)),
        compiler_params=_params("parallel"),
        name="gate_proj",
    )(h, w_gate_bf16)


def _compress_kernel(kv_ref, pos_ref, w1_ref, w2_ref, o_ref):
    half = CMP_STRIDE * HEAD_DIM
    c = kv_ref[0, 0].astype(_F32)
    n_chunk = c.shape[0]
    top = (c + pos_ref[0, 0]).astype(_BF16)
    bot = (c + pos_ref[0, 1]).astype(_BF16)
    u = _dot(top, w1_ref[0, :half, :])
    v = _dot(bot, w1_ref[0, half:, :])
    hidden = u + pltpu.roll(v, n_chunk - 1, 0)
    act = jax.nn.gelu(hidden).astype(_BF16)
    o_ref[0, 0, 0] = _dot(act, w2_ref[0]).astype(o_ref.dtype)


def _compress(cmp_flat, pos, w1, w2, batch, seq):
    n_chunk = seq // CMP_STRIDE
    return pl.pallas_call(
        _compress_kernel,
        out_shape=jax.ShapeDtypeStruct((2, batch, N_KV_GROUPS, n_chunk, HEAD_DIM), _BF16),
        grid=(2, batch, N_KV_GROUPS),
        in_specs=[pl.BlockSpec((1, 1, n_chunk, CMP_STRIDE * HEAD_DIM), lambda s, b, g: (s, g, b, 0)),
                  pl.BlockSpec((1, 2, 1, CMP_STRIDE * HEAD_DIM), lambda s, b, g: (s, 0, 0, 0)),
                  pl.BlockSpec((1, CMP_BLOCK * HEAD_DIM, CMP_HIDDEN), lambda s, b, g: (s, 0, 0)),
                  pl.BlockSpec((1, CMP_HIDDEN, HEAD_DIM), lambda s, b, g: (s, 0, 0))],
        out_specs=pl.BlockSpec((1, 1, 1, n_chunk, HEAD_DIM), lambda s, b, g: (s, b, g, 0, 0)),
        compiler_params=_params("parallel", "parallel", "parallel"),
        name="compress",
    )(cmp_flat, pos, w1, w2)


def _conv_kernel(cur_ref, halo_ref, dw_ref, dwb_ref, g_ref, b_ref, o_ref, ext_ref, sh_ref, acc_ref):
    i = pl.program_id(1)
    ts = cur_ref.shape[0]
    ext_ref[:CONV_HALO, :] = jnp.where(i > 0, halo_ref[...], 0.0)
    ext_ref[CONV_HALO:, :] = cur_ref[...]
    n_sh = sh_ref.shape[1]
    for b in range(1, SUBLANES):
        sh_ref[b - 1] = ext_ref[b:b + n_sh, :]
    off = CONV_HALO - (DW_TAPS - 1)

    def chunk(c, carry):
        r0 = pl.multiple_of(c * CONV_ROWS, CONV_ROWS)
        y = None
        for k in range(DW_TAPS):
            a, b = divmod(off + k, SUBLANES)
            src = ext_ref if b == 0 else sh_ref.at[b - 1]
            term = src[pl.ds(r0 + SUBLANES * a, CONV_ROWS), :] * dw_ref[k:k + 1, :]
            y = term if y is None else y + term
        acc_ref[pl.ds(r0, CONV_ROWS), :] = y
        return carry

    lax.fori_loop(0, ts // CONV_ROWS, chunk, 0)
    y = acc_ref[...] + dwb_ref[...]
    mu = jnp.mean(y, axis=-1, keepdims=True)
    yc = y - mu
    var = jnp.mean(yc * yc, axis=-1, keepdims=True)
    z = yc * lax.rsqrt(var + NORM_EPS) * g_ref[...] + b_ref[...]
    o_ref[...] = (z * _sigmoid(z)).astype(o_ref.dtype)


def _conv(glu, dw, dw_b, ln_g, ln_b, batch, seq):
    t, c = glu.shape
    ts = CONV_TILE
    nt = seq // ts
    hb = ts // CONV_HALO
    dw_pad = jnp.concatenate([dw, jnp.zeros((CONV_HALO - DW_TAPS, c), _F32)], axis=0)
    vec = pl.BlockSpec((1, c), lambda b, i: (0, 0))
    return pl.pallas_call(
        _conv_kernel,
        out_shape=jax.ShapeDtypeStruct((t, c), _BF16),
        grid=(batch, nt),
        in_specs=[pl.BlockSpec((ts, c), lambda b, i: (b * nt + i, 0)),
                  pl.BlockSpec((CONV_HALO, c), lambda b, i: (jnp.maximum((b * nt + i) * hb - 1, 0), 0)),
                  pl.BlockSpec((CONV_HALO, c), lambda b, i: (0, 0)),
                  vec, vec, vec],
        out_specs=pl.BlockSpec((ts, c), lambda b, i: (b * nt + i, 0)),
        scratch_shapes=[pltpu.VMEM((ts + CONV_HALO, c), _F32),
                        pltpu.VMEM((SUBLANES - 1, ts + CONV_HALO - SUBLANES, c), _F32),
                        pltpu.VMEM((ts, c), _F32)],
        compiler_params=_params("parallel", "arbitrary"),
        name="conv",
    )(glu, glu, dw_pad, dw_b.reshape(1, c), ln_g.reshape(1, c), ln_b.reshape(1, c))


def _importance_matrix(n_cmp_pad, n_blk):
    r_sel, r_cmp = SEL_BLOCK // CMP_STRIDE, CMP_BLOCK // CMP_STRIDE
    a = np.zeros((n_blk, n_cmp_pad), np.float32)
    for j in range(n_blk):
        for m in range(r_sel):
            for n in range(r_cmp):
                c = r_sel * j + m - n
                if 0 <= c < n_cmp_pad - 1:
                    a[j, c] += 1.0
    return a


def _expand_matrix(n_blk, seq):
    e = np.zeros((LANES, seq), np.float32)
    for j in range(n_blk):
        e[j, j * SEL_BLOCK:(j + 1) * SEL_BLOCK] = 1.0
    return e


def _nsa_step(n_active, q_ref, kc_ref, vc_ref, ks_ref, vs_ref, kw_ref, vw_ref, gate_ref, amat_ref, emat_ref,
              o_ref):
    tq_n = q_ref.shape[2]
    n_cmp_pad = kc_ref.shape[3]
    n_blk = amat_ref.shape[0]
    rows = GROUP_SIZE * tq_n
    start = pl.program_id(2) * tq_n
    q = q_ref[0].reshape(rows, HEAD_DIM)
    tq = start + lax.broadcasted_iota(jnp.int32, (tq_n, 1), 0)

    span = WINDOW + tq_n
    kstart = pl.multiple_of(jnp.maximum(start - WINDOW, 0), tq_n)
    kpos = kstart + lax.broadcasted_iota(jnp.int32, (1, span), 1)
    wbias = jnp.where((kpos <= tq) & (tq - kpos < WINDOW), 0.0, NEG_INF)
    sw = _dot_nt(q, kw_ref[0, 0, pl.ds(kstart, span), :]).reshape(GROUP_SIZE, tq_n, span) + wbias[None]

    cidx = lax.broadcasted_iota(jnp.int32, (1, n_cmp_pad), 1)
    cvalid = (cidx * CMP_STRIDE + (CMP_BLOCK - 1) <= tq) & (cidx < n_cmp_pad - 1)
    cbias = jnp.where(cvalid, 0.0, NEG_INF)
    s = _dot_nt(q, kc_ref[0, 0, 0]).reshape(GROUP_SIZE, tq_n, n_cmp_pad) + cbias[None]
    e_w = jnp.exp2(sw - jnp.max(sw, axis=-1, keepdims=True)).astype(_BF16)
    e = jnp.exp2(s - jnp.max(s, axis=-1, keepdims=True))
    inv_l = jnp.where((tq >= CMP_BLOCK - 1)[None], 1.0 / jnp.sum(e, axis=-1, keepdims=True), 0.0)
    p = e * inv_l
    vw_aug = jnp.concatenate([vw_ref[0, 0, pl.ds(kstart, span), :], jnp.ones((span, HEAD_DIM), _BF16)], axis=1)
    win = _dot(e_w.reshape(rows, span), vw_aug).reshape(GROUP_SIZE, tq_n, 2 * HEAD_DIM)
    o_cmp = _dot(p.reshape(rows, n_cmp_pad).astype(_BF16), vc_ref[0, 0, 0])

    imp = lax.dot_general(amat_ref[...], jnp.sum(p, axis=0), (((1,), (1,)), ((), ())),
                          preferred_element_type=_F32, precision=lax.Precision.HIGHEST)
    tq_row = start + lax.broadcasted_iota(jnp.int32, (1, tq_n), 1)
    blk = lax.broadcasted_iota(jnp.int32, (n_blk, 1), 0)
    cur = tq_row // SEL_BLOCK
    forced = (blk == 0) | ((blk <= cur) & (blk > cur - N_FORCED_LOCAL))
    imp = jnp.where(forced, imp + FORCE_BONUS, imp)
    imp = jnp.where(blk * SEL_BLOCK <= tq_row, imp, NEG_INF)
    rank = jnp.zeros((n_blk, tq_n), _F32)
    for i in range(n_blk):
        row = imp[i:i + 1, :]
        before = (row > imp) | ((row == imp) & (blk > i))
        rank = rank + jnp.where(before, 1.0, 0.0)
    sel_t = jnp.where(rank < float(min(N_SELECT, n_blk)), 1.0, 0.0)
    sel = jnp.concatenate([sel_t, jnp.zeros((LANES - n_blk, tq_n), _F32)], axis=0).T.astype(_BF16)

    ones_c = jnp.ones((SEL_CHUNK, HEAD_DIM), _BF16)

    def scores(c):
        return _dot_nt(q, ks_ref[0, 0, c * SEL_CHUNK:(c + 1) * SEL_CHUNK, :]).reshape(GROUP_SIZE, tq_n, SEL_CHUNK)

    m_run = None
    acc = None
    sc_next = scores(0)
    for c in range(n_active):
        lo = c * SEL_CHUNK
        sc_raw = sc_next
        if c + 1 < n_active:
            sc_next = scores(c + 1)
        chosen = _dot(sel, emat_ref[:, lo:lo + SEL_CHUNK]) > 0.5
        if c == n_active - 1:
            kpos = lo + lax.broadcasted_iota(jnp.int32, (1, SEL_CHUNK), 1)
            chosen = chosen & (kpos <= tq)
        sc = sc_raw + jnp.where(chosen, 0.0, NEG_INF)[None]
        m_c = jnp.max(sc, axis=-1, keepdims=True)
        m_new = m_c if c == 0 else jnp.maximum(m_run, m_c)
        e_c = jnp.exp2(sc - m_new).astype(_BF16)
        v_aug = jnp.concatenate([vs_ref[0, 0, lo:lo + SEL_CHUNK, :], ones_c], axis=1)
        pv = _dot(e_c.reshape(rows, SEL_CHUNK), v_aug).reshape(GROUP_SIZE, tq_n, 2 * HEAD_DIM)
        acc = pv if c == 0 else jnp.exp2(m_run - m_new) * acc + pv
        m_run = m_new

    o_cmp = o_cmp.reshape(GROUP_SIZE, tq_n, HEAD_DIM)
    gates = gate_ref[0]
    for r in range(GROUP_SIZE):
        g_cmp = gates[:, r:r + 1]
        g_slc = gates[:, GROUP_SIZE + r:GROUP_SIZE + r + 1] / acc[r, :, HEAD_DIM:]
        g_win = gates[:, 2 * GROUP_SIZE + r:2 * GROUP_SIZE + r + 1] / win[r, :, HEAD_DIM:]
        o = g_cmp * o_cmp[r] + g_slc * acc[r, :, :HEAD_DIM] + g_win * win[r, :, :HEAD_DIM]
        o_ref[:, r * HEAD_DIM:(r + 1) * HEAD_DIM] = o.astype(o_ref.dtype)


def _nsa_kernel(q_ref, kc_ref, vc_ref, ks_ref, vs_ref, kw_ref, vw_ref, gate_ref, amat_ref, emat_ref, o_ref):
    tq_n = q_ref.shape[2]
    seq = ks_ref.shape[2]
    n_active = (pl.program_id(2) * tq_n + tq_n - 1) // SEL_CHUNK + 1
    for k in range(1, seq // SEL_CHUNK + 1):
        @pl.when(n_active == k)
        def _(k=k):
            _nsa_step(k, q_ref, kc_ref, vc_ref, ks_ref, vs_ref, kw_ref, vw_ref, gate_ref, amat_ref, emat_ref,
                      o_ref)


def _nsa(heads, cmp_kv, gates, batch, seq):
    t = heads.shape[2]
    tq = Q_TILE
    nq = seq // tq
    n_chunk = cmp_kv.shape[3]
    n_blk = seq // SEL_BLOCK
    assert n_chunk == LANES and n_blk <= LANES and seq % SEL_CHUNK == 0 and WINDOW % tq == 0
    amat = jnp.asarray(_importance_matrix(n_chunk, n_blk))
    emat = jnp.asarray(_expand_matrix(n_blk, seq), dtype=_BF16)

    def kv_spec(slab):
        return pl.BlockSpec((1, 1, seq, HEAD_DIM), lambda b, g, i: (slab, g, b, 0))

    def cmp_spec(s):
        return pl.BlockSpec((1, 1, 1, n_chunk, HEAD_DIM), lambda b, g, i: (s, b, g, 0, 0))

    return pl.pallas_call(
        _nsa_kernel,
        out_shape=jax.ShapeDtypeStruct((t, Q_COLS), _BF16),
        grid=(batch, N_KV_GROUPS, nq),
        in_specs=[pl.BlockSpec((1, GROUP_SIZE, tq, HEAD_DIM), lambda b, g, i: (g, 0, b * nq + i, 0)),
                  cmp_spec(0), cmp_spec(1),
                  kv_spec(N_Q_SLABS), kv_spec(N_Q_SLABS + 1),
                  kv_spec(N_Q_SLABS + 2), kv_spec(N_Q_SLABS + 3),
                  pl.BlockSpec((1, tq, 3 * GROUP_SIZE), lambda b, g, i: (g, b * nq + i, 0)),
                  pl.BlockSpec((n_blk, n_chunk), lambda b, g, i: (0, 0)),
                  pl.BlockSpec((LANES, seq), lambda b, g, i: (0, 0))],
        out_specs=pl.BlockSpec((tq, HEAD_TILE), lambda b, g, i: (b * nq + i, g)),
        compiler_params=_params("parallel", "parallel", "arbitrary"),
        name="nsa",
    )(heads, cmp_kv, cmp_kv, heads, heads, heads, heads, gates, amat, emat)


def _mix_kernel(c_ref, o_ref, h_ref, wpc_ref, wpn_ref, wmc_ref, wmn_ref, bc_ref, bn_ref, out_ref):
    h = h_ref[...]
    g_conv = _sigmoid(_dot(h, wmc_ref[...]) + bc_ref[...])
    g_nsa = _sigmoid(_dot(h, wmn_ref[...]) + bn_ref[...])
    y_conv = _dot(c_ref[...], wpc_ref[...])
    y_nsa = _dot(o_ref[...], wpn_ref[...])
    out_ref[...] = (g_conv * y_conv + g_nsa * y_nsa).astype(out_ref.dtype)


def _mix(c, o, h, wpc, wpn, wm, bm):
    t, d = h.shape
    tm, tn = MIX_TILE, MIX_COLS
    nb = d // tn
    bm2 = bm.reshape(1, 2 * d)
    return pl.pallas_call(
        _mix_kernel,
        out_shape=jax.ShapeDtypeStruct((t, d), _BF16),
        grid=(t // tm, nb),
        in_specs=[pl.BlockSpec((tm, CONV_CHANNELS), lambda i, j: (i, 0)),
                  pl.BlockSpec((tm, Q_COLS), lambda i, j: (i, 0)),
                  pl.BlockSpec((tm, d), lambda i, j: (i, 0)),
                  pl.BlockSpec((CONV_CHANNELS, tn), lambda i, j: (0, j)),
                  pl.BlockSpec((Q_COLS, tn), lambda i, j: (0, j)),
                  pl.BlockSpec((d, tn), lambda i, j: (0, j)),
                  pl.BlockSpec((d, tn), lambda i, j: (0, j + nb)),
                  pl.BlockSpec((1, tn), lambda i, j: (0, j)),
                  pl.BlockSpec((1, tn), lambda i, j: (0, j + nb))],
        out_specs=pl.BlockSpec((tm, tn), lambda i, j: (i, j)),
        compiler_params=_params("parallel", "arbitrary"),
        name="mix",
    )(c, o, h, wpc, wpn, wm, wm, bm2, bm2)


def _out_router_kernel(mix_ref, x_ref, wout_ref, g_ref, wrh_ref, wrl_ref, br_ref, x1_ref, h2_ref, route_ref):
    x1 = x_ref[...] + _dot(mix_ref[...], wout_ref[...])
    x1_ref[...] = x1
    ms = jnp.mean(x1 * x1, axis=-1, keepdims=True)
    h2 = x1 * lax.rsqrt(ms + NORM_EPS) * g_ref[...]
    h2_ref[...] = h2
    h_hi = h2.astype(_BF16)
    h_lo = (h2 - h_hi.astype(_F32)).astype(_BF16)
    logits = (_dot(h_hi, wrh_ref[...]) + _dot(h_lo, wrh_ref[...]) + _dot(h_hi, wrl_ref[...])
              + br_ref[...])
    lane = lax.broadcasted_iota(jnp.int32, (1, LANES), 1)
    lane_f = lane.astype(_F32)
    big = float(LANES)
    gl = jnp.where(lane < N_GROUPS, logits, NEG_INF)
    gmax = jnp.max(gl, axis=-1, keepdims=True)
    p_g = 1.0 / jnp.sum(jnp.where(lane < N_GROUPS, jnp.exp(gl - gmax), 0.0), axis=-1, keepdims=True)
    g_sel = jnp.min(jnp.where(gl == gmax, lane_f, big), axis=-1, keepdims=True)
    lo = N_GROUPS + g_sel * EXPERTS_PER_GROUP
    in_grp = (lane_f >= lo) & (lane_f < lo + EXPERTS_PER_GROUP)
    el = jnp.where(in_grp, logits, NEG_INF)
    v0 = jnp.max(el, axis=-1, keepdims=True)
    i0 = jnp.min(jnp.where(el == v0, lane_f, big), axis=-1, keepdims=True)
    el2 = jnp.where(lane_f == i0, NEG_INF, el)
    v1 = jnp.max(el2, axis=-1, keepdims=True)
    i1 = jnp.min(jnp.where(el2 == v1, lane_f, big), axis=-1, keepdims=True)
    t1 = jnp.exp(v1 - v0)
    w0 = p_g / (1.0 + t1)
    w1 = p_g * t1 / (1.0 + t1)
    route = jnp.where(lane == 0, i0 - N_GROUPS, 0.0)
    route = jnp.where(lane == 1, i1 - N_GROUPS, route)
    route = jnp.where(lane == 2, w0, route)
    route = jnp.where(lane == 3, w1, route)
    route_ref[...] = route


def _out_router(mixed, x, w_out, ffn_g, w_route, b_route):
    t, d = x.shape
    tm = OUT_TILE
    row = pl.BlockSpec((tm, d), lambda i: (i, 0))
    w_route_hi = w_route.astype(_BF16)
    w_route_lo = (w_route - w_route_hi.astype(_F32)).astype(_BF16)
    return pl.pallas_call(
        _out_router_kernel,
        out_shape=(jax.ShapeDtypeStruct((t, d), _F32),
                   jax.ShapeDtypeStruct((t, d), _F32),
                   jax.ShapeDtypeStruct((t, LANES), _F32)),
        grid=(t // tm,),
        in_specs=[row, row,
                  pl.BlockSpec((d, d), lambda i: (0, 0), pipeline_mode=pl.Buffered(1)),
                  pl.BlockSpec((1, d), lambda i: (0, 0)),
                  pl.BlockSpec((d, LANES), lambda i: (0, 0)),
                  pl.BlockSpec((d, LANES), lambda i: (0, 0)),
                  pl.BlockSpec((1, LANES), lambda i: (0, 0))],
        out_specs=(row, row, pl.BlockSpec((tm, LANES), lambda i: (i, 0))),
        compiler_params=_params("parallel"),
        name="out_router",
    )(mixed, x, w_out, ffn_g.reshape(1, d), w_route_hi, w_route_lo, b_route)


def _moe_kernel(be_ref, half_ref, tok_ref, nb_ref, h2_hbm, w1_ref, w3_ref, w2_ref, y_ref, xbuf, w1b, w3b, w2b, sem):
    i = pl.program_id(0)
    nb = nb_ref[0]
    slot = i % 2
    bm = xbuf.shape[1]
    hb = bm // 2

    def row_copy(tok, r, sl):
        return pltpu.make_async_copy(h2_hbm.at[pl.ds(tok, 1), :], xbuf.at[sl, pl.ds(r, 1), :], sem.at[sl])

    def rows_wait(sl, n):
        pltpu.make_async_copy(h2_hbm.at[pl.ds(0, n), :], xbuf.at[sl, pl.ds(0, n), :], sem.at[sl]).wait()

    def ffn(x):
        a = _dot(x, w1b[...])
        b = _dot(x, w3b[...])
        act = (a * _sigmoid(a) * b).astype(_BF16)
        return _dot(act, w2b[...])

    @pl.when(i == 0)
    def _():
        def body(r, carry):
            row_copy(tok_ref[r], r, 0).start()
            return carry
        lax.fori_loop(0, jnp.where(half_ref[0] == 1, hb, bm), body, 0)

    for par in range(2):
        for n_rows, is_half in ((bm, 0), (hb, 1)):
            @pl.when((i + 1 < nb) & (slot == par) & (half_ref[jnp.minimum(i + 1, nb - 1)] == is_half))
            def _(par=par, n_rows=n_rows):
                base = (i + 1) * bm
                for r in range(n_rows):
                    row_copy(tok_ref[base + r], r, 1 - par).start()

    @pl.when(i < nb)
    def _():
        half = half_ref[i] == 1

        @pl.when(half)
        def _():
            rows_wait(slot, hb)

        @pl.when(jnp.logical_not(half))
        def _():
            rows_wait(slot, bm)

        changed = (i == 0) | (be_ref[i] != be_ref[jnp.maximum(i - 1, 0)])

        @pl.when(changed)
        def _():
            w1b[...] = w1_ref[0].astype(_BF16)
            w3b[...] = w3_ref[0].astype(_BF16)
            w2b[...] = w2_ref[0].astype(_BF16)

        @pl.when(half)
        def _():
            y_ref[:hb, :] = ffn(xbuf[slot, :hb, :].astype(_BF16))
            y_ref[hb:, :] = jnp.zeros((bm - hb, y_ref.shape[1]), y_ref.dtype)

        @pl.when(jnp.logical_not(half))
        def _():
            y_ref[...] = ffn(xbuf[slot].astype(_BF16))

    @pl.when(i >= nb)
    def _():
        y_ref[...] = jnp.zeros(y_ref.shape, y_ref.dtype)


def _moe(h2, tables, w1, w3, w2):
    t, d = h2.shape
    bm = MOE_ROWS
    block_e, block_half, row_tok, n_used = tables
    n_blocks = block_e.shape[0]
    ff = w1.shape[2]
    grid_spec = pltpu.PrefetchScalarGridSpec(
        num_scalar_prefetch=4,
        grid=(n_blocks,),
        in_specs=[pl.BlockSpec(memory_space=pl.ANY),
                  pl.BlockSpec((1, d, ff), lambda i, be, *_: (be[i], 0, 0)),
                  pl.BlockSpec((1, d, ff), lambda i, be, *_: (be[i], 0, 0)),
                  pl.BlockSpec((1, ff, d), lambda i, be, *_: (be[i], 0, 0))],
        out_specs=pl.BlockSpec((bm, d), lambda i, *_: (i, 0)),
        scratch_shapes=[pltpu.VMEM((2, bm, d), _F32),
                        pltpu.VMEM((d, ff), _BF16),
                        pltpu.VMEM((d, ff), _BF16),
                        pltpu.VMEM((ff, d), _BF16),
                        pltpu.SemaphoreType.DMA((2,))],
    )
    return pl.pallas_call(
        _moe_kernel,
        out_shape=jax.ShapeDtypeStruct((n_blocks * bm, d), _F32),
        grid_spec=grid_spec,
        compiler_params=_params("arbitrary", disable_bounds_checks=True),
        name="moe_experts",
    )(block_e, block_half, row_tok, n_used, h2, w1, w3, w2)


def _final_kernel(pos_ref, y_hbm, x1_ref, route_ref, g_ref, o_ref, ybuf, sem):
    i = pl.program_id(0)
    n = pl.num_programs(0)
    tm = x1_ref.shape[0]

    def row_copy(p, r, k, sl):
        return pltpu.make_async_copy(y_hbm.at[pl.ds(p, 1), :], ybuf.at[sl, k, pl.ds(r, 1), :], sem.at[sl])

    @pl.when(i == 0)
    def _():
        def body(r, carry):
            for k in range(TOP_K):
                row_copy(pos_ref[r * TOP_K + k], r, k, 0).start()
            return carry
        lax.fori_loop(0, tm, body, 0, unroll=DMA_UNROLL)

    def step(par, issue_next):
        if issue_next:
            base = (i + 1) * tm * TOP_K
            for r in range(tm):
                for k in range(TOP_K):
                    row_copy(pos_ref[base + r * TOP_K + k], r, k, 1 - par).start()
        for k in range(TOP_K):
            pltpu.make_async_copy(y_hbm.at[pl.ds(0, tm), :], ybuf.at[par, k], sem.at[par]).wait()
        route = route_ref[...]
        x = x1_ref[...] + route[:, 2:3] * ybuf[par, 0] + route[:, 3:4] * ybuf[par, 1]
        ms = jnp.mean(x * x, axis=-1, keepdims=True)
        o_ref[...] = x * lax.rsqrt(ms + NORM_EPS) * g_ref[...]

    for par in range(2):
        @pl.when((i % 2 == par) & (i + 1 < n))
        def _(par=par):
            step(par, True)

        @pl.when((i % 2 == par) & (i + 1 == n))
        def _(par=par):
            step(par, False)


def _final(pos, y_rows, x1, route, g):
    t, d = x1.shape
    tm = FINAL_TILE
    grid_spec = pltpu.PrefetchScalarGridSpec(
        num_scalar_prefetch=1,
        grid=(t // tm,),
        in_specs=[pl.BlockSpec(memory_space=pl.ANY),
                  pl.BlockSpec((tm, d), lambda i, pos: (i, 0)),
                  pl.BlockSpec((tm, LANES), lambda i, pos: (i, 0)),
                  pl.BlockSpec((1, d), lambda i, pos: (0, 0))],
        out_specs=pl.BlockSpec((tm, d), lambda i, pos: (i, 0)),
        scratch_shapes=[pltpu.VMEM((2, TOP_K, tm, d), _F32),
                        pltpu.SemaphoreType.DMA((2,))],
    )
    return pl.pallas_call(
        _final_kernel,
        out_shape=jax.ShapeDtypeStruct((t, d), _F32),
        grid_spec=grid_spec,
        compiler_params=_params("arbitrary", disable_bounds_checks=True),
        name="combine_norm",
    )(pos, y_rows, x1, route, g.reshape(1, d))


def _routing_tables(expert, n_tokens):
    bm = MOE_ROWS
    n_assign = n_tokens * TOP_K
    n_blocks = -(-(n_assign + N_EXPERTS * (bm - 1)) // bm)
    i32 = jnp.int32
    flat_e = expert.reshape(-1)
    experts = jnp.arange(N_EXPERTS, dtype=i32)
    assign = jnp.arange(n_assign, dtype=i32)
    e_sorted, order = lax.sort((flat_e, assign), num_keys=1)
    start = jnp.sum((e_sorted[:, None] < experts[None, :]).astype(i32), axis=0)
    counts = jnp.concatenate([start[1:], jnp.full((1,), n_assign, i32)]) - start
    padded = (counts + bm - 1) // bm * bm
    pend = jnp.cumsum(padded)
    pstart = pend - padded
    shift = pstart - start
    dshift = shift - jnp.concatenate([jnp.zeros((1,), i32), shift[:-1]])
    shift_sorted = jnp.sum(jnp.where(assign[:, None] >= start[None, :], dshift[None, :], 0), axis=1)
    _, pos = lax.sort((order, assign + shift_sorted), num_keys=1)
    block_row0 = jnp.arange(n_blocks, dtype=i32) * bm
    block_e = jnp.minimum(jnp.sum((pend[None, :] <= block_row0[:, None]).astype(i32), axis=1), N_EXPERTS - 1)
    onehot = block_e[:, None] == experts[None, :]

    def per_block(table):
        return jnp.sum(jnp.where(onehot, table[None, :], 0), axis=1)

    block_start = per_block(start - pstart) + block_row0
    block_end = per_block(start + counts)
    src = block_start[:, None] + jnp.arange(bm, dtype=i32)[None, :]
    valid = src < block_end[:, None]
    tok_sorted = order // TOP_K
    idx = jnp.clip(src, 0, n_assign - 1)
    hb = bm // 2
    looked_up = jnp.concatenate([tok_sorted[idx[:, :hb]], tok_sorted[idx[:, hb:]]], axis=1)
    row_tok = jnp.where(valid, looked_up, 0)
    block_half = (block_end - block_start <= bm // 2).astype(i32)
    n_used = (pend[-1] // bm).astype(i32).reshape(1)
    return (block_e, block_half, row_tok.reshape(-1).astype(i32), n_used), pos.astype(i32)


def kernel(x, attn_norm, w_in, conv_dw, conv_dw_b, conv_ln_g, conv_ln_b, cmp_pos_k, cmp_k_w1, cmp_k_w2, cmp_pos_v, cmp_v_w1, cmp_v_w2, w_proj_conv, w_proj_nsa, w_merge, b_merge, w_out, ffn_norm, w_grp, b_grp, w_exp, b_exp, exp_w1, exp_w3, exp_w2, final_norm):
    batch, seq, d = x.shape
    t = batch * seq
    assert w_in.shape[0] == 1, "the block is specified with a single layer"
    xf = x.reshape(t, d)

    w_in_b = w_in[0].astype(_BF16)
    w_gate = w_in[0][:, 2 * CONV_CHANNELS + HEAD_COLS:].reshape(d, 3, N_KV_GROUPS, GROUP_SIZE)
    w_gate = w_gate.transpose(0, 2, 1, 3).reshape(d, GATE_COLS).astype(_BF16)

    h = _rmsnorm(xf, attn_norm[0], _BF16)
    glu = _glu_proj(h, w_in_b)
    tables = _rope_tables(seq)
    heads = _heads_proj(h, w_in_b, tables, seq)
    cmp_flat = _cmp_proj(h, w_in_b, tables, seq)
    gates = _gate_proj(h, w_gate).reshape(t, N_KV_GROUPS, 3 * GROUP_SIZE).transpose(1, 0, 2)

    half = CMP_STRIDE * HEAD_DIM
    pos = jnp.stack([cmp_pos_k[0], cmp_pos_v[0]]).reshape(2, 2, 1, half)
    cw1 = jnp.stack([cmp_k_w1[0], cmp_v_w1[0]]).astype(_BF16)
    cw2 = jnp.stack([cmp_k_w2[0], cmp_v_w2[0]]).astype(_BF16)
    cmp_kv = _compress(cmp_flat, pos, cw1, cw2, batch, seq)

    c = _conv(glu, conv_dw[0], conv_dw_b[0], conv_ln_g[0], conv_ln_b[0], batch, seq)
    o = _nsa(heads, cmp_kv, gates, batch, seq)
    mixed = _mix(c, o, h, w_proj_conv[0].astype(_BF16), w_proj_nsa[0].astype(_BF16),
                 w_merge[0].astype(_BF16), b_merge[0])

    w_route = jnp.concatenate(
        [w_grp[0], w_exp[0], jnp.zeros((d, LANES - N_GROUPS - N_EXPERTS), _F32)], axis=1)
    b_route = jnp.concatenate(
        [b_grp[0], b_exp[0], jnp.zeros((LANES - N_GROUPS - N_EXPERTS,), _F32)]).reshape(1, LANES)
    x1, h2, route = _out_router(mixed, xf, w_out[0].astype(_BF16), ffn_norm[0], w_route, b_route)

    expert = route[:, :TOP_K].astype(jnp.int32)
    moe_tables, pos_rows = _routing_tables(expert, t)
    y_rows = _moe(h2, moe_tables, exp_w1[0], exp_w3[0], exp_w2[0])
    out = _final(pos_rows, y_rows, x1, route, final_norm)
    return out.reshape(batch, seq, d)
```

```python
import numpy as np
import jax
import jax.numpy as jnp
from jax import lax
from jax.experimental import pallas as pl
from jax.experimental.pallas import tpu as pltpu

D_MODEL = 2048
CONV_CHANNELS = D_MODEL // 2
DW_TAPS = 31
N_HEADS = 16
HEAD_DIM = 128
N_KV_GROUPS = 4
GROUP_SIZE = N_HEADS // N_KV_GROUPS
ROT_DIM = HEAD_DIM // 4
ROPE_THETA = 500000.0
CMP_BLOCK = 32
CMP_STRIDE = 16
CMP_HIDDEN = 256
SEL_BLOCK = 64
N_SELECT = 16
N_FORCED_LOCAL = 2
WINDOW = 512
N_GROUPS = 8
EXPERTS_PER_GROUP = 8
N_EXPERTS = N_GROUPS * EXPERTS_PER_GROUP
TOP_K = 2
EXPERT_FF = 512
Q_COLS = N_HEADS * HEAD_DIM
KV_COLS = N_KV_GROUPS * HEAD_DIM
HEAD_COLS = Q_COLS + 6 * KV_COLS
GATE_COLS = 3 * N_HEADS
NORM_EPS = 1e-6
NEG_INF = -1e30
FORCE_BONUS = 1e4
LOG2_E = 1.4426950408889634

LANES = 128
SUBLANES = 8
VMEM_LIMIT_BYTES = 56 * 1024 * 1024

ROW_TILE = 512
PROJ_TILE = 2048
MIX_TILE = 1024
MIX_COLS = 256
HEAD_TILE = GROUP_SIZE * HEAD_DIM
Q_TILE = 256
SEL_CHUNK = 512
CONV_TILE = 256
CONV_HALO = 32
CONV_ROWS = 16
OUT_TILE = 512
MOE_ROWS = 512
FINAL_TILE = 128
DMA_UNROLL = 8

_BF16 = jnp.bfloat16
_F32 = jnp.float32


def _params(*sem, **kw):
    return pltpu.CompilerParams(dimension_semantics=sem, vmem_limit_bytes=VMEM_LIMIT_BYTES, **kw)


def _dot(a, b):
    return jnp.dot(a, b, preferred_element_type=_F32)


def _dot_nt(a, b):
    return lax.dot_general(a, b, (((1,), (1,)), ((), ())), preferred_element_type=_F32)


def _sigmoid(x):
    return 1.0 / (1.0 + jnp.exp(-x))


def _rmsnorm_kernel(x_ref, g_ref, o_ref):
    x = x_ref[...]
    ms = jnp.mean(x * x, axis=-1, keepdims=True)
    o_ref[...] = (x * lax.rsqrt(ms + NORM_EPS) * g_ref[...]).astype(o_ref.dtype)


def _rmsnorm(x, g, out_dtype):
    t, d = x.shape
    return pl.pallas_call(
        _rmsnorm_kernel,
        out_shape=jax.ShapeDtypeStruct((t, d), out_dtype),
        grid=(t // ROW_TILE,),
        in_specs=[pl.BlockSpec((ROW_TILE, d), lambda i: (i, 0)),
                  pl.BlockSpec((1, d), lambda i: (0, 0))],
        out_specs=pl.BlockSpec((ROW_TILE, d), lambda i: (i, 0)),
        compiler_params=_params("parallel"),
        name="rmsnorm",
    )(x, g.reshape(1, d))


def _glu_kernel(h_ref, wa_ref, wb_ref, o_ref):
    h = h_ref[...]
    a = _dot(h, wa_ref[...])
    b = _dot(h, wb_ref[...])
    o_ref[...] = a * _sigmoid(b)


def _glu_proj(h, w_in_bf16):
    t, d = h.shape
    tn = HEAD_TILE
    nb = CONV_CHANNELS // tn
    return pl.pallas_call(
        _glu_kernel,
        out_shape=jax.ShapeDtypeStruct((t, CONV_CHANNELS), _F32),
        grid=(t // PROJ_TILE, nb),
        in_specs=[pl.BlockSpec((PROJ_TILE, d), lambda i, j: (i, 0)),
                  pl.BlockSpec((d, tn), lambda i, j: (0, j)),
                  pl.BlockSpec((d, tn), lambda i, j: (0, j + nb))],
        out_specs=pl.BlockSpec((PROJ_TILE, tn), lambda i, j: (i, j)),
        compiler_params=_params("parallel", "arbitrary"),
        name="glu_proj",
    )(h, w_in_bf16, w_in_bf16)


N_Q_SLABS = Q_COLS // HEAD_TILE
N_HEAD_SLABS = N_Q_SLABS + 4
Q_COL_BLOCK = 2 * CONV_CHANNELS // HEAD_TILE
CMP_COL_BLOCK = Q_COL_BLOCK + N_Q_SLABS
KV_COL_BLOCK = CMP_COL_BLOCK + 2


def _rope_tables(seq):
    pos = jnp.arange(seq, dtype=_F32)
    inv = ROPE_THETA ** (-jnp.arange(0, ROT_DIM, 2, dtype=_F32) / ROT_DIM)
    ang = pos[:, None] * inv[None, :]
    cos, sin = jnp.cos(ang), jnp.sin(ang)
    half = ROT_DIM // 2
    ones = jnp.ones((seq, HEAD_DIM - ROT_DIM), _F32)
    zeros_h = jnp.zeros((seq, half), _F32)
    zeros_r = jnp.zeros((seq, HEAD_DIM - ROT_DIM), _F32)
    c = jnp.concatenate([cos, cos, ones], axis=-1)
    s_lo = jnp.concatenate([zeros_h, sin, zeros_r], axis=-1)
    s_hi = jnp.concatenate([-sin, zeros_h, zeros_r], axis=-1)
    q_scale = HEAD_DIM ** -0.5 * LOG2_E
    zeros = jnp.zeros_like(c)
    return (jnp.stack([c * q_scale, c, jnp.ones_like(c)]),
            jnp.stack([s_lo * q_scale, s_lo, zeros]),
            jnp.stack([s_hi * q_scale, s_hi, zeros]))


ROPE_Q, ROPE_K, ROPE_NONE = 0, 1, 2


def _rotate_head(xs, c, slo, shi):
    half = ROT_DIM // 2
    return xs * c + pltpu.roll(xs, half, 1) * slo + pltpu.roll(xs, HEAD_DIM - half, 1) * shi


def _heads_kernel(h_ref, w_ref, c_ref, slo_ref, shi_ref, o_ref, acc_ref):
    s = pl.program_id(0)

    @pl.when(s == 0)
    def _():
        acc_ref[...] = jnp.zeros(acc_ref.shape, acc_ref.dtype)

    for par in range(2):
        @pl.when(s % 2 == par)
        def _(par=par):
            acc_ref[par] = _dot(h_ref[...], w_ref[...])
            c, slo, shi = c_ref[0], slo_ref[0], shi_ref[0]
            for hd in range(GROUP_SIZE):
                r = _rotate_head(acc_ref[1 - par, :, hd * HEAD_DIM:(hd + 1) * HEAD_DIM], c, slo, shi)
                o_ref[0, hd] = r.astype(o_ref.dtype)


def _heads_proj(h, w_in_bf16, tables, seq):
    t, d = h.shape
    tm = PROJ_TILE
    spt = seq // tm
    n_slabs = N_HEAD_SLABS
    last = (t // tm) * n_slabs - 1

    def cur(s):
        s = jnp.minimum(s, last)
        return s // n_slabs, s % n_slabs

    def prev(s):
        s = jnp.maximum(s - 1, 0)
        return s // n_slabs, s % n_slabs

    def variant(j):
        return jnp.where(j < N_Q_SLABS, ROPE_Q, jnp.where(j % 2 == 0, ROPE_K, ROPE_NONE))

    def w_block(s):
        j = cur(s)[1]
        return (0, jnp.where(j < N_Q_SLABS, j + Q_COL_BLOCK, j - N_Q_SLABS + KV_COL_BLOCK))

    tab = pl.BlockSpec((1, tm, HEAD_DIM), lambda s: (variant(prev(s)[1]), prev(s)[0] % spt, 0))
    return pl.pallas_call(
        _heads_kernel,
        out_shape=jax.ShapeDtypeStruct((n_slabs, GROUP_SIZE, t, HEAD_DIM), _BF16),
        grid=(last + 2,),
        in_specs=[pl.BlockSpec((tm, d), lambda s: (cur(s)[0], 0)),
                  pl.BlockSpec((d, HEAD_TILE), w_block),
                  tab, tab, tab],
        out_specs=pl.BlockSpec((1, GROUP_SIZE, tm, HEAD_DIM), lambda s: (prev(s)[1], 0, prev(s)[0], 0)),
        scratch_shapes=[pltpu.VMEM((2, tm, HEAD_TILE), _F32)],
        compiler_params=_params("arbitrary"),
        name="heads_proj",
    )(h, w_in_bf16, *tables)


def _cmp_proj_kernel(h_ref, w_ref, c_ref, slo_ref, shi_ref, o_ref, stage_ref):
    acc = _dot(h_ref[...], w_ref[...])
    c, slo, shi = c_ref[0], slo_ref[0], shi_ref[0]
    for g in range(N_KV_GROUPS):
        stage_ref[g] = _rotate_head(acc[:, g * HEAD_DIM:(g + 1) * HEAD_DIM], c, slo, shi)

    n_chunk = stage_ref.shape[1] // CMP_STRIDE
    for g in range(N_KV_GROUPS):
        for l in range(CMP_STRIDE):
            rows = stage_ref[g, pl.ds(l, n_chunk, stride=CMP_STRIDE), :]
            o_ref[0, g, :, l * HEAD_DIM:(l + 1) * HEAD_DIM] = rows.astype(o_ref.dtype)


def _cmp_proj(h, w_in_bf16, tables, seq):
    t, d = h.shape
    tm = PROJ_TILE
    spt = seq // tm
    tab = pl.BlockSpec((1, tm, HEAD_DIM), lambda i, j: (jnp.where(j == 0, ROPE_K, ROPE_NONE), i % spt, 0))
    return pl.pallas_call(
        _cmp_proj_kernel,
        out_shape=jax.ShapeDtypeStruct((2, N_KV_GROUPS, t // CMP_STRIDE, CMP_STRIDE * HEAD_DIM), _BF16),
        grid=(t // tm, 2),
        in_specs=[pl.BlockSpec((tm, d), lambda i, j: (i, 0)),
                  pl.BlockSpec((d, HEAD_TILE), lambda i, j: (0, j + CMP_COL_BLOCK)),
                  tab, tab, tab],
        out_specs=pl.BlockSpec((1, N_KV_GROUPS, tm // CMP_STRIDE, CMP_STRIDE * HEAD_DIM),
                               lambda i, j: (j, 0, i, 0)),
        scratch_shapes=[pltpu.VMEM((N_KV_GROUPS, tm, HEAD_DIM), _F32)],
        compiler_params=_params("parallel", "arbitrary"),
        name="cmp_proj",
    )(h, w_in_bf16, *tables)


def _gate_kernel(h_ref, w_ref, o_ref):
    o_ref[...] = _sigmoid(_dot(h_ref[...], w_ref[...]))


def _gate_proj(h, w_gate_bf16):
    t, d = h.shape
    n = w_gate_bf16.shape[1]
    return pl.pallas_call(
        _gate_kernel,
        out_shape=jax.ShapeDtypeStruct((t, n), _F32),
        grid=(t // ROW_TILE,),
        in_specs=[pl.BlockSpec((ROW_TILE, d), lambda i: (i, 0)),
                  pl.BlockSpec((d, n), lambda i: (0, 0))],
        out_specs=pl.BlockSpec((ROW_TILE, n), lambda i: (i, 0)),
        compiler_params=_params("parallel"),
        name="gate_proj",
    )(h, w_gate_bf16)


def _compress_kernel(kv_ref, pos_ref, w1_ref, w2_ref, o_ref):
    half = CMP_STRIDE * HEAD_DIM
    c = kv_ref[0, 0].astype(_F32)
    n_chunk = c.shape[0]
    top = (c + pos_ref[0, 0]).astype(_BF16)
    bot = (c + pos_ref[0, 1]).astype(_BF16)
    u = _dot(top, w1_ref[0, :half, :])
    v = _dot(bot, w1_ref[0, half:, :])
    hidden = u + pltpu.roll(v, n_chunk - 1, 0)
    act = jax.nn.gelu(hidden).astype(_BF16)
    o_ref[0, 0, 0] = _dot(act, w2_ref[0]).astype(o_ref.dtype)


def _compress(cmp_flat, pos, w1, w2, batch, seq):
    n_chunk = seq // CMP_STRIDE
    return pl.pallas_call(
        _compress_kernel,
        out_shape=jax.ShapeDtypeStruct((2, batch, N_KV_GROUPS, n_chunk, HEAD_DIM), _BF16),
        grid=(2, batch, N_KV_GROUPS),
        in_specs=[pl.BlockSpec((1, 1, n_chunk, CMP_STRIDE * HEAD_DIM), lambda s, b, g: (s, g, b, 0)),
                  pl.BlockSpec((1, 2, 1, CMP_STRIDE * HEAD_DIM), lambda s, b, g: (s, 0, 0, 0)),
                  pl.BlockSpec((1, CMP_BLOCK * HEAD_DIM, CMP_HIDDEN), lambda s, b, g: (s, 0, 0)),
                  pl.BlockSpec((1, CMP_HIDDEN, HEAD_DIM), lambda s, b, g: (s, 0, 0))],
        out_specs=pl.BlockSpec((1, 1, 1, n_chunk, HEAD_DIM), lambda s, b, g: (s, b, g, 0, 0)),
        compiler_params=_params("parallel", "parallel", "parallel"),
        name="compress",
    )(cmp_flat, pos, w1, w2)


def _conv_kernel(cur_ref, halo_ref, dw_ref, dwb_ref, g_ref, b_ref, o_ref, ext_ref, sh_ref, acc_ref):
    i = pl.program_id(1)
    ts = cur_ref.shape[0]
    ext_ref[:CONV_HALO, :] = jnp.where(i > 0, halo_ref[...], 0.0)
    ext_ref[CONV_HALO:, :] = cur_ref[...]
    n_sh = sh_ref.shape[1]
    for b in range(1, SUBLANES):
        sh_ref[b - 1] = ext_ref[b:b + n_sh, :]
    off = CONV_HALO - (DW_TAPS - 1)

    def chunk(c, carry):
        r0 = pl.multiple_of(c * CONV_ROWS, CONV_ROWS)
        y = None
        for k in range(DW_TAPS):
            a, b = divmod(off + k, SUBLANES)
            src = ext_ref if b == 0 else sh_ref.at[b - 1]
            term = src[pl.ds(r0 + SUBLANES * a, CONV_ROWS), :] * dw_ref[k:k + 1, :]
            y = term if y is None else y + term
        acc_ref[pl.ds(r0, CONV_ROWS), :] = y
        return carry

    lax.fori_loop(0, ts // CONV_ROWS, chunk, 0)
    y = acc_ref[...] + dwb_ref[...]
    mu = jnp.mean(y, axis=-1, keepdims=True)
    yc = y - mu
    var = jnp.mean(yc * yc, axis=-1, keepdims=True)
    z = yc * lax.rsqrt(var + NORM_EPS) * g_ref[...] + b_ref[...]
    o_ref[...] = (z * _sigmoid(z)).astype(o_ref.dtype)


def _conv(glu, dw, dw_b, ln_g, ln_b, batch, seq):
    t, c = glu.shape
    ts = CONV_TILE
    nt = seq // ts
    hb = ts // CONV_HALO
    dw_pad = jnp.concatenate([dw, jnp.zeros((CONV_HALO - DW_TAPS, c), _F32)], axis=0)
    vec = pl.BlockSpec((1, c), lambda b, i: (0, 0))
    return pl.pallas_call(
        _conv_kernel,
        out_shape=jax.ShapeDtypeStruct((t, c), _BF16),
        grid=(batch, nt),
        in_specs=[pl.BlockSpec((ts, c), lambda b, i: (b * nt + i, 0)),
                  pl.BlockSpec((CONV_HALO, c), lambda b, i: (jnp.maximum((b * nt + i) * hb - 1, 0), 0)),
                  pl.BlockSpec((CONV_HALO, c), lambda b, i: (0, 0)),
                  vec, vec, vec],
        out_specs=pl.BlockSpec((ts, c), lambda b, i: (b * nt + i, 0)),
        scratch_shapes=[pltpu.VMEM((ts + CONV_HALO, c), _F32),
                        pltpu.VMEM((SUBLANES - 1, ts + CONV_HALO - SUBLANES, c), _F32),
                        pltpu.VMEM((ts, c), _F32)],
        compiler_params=_params("parallel", "arbitrary"),
        name="conv",
    )(glu, glu, dw_pad, dw_b.reshape(1, c), ln_g.reshape(1, c), ln_b.reshape(1, c))


def _importance_matrix(n_cmp_pad, n_blk):
    r_sel, r_cmp = SEL_BLOCK // CMP_STRIDE, CMP_BLOCK // CMP_STRIDE
    a = np.zeros((n_blk, n_cmp_pad), np.float32)
    for j in range(n_blk):
        for m in range(r_sel):
            for n in range(r_cmp):
                c = r_sel * j + m - n
                if 0 <= c < n_cmp_pad - 1:
                    a[j, c] += 1.0
    return a


def _expand_matrix(n_blk, seq):
    e = np.zeros((LANES, seq), np.float32)
    for j in range(n_blk):
        e[j, j * SEL_BLOCK:(j + 1) * SEL_BLOCK] = 1.0
    return e


def _nsa_step(n_active, q_ref, kc_ref, vc_ref, ks_ref, vs_ref, kw_ref, vw_ref, gate_ref, amat_ref, emat_ref,
              o_ref):
    tq_n = q_ref.shape[2]
    n_cmp_pad = kc_ref.shape[3]
    n_blk = amat_ref.shape[0]
    rows = GROUP_SIZE * tq_n
    start = pl.program_id(2) * tq_n
    q = q_ref[0].reshape(rows, HEAD_DIM)
    tq = start + lax.broadcasted_iota(jnp.int32, (tq_n, 1), 0)

    span = WINDOW + tq_n
    kstart = pl.multiple_of(jnp.maximum(start - WINDOW, 0), tq_n)
    kpos = kstart + lax.broadcasted_iota(jnp.int32, (1, span), 1)
    wbias = jnp.where((kpos <= tq) & (tq - kpos < WINDOW), 0.0, NEG_INF)
    sw = _dot_nt(q, kw_ref[0, 0, pl.ds(kstart, span), :]).reshape(GROUP_SIZE, tq_n, span) + wbias[None]

    cidx = lax.broadcasted_iota(jnp.int32, (1, n_cmp_pad), 1)
    cvalid = (cidx * CMP_STRIDE + (CMP_BLOCK - 1) <= tq) & (cidx < n_cmp_pad - 1)
    cbias = jnp.where(cvalid, 0.0, NEG_INF)
    s = _dot_nt(q, kc_ref[0, 0, 0]).reshape(GROUP_SIZE, tq_n, n_cmp_pad) + cbias[None]
    e_w = jnp.exp2(sw - jnp.max(sw, axis=-1, keepdims=True)).astype(_BF16)
    e = jnp.exp2(s - jnp.max(s, axis=-1, keepdims=True))
    inv_l = jnp.where((tq >= CMP_BLOCK - 1)[None], 1.0 / jnp.sum(e, axis=-1, keepdims=True), 0.0)
    p = e * inv_l
    vw_aug = jnp.concatenate([vw_ref[0, 0, pl.ds(kstart, span), :], jnp.ones((span, HEAD_DIM), _BF16)], axis=1)
    win = _dot(e_w.reshape(rows, span), vw_aug).reshape(GROUP_SIZE, tq_n, 2 * HEAD_DIM)
    o_cmp = _dot(p.reshape(rows, n_cmp_pad).astype(_BF16), vc_ref[0, 0, 0])

    imp = lax.dot_general(amat_ref[...], jnp.sum(p, axis=0), (((1,), (1,)), ((), ())),
                          preferred_element_type=_F32, precision=lax.Precision.HIGHEST)
    tq_row = start + lax.broadcasted_iota(jnp.int32, (1, tq_n), 1)
    blk = lax.broadcasted_iota(jnp.int32, (n_blk, 1), 0)
    cur = tq_row // SEL_BLOCK
    forced = (blk == 0) | ((blk <= cur) & (blk > cur - N_FORCED_LOCAL))
    imp = jnp.where(forced, imp + FORCE_BONUS, imp)
    imp = jnp.where(blk * SEL_BLOCK <= tq_row, imp, NEG_INF)
    rank = jnp.zeros((n_blk, tq_n), _F32)
    for i in range(n_blk):
        row = imp[i:i + 1, :]
        before = (row > imp) | ((row == imp) & (blk > i))
        rank = rank + jnp.where(before, 1.0, 0.0)
    sel_t = jnp.where(rank < float(min(N_SELECT, n_blk)), 1.0, 0.0)
    sel = jnp.concatenate([sel_t, jnp.zeros((LANES - n_blk, tq_n), _F32)], axis=0).T.astype(_BF16)

    ones_c = jnp.ones((SEL_CHUNK, HEAD_DIM), _BF16)

    def scores(c):
        return _dot_nt(q, ks_ref[0, 0, c * SEL_CHUNK:(c + 1) * SEL_CHUNK, :]).reshape(GROUP_SIZE, tq_n, SEL_CHUNK)

    m_run = None
    acc = None
    sc_next = scores(0)
    for c in range(n_active):
        lo = c * SEL_CHUNK
        sc_raw = sc_next
        if c + 1 < n_active:
            sc_next = scores(c + 1)
        chosen = _dot(sel, emat_ref[:, lo:lo + SEL_CHUNK]) > 0.5
        if c == n_active - 1:
            kpos = lo + lax.broadcasted_iota(jnp.int32, (1, SEL_CHUNK), 1)
            chosen = chosen & (kpos <= tq)
        sc = sc_raw + jnp.where(chosen, 0.0, NEG_INF)[None]
        m_c = jnp.max(sc, axis=-1, keepdims=True)
        m_new = m_c if c == 0 else jnp.maximum(m_run, m_c)
        e_c = jnp.exp2(sc - m_new).astype(_BF16)
        v_aug = jnp.concatenate([vs_ref[0, 0, lo:lo + SEL_CHUNK, :], ones_c], axis=1)
        pv = _dot(e_c.reshape(rows, SEL_CHUNK), v_aug).reshape(GROUP_SIZE, tq_n, 2 * HEAD_DIM)
        acc = pv if c == 0 else jnp.exp2(m_run - m_new) * acc + pv
        m_run = m_new

    o_cmp = o_cmp.reshape(GROUP_SIZE, tq_n, HEAD_DIM)
    gates = gate_ref[0]
    for r in range(GROUP_SIZE):
        g_cmp = gates[:, r:r + 1]
        g_slc = gates[:, GROUP_SIZE + r:GROUP_SIZE + r + 1] / acc[r, :, HEAD_DIM:]
        g_win = gates[:, 2 * GROUP_SIZE + r:2 * GROUP_SIZE + r + 1] / win[r, :, HEAD_DIM:]
        o = g_cmp * o_cmp[r] + g_slc * acc[r, :, :HEAD_DIM] + g_win * win[r, :, :HEAD_DIM]
        o_ref[:, r * HEAD_DIM:(r + 1) * HEAD_DIM] = o.astype(o_ref.dtype)


def _nsa_kernel(q_ref, kc_ref, vc_ref, ks_ref, vs_ref, kw_ref, vw_ref, gate_ref, amat_ref, emat_ref, o_ref):
    tq_n = q_ref.shape[2]
    seq = ks_ref.shape[2]
    n_active = (pl.program_id(2) * tq_n + tq_n - 1) // SEL_CHUNK + 1
    for k in range(1, seq // SEL_CHUNK + 1):
        @pl.when(n_active == k)
        def _(k=k):
            _nsa_step(k, q_ref, kc_ref, vc_ref, ks_ref, vs_ref, kw_ref, vw_ref, gate_ref, amat_ref, emat_ref,
                      o_ref)


def _nsa(heads, cmp_kv, gates, batch, seq):
    t = heads.shape[2]
    tq = Q_TILE
    nq = seq // tq
    n_chunk = cmp_kv.shape[3]
    n_blk = seq // SEL_BLOCK
    assert n_chunk == LANES and n_blk <= LANES and seq % SEL_CHUNK == 0 and WINDOW % tq == 0
    amat = jnp.asarray(_importance_matrix(n_chunk, n_blk))
    emat = jnp.asarray(_expand_matrix(n_blk, seq), dtype=_BF16)

    def kv_spec(slab):
        return pl.BlockSpec((1, 1, seq, HEAD_DIM), lambda b, g, i: (slab, g, b, 0))

    def cmp_spec(s):
        return pl.BlockSpec((1, 1, 1, n_chunk, HEAD_DIM), lambda b, g, i: (s, b, g, 0, 0))

    return pl.pallas_call(
        _nsa_kernel,
        out_shape=jax.ShapeDtypeStruct((t, Q_COLS), _BF16),
        grid=(batch, N_KV_GROUPS, nq),
        in_specs=[pl.BlockSpec((1, GROUP_SIZE, tq, HEAD_DIM), lambda b, g, i: (g, 0, b * nq + i, 0)),
                  cmp_spec(0), cmp_spec(1),
                  kv_spec(N_Q_SLABS), kv_spec(N_Q_SLABS + 1),
                  kv_spec(N_Q_SLABS + 2), kv_spec(N_Q_SLABS + 3),
                  pl.BlockSpec((1, tq, 3 * GROUP_SIZE), lambda b, g, i: (g, b * nq + i, 0)),
                  pl.BlockSpec((n_blk, n_chunk), lambda b, g, i: (0, 0)),
                  pl.BlockSpec((LANES, seq), lambda b, g, i: (0, 0))],
        out_specs=pl.BlockSpec((tq, HEAD_TILE), lambda b, g, i: (b * nq + i, g)),
        compiler_params=_params("parallel", "parallel", "arbitrary"),
        name="nsa",
    )(heads, cmp_kv, cmp_kv, heads, heads, heads, heads, gates, amat, emat)


def _mix_kernel(c_ref, o_ref, h_ref, wpc_ref, wpn_ref, wmc_ref, wmn_ref, bc_ref, bn_ref, out_ref):
    h = h_ref[...]
    g_conv = _sigmoid(_dot(h, wmc_ref[...]) + bc_ref[...])
    g_nsa = _sigmoid(_dot(h, wmn_ref[...]) + bn_ref[...])
    y_conv = _dot(c_ref[...], wpc_ref[...])
    y_nsa = _dot(o_ref[...], wpn_ref[...])
    out_ref[...] = (g_conv * y_conv + g_nsa * y_nsa).astype(out_ref.dtype)


def _mix(c, o, h, wpc, wpn, wm, bm):
    t, d = h.shape
    tm, tn = MIX_TILE, MIX_COLS
    nb = d // tn
    bm2 = bm.reshape(1, 2 * d)
    return pl.pallas_call(
        _mix_kernel,
        out_shape=jax.ShapeDtypeStruct((t, d), _BF16),
        grid=(t // tm, nb),
        in_specs=[pl.BlockSpec((tm, CONV_CHANNELS), lambda i, j: (i, 0)),
                  pl.BlockSpec((tm, Q_COLS), lambda i, j: (i, 0)),
                  pl.BlockSpec((tm, d), lambda i, j: (i, 0)),
                  pl.BlockSpec((CONV_CHANNELS, tn), lambda i, j: (0, j)),
                  pl.BlockSpec((Q_COLS, tn), lambda i, j: (0, j)),
                  pl.BlockSpec((d, tn), lambda i, j: (0, j)),
                  pl.BlockSpec((d, tn), lambda i, j: (0, j + nb)),
                  pl.BlockSpec((1, tn), lambda i, j: (0, j)),
                  pl.BlockSpec((1, tn), lambda i, j: (0, j + nb))],
        out_specs=pl.BlockSpec((tm, tn), lambda i, j: (i, j)),
        compiler_params=_params("parallel", "arbitrary"),
        name="mix",
    )(c, o, h, wpc, wpn, wm, wm, bm2, bm2)


def _out_router_kernel(mix_ref, x_ref, wout_ref, g_ref, wrh_ref, wrl_ref, br_ref, x1_ref, h2_ref, route_ref):
    x1 = x_ref[...] + _dot(mix_ref[...], wout_ref[...])
    x1_ref[...] = x1
    ms = jnp.mean(x1 * x1, axis=-1, keepdims=True)
    h2 = x1 * lax.rsqrt(ms + NORM_EPS) * g_ref[...]
    h2_ref[...] = h2
    h_hi = h2.astype(_BF16)
    h_lo = (h2 - h_hi.astype(_F32)).astype(_BF16)
    logits = (_dot(h_hi, wrh_ref[...]) + _dot(h_lo, wrh_ref[...]) + _dot(h_hi, wrl_ref[...])
              + br_ref[...])
    lane = lax.broadcasted_iota(jnp.int32, (1, LANES), 1)
    lane_f = lane.astype(_F32)
    big = float(LANES)
    gl = jnp.where(lane < N_GROUPS, logits, NEG_INF)
    gmax = jnp.max(gl, axis=-1, keepdims=True)
    p_g = 1.0 / jnp.sum(jnp.where(lane < N_GROUPS, jnp.exp(gl - gmax), 0.0), axis=-1, keepdims=True)
    g_sel = jnp.min(jnp.where(gl == gmax, lane_f, big), axis=-1, keepdims=True)
    lo = N_GROUPS + g_sel * EXPERTS_PER_GROUP
    in_grp = (lane_f >= lo) & (lane_f < lo + EXPERTS_PER_GROUP)
    el = jnp.where(in_grp, logits, NEG_INF)
    v0 = jnp.max(el, axis=-1, keepdims=True)
    i0 = jnp.min(jnp.where(el == v0, lane_f, big), axis=-1, keepdims=True)
    el2 = jnp.where(lane_f == i0, NEG_INF, el)
    v1 = jnp.max(el2, axis=-1, keepdims=True)
    i1 = jnp.min(jnp.where(el2 == v1, lane_f, big), axis=-1, keepdims=True)
    t1 = jnp.exp(v1 - v0)
    w0 = p_g / (1.0 + t1)
    w1 = p_g * t1 / (1.0 + t1)
    route = jnp.where(lane == 0, i0 - N_GROUPS, 0.0)
    route = jnp.where(lane == 1, i1 - N_GROUPS, route)
    route = jnp.where(lane == 2, w0, route)
    route = jnp.where(lane == 3, w1, route)
    route_ref[...] = route


def _out_router(mixed, x, w_out, ffn_g, w_route, b_route):
    t, d = x.shape
    tm = OUT_TILE
    row = pl.BlockSpec((tm, d), lambda i: (i, 0))
    w_route_hi = w_route.astype(_BF16)
    w_route_lo = (w_route - w_route_hi.astype(_F32)).astype(_BF16)
    return pl.pallas_call(
        _out_router_kernel,
        out_shape=(jax.ShapeDtypeStruct((t, d), _F32),
                   jax.ShapeDtypeStruct((t, d), _F32),
                   jax.ShapeDtypeStruct((t, LANES), _F32)),
        grid=(t // tm,),
        in_specs=[row, row,
                  pl.BlockSpec((d, d), lambda i: (0, 0), pipeline_mode=pl.Buffered(1)),
                  pl.BlockSpec((1, d), lambda i: (0, 0)),
                  pl.BlockSpec((d, LANES), lambda i: (0, 0)),
                  pl.BlockSpec((d, LANES), lambda i: (0, 0)),
                  pl.BlockSpec((1, LANES), lambda i: (0, 0))],
        out_specs=(row, row, pl.BlockSpec((tm, LANES), lambda i: (i, 0))),
        compiler_params=_params("parallel"),
        name="out_router",
    )(mixed, x, w_out, ffn_g.reshape(1, d), w_route_hi, w_route_lo, b_route)


def _moe_kernel(be_ref, half_ref, tok_ref, nb_ref, h2_hbm, w1_ref, w3_ref, w2_ref, y_ref, xbuf, w1b, w3b, w2b, sem):
    i = pl.program_id(0)
    nb = nb_ref[0]
    slot = i % 2
    bm = xbuf.shape[1]
    hb = bm // 2

    def row_copy(tok, r, sl):
        return pltpu.make_async_copy(h2_hbm.at[pl.ds(tok, 1), :], xbuf.at[sl, pl.ds(r, 1), :], sem.at[sl])

    def rows_wait(sl, n):
        pltpu.make_async_copy(h2_hbm.at[pl.ds(0, n), :], xbuf.at[sl, pl.ds(0, n), :], sem.at[sl]).wait()

    def ffn(x):
        a = _dot(x, w1b[...])
        b = _dot(x, w3b[...])
        act = (a * _sigmoid(a) * b).astype(_BF16)
        return _dot(act, w2b[...])

    @pl.when(i == 0)
    def _():
        def body(r, carry):
            row_copy(tok_ref[r], r, 0).start()
            return carry
        lax.fori_loop(0, jnp.where(half_ref[0] == 1, hb, bm), body, 0)

    for par in range(2):
        for n_rows, is_half in ((bm, 0), (hb, 1)):
            @pl.when((i + 1 < nb) & (slot == par) & (half_ref[jnp.minimum(i + 1, nb - 1)] == is_half))
            def _(par=par, n_rows=n_rows):
                base = (i + 1) * bm
                for r in range(n_rows):
                    row_copy(tok_ref[base + r], r, 1 - par).start()

    @pl.when(i < nb)
    def _():
        half = half_ref[i] == 1

        @pl.when(half)
        def _():
            rows_wait(slot, hb)

        @pl.when(jnp.logical_not(half))
        def _():
            rows_wait(slot, bm)

        changed = (i == 0) | (be_ref[i] != be_ref[jnp.maximum(i - 1, 0)])

        @pl.when(changed)
        def _():
            w1b[...] = w1_ref[0].astype(_BF16)
            w3b[...] = w3_ref[0].astype(_BF16)
            w2b[...] = w2_ref[0].astype(_BF16)

        @pl.when(half)
        def _():
            y_ref[:hb, :] = ffn(xbuf[slot, :hb, :].astype(_BF16))
            y_ref[hb:, :] = jnp.zeros((bm - hb, y_ref.shape[1]), y_ref.dtype)

        @pl.when(jnp.logical_not(half))
        def _():
            y_ref[...] = ffn(xbuf[slot].astype(_BF16))

    @pl.when(i >= nb)
    def _():
        y_ref[...] = jnp.zeros(y_ref.shape, y_ref.dtype)


def _moe(h2, tables, w1, w3, w2):
    t, d = h2.shape
    bm = MOE_ROWS
    block_e, block_half, row_tok, n_used = tables
    n_blocks = block_e.shape[0]
    ff = w1.shape[2]
    grid_spec = pltpu.PrefetchScalarGridSpec(
        num_scalar_prefetch=4,
        grid=(n_blocks,),
        in_specs=[pl.BlockSpec(memory_space=pl.ANY),
                  pl.BlockSpec((1, d, ff), lambda i, be, *_: (be[i], 0, 0)),
                  pl.BlockSpec((1, d, ff), lambda i, be, *_: (be[i], 0, 0)),
                  pl.BlockSpec((1, ff, d), lambda i, be, *_: (be[i], 0, 0))],
        out_specs=pl.BlockSpec((bm, d), lambda i, *_: (i, 0)),
        scratch_shapes=[pltpu.VMEM((2, bm, d), _F32),
                        pltpu.VMEM((d, ff), _BF16),
                        pltpu.VMEM((d, ff), _BF16),
                        pltpu.VMEM((ff, d), _BF16),
                        pltpu.SemaphoreType.DMA((2,))],
    )
    return pl.pallas_call(
        _moe_kernel,
        out_shape=jax.ShapeDtypeStruct((n_blocks * bm, d), _F32),
        grid_spec=grid_spec,
        compiler_params=_params("arbitrary", disable_bounds_checks=True),
        name="moe_experts",
    )(block_e, block_half, row_tok, n_used, h2, w1, w3, w2)


def _final_kernel(pos_ref, y_hbm, x1_ref, route_ref, g_ref, o_ref, ybuf, sem):
    i = pl.program_id(0)
    n = pl.num_programs(0)
    tm = x1_ref.shape[0]

    def row_copy(p, r, k, sl):
        return pltpu.make_async_copy(y_hbm.at[pl.ds(p, 1), :], ybuf.at[sl, k, pl.ds(r, 1), :], sem.at[sl])

    @pl.when(i == 0)
    def _():
        def body(r, carry):
            for k in range(TOP_K):
                row_copy(pos_ref[r * TOP_K + k], r, k, 0).start()
            return carry
        lax.fori_loop(0, tm, body, 0, unroll=DMA_UNROLL)

    def step(par, issue_next):
        if issue_next:
            base = (i + 1) * tm * TOP_K
            for r in range(tm):
                for k in range(TOP_K):
                    row_copy(pos_ref[base + r * TOP_K + k], r, k, 1 - par).start()
        for k in range(TOP_K):
            pltpu.make_async_copy(y_hbm.at[pl.ds(0, tm), :], ybuf.at[par, k], sem.at[par]).wait()
        route = route_ref[...]
        x = x1_ref[...] + route[:, 2:3] * ybuf[par, 0] + route[:, 3:4] * ybuf[par, 1]
        ms = jnp.mean(x * x, axis=-1, keepdims=True)
        o_ref[...] = x * lax.rsqrt(ms + NORM_EPS) * g_ref[...]

    for par in range(2):
        @pl.when((i % 2 == par) & (i + 1 < n))
        def _(par=par):
            step(par, True)

        @pl.when((i % 2 == par) & (i + 1 == n))
        def _(par=par):
            step(par, False)


def _final(pos, y_rows, x1, route, g):
    t, d = x1.shape
    tm = FINAL_TILE
    grid_spec = pltpu.PrefetchScalarGridSpec(
        num_scalar_prefetch=1,
        grid=(t // tm,),
        in_specs=[pl.BlockSpec(memory_space=pl.ANY),
                  pl.BlockSpec((tm, d), lambda i, pos: (i, 0)),
                  pl.BlockSpec((tm, LANES), lambda i, pos: (i, 0)),
                  pl.BlockSpec((1, d), lambda i, pos: (0, 0))],
        out_specs=pl.BlockSpec((tm, d), lambda i, pos: (i, 0)),
        scratch_shapes=[pltpu.VMEM((2, TOP_K, tm, d), _F32),
                        pltpu.SemaphoreType.DMA((2,))],
    )
    return pl.pallas_call(
        _final_kernel,
        out_shape=jax.ShapeDtypeStruct((t, d), _F32),
        grid_spec=grid_spec,
        compiler_params=_params("arbitrary", disable_bounds_checks=True),
        name="combine_norm",
    )(pos, y_rows, x1, route, g.reshape(1, d))


def _routing_tables(expert, n_tokens):
    bm = MOE_ROWS
    n_assign = n_tokens * TOP_K
    n_blocks = -(-(n_assign + N_EXPERTS * (bm - 1)) // bm)
    i32 = jnp.int32
    flat_e = expert.reshape(-1)
    experts = jnp.arange(N_EXPERTS, dtype=i32)
    assign = jnp.arange(n_assign, dtype=i32)
    e_sorted, order = lax.sort((flat_e, assign), num_keys=1)
    start = jnp.sum((e_sorted[:, None] < experts[None, :]).astype(i32), axis=0)
    counts = jnp.concatenate([start[1:], jnp.full((1,), n_assign, i32)]) - start
    padded = (counts + bm - 1) // bm * bm
    pend = jnp.cumsum(padded)
    pstart = pend - padded
    shift = pstart - start
    dshift = shift - jnp.concatenate([jnp.zeros((1,), i32), shift[:-1]])
    shift_sorted = jnp.sum(jnp.where(assign[:, None] >= start[None, :], dshift[None, :], 0), axis=1)
    _, pos = lax.sort((order, assign + shift_sorted), num_keys=1)
    block_row0 = jnp.arange(n_blocks, dtype=i32) * bm
    block_e = jnp.minimum(jnp.sum((pend[None, :] <= block_row0[:, None]).astype(i32), axis=1), N_EXPERTS - 1)
    onehot = block_e[:, None] == experts[None, :]

    def per_block(table):
        return jnp.sum(jnp.where(onehot, table[None, :], 0), axis=1)

    block_start = per_block(start - pstart) + block_row0
    block_end = per_block(start + counts)
    src = block_start[:, None] + jnp.arange(bm, dtype=i32)[None, :]
    valid = src < block_end[:, None]
    tok_sorted = order // TOP_K
    idx = jnp.clip(src, 0, n_assign - 1)
    hb = bm // 2
    looked_up = jnp.concatenate([tok_sorted[idx[:, :hb]], tok_sorted[idx[:, hb:]]], axis=1)
    row_tok = jnp.where(valid, looked_up, 0)
    block_half = (block_end - block_start <= bm // 2).astype(i32)
    n_used = (pend[-1] // bm).astype(i32).reshape(1)
    return (block_e, block_half, row_tok.reshape(-1).astype(i32), n_used), pos.astype(i32)


def kernel(x, attn_norm, w_in, conv_dw, conv_dw_b, conv_ln_g, conv_ln_b, cmp_pos_k, cmp_k_w1, cmp_k_w2, cmp_pos_v, cmp_v_w1, cmp_v_w2, w_proj_conv, w_proj_nsa, w_merge, b_merge, w_out, ffn_norm, w_grp, b_grp, w_exp, b_exp, exp_w1, exp_w3, exp_w2, final_norm):
    batch, seq, d = x.shape
    t = batch * seq
    assert w_in.shape[0] == 1, "the block is specified with a single layer"
    xf = x.reshape(t, d)

    w_in_b = w_in[0].astype(_BF16)
    w_gate = w_in[0][:, 2 * CONV_CHANNELS + HEAD_COLS:].reshape(d, 3, N_KV_GROUPS, GROUP_SIZE)
    w_gate = w_gate.transpose(0, 2, 1, 3).reshape(d, GATE_COLS).astype(_BF16)

    h = _rmsnorm(xf, attn_norm[0], _BF16)
    glu = _glu_proj(h, w_in_b)
    tables = _rope_tables(seq)
    heads = _heads_proj(h, w_in_b, tables, seq)
    cmp_flat = _cmp_proj(h, w_in_b, tables, seq)
    gates = _gate_proj(h, w_gate).reshape(t, N_KV_GROUPS, 3 * GROUP_SIZE).transpose(1, 0, 2)

    half = CMP_STRIDE * HEAD_DIM
    pos = jnp.stack([cmp_pos_k[0], cmp_pos_v[0]]).reshape(2, 2, 1, half)
    cw1 = jnp.stack([cmp_k_w1[0], cmp_v_w1[0]]).astype(_BF16)
    cw2 = jnp.stack([cmp_k_w2[0], cmp_v_w2[0]]).astype(_BF16)
    cmp_kv = _compress(cmp_flat, pos, cw1, cw2, batch, seq)

    c = _conv(glu, conv_dw[0], conv_dw_b[0], conv_ln_g[0], conv_ln_b[0], batch, seq)
    o = _nsa(heads, cmp_kv, gates, batch, seq)
    mixed = _mix(c, o, h, w_proj_conv[0].astype(_BF16), w_proj_nsa[0].astype(_BF16),
                 w_merge[0].astype(_BF16), b_merge[0])

    w_route = jnp.concatenate(
        [w_grp[0], w_exp[0], jnp.zeros((d, LANES - N_GROUPS - N_EXPERTS), _F32)], axis=1)
    b_route = jnp.concatenate(
        [b_grp[0], b_exp[0], jnp.zeros((LANES - N_GROUPS - N_EXPERTS,), _F32)]).reshape(1, LANES)
    x1, h2, route = _out_router(mixed, xf, w_out[0].astype(_BF16), ffn_norm[0], w_route, b_route)

    expert = route[:, :TOP_K].astype(jnp.int32)
    moe_tables, pos_rows = _routing_tables(expert, t)
    y_rows = _moe(h2, moe_tables, exp_w1[0], exp_w3[0], exp_w2[0])
    out = _final(pos_rows, y_rows, x1, route, final_norm)
    return out.reshape(batch, seq, d)
```

```python
import numpy as np
import jax
import jax.numpy as jnp
from jax import lax
from jax.experimental import pallas as pl
from jax.experimental.pallas import tpu as pltpu

D_MODEL = 2048
CONV_CHANNELS = D_MODEL // 2
DW_TAPS = 31
N_HEADS = 16
HEAD_DIM = 128
N_KV_GROUPS = 4
GROUP_SIZE = N_HEADS // N_KV_GROUPS
ROT_DIM = HEAD_DIM // 4
ROPE_THETA = 500000.0
CMP_BLOCK = 32
CMP_STRIDE = 16
CMP_HIDDEN = 256
SEL_BLOCK = 64
N_SELECT = 16
N_FORCED_LOCAL = 2
WINDOW = 512
N_GROUPS = 8
EXPERTS_PER_GROUP = 8
N_EXPERTS = N_GROUPS * EXPERTS_PER_GROUP
TOP_K = 2
EXPERT_FF = 512
Q_COLS = N_HEADS * HEAD_DIM
KV_COLS = N_KV_GROUPS * HEAD_DIM
HEAD_COLS = Q_COLS + 6 * KV_COLS
GATE_COLS = 3 * N_HEADS
NORM_EPS = 1e-6
NEG_INF = -1e30
FORCE_BONUS = 1e4
LOG2_E = 1.4426950408889634

LANES = 128
SUBLANES = 8
VMEM_LIMIT_BYTES = 56 * 1024 * 1024

ROW_TILE = 512
PROJ_TILE = 2048
MIX_TILE = 1024
MIX_COLS = 256
HEAD_TILE = GROUP_SIZE * HEAD_DIM
Q_TILE = 256
SEL_CHUNK = 512
CONV_TILE = 256
CONV_HALO = 32
CONV_ROWS = 16
OUT_TILE = 512
MOE_ROWS = 512
FINAL_TILE = 256
DMA_UNROLL = 8

_BF16 = jnp.bfloat16
_F32 = jnp.float32


def _params(*sem, **kw):
    return pltpu.CompilerParams(dimension_semantics=sem, vmem_limit_bytes=VMEM_LIMIT_BYTES, **kw)


def _dot(a, b):
    return jnp.dot(a, b, preferred_element_type=_F32)


def _dot_nt(a, b):
    return lax.dot_general(a, b, (((1,), (1,)), ((), ())), preferred_element_type=_F32)


def _sigmoid(x):
    return 1.0 / (1.0 + jnp.exp(-x))


def _rmsnorm_kernel(x_ref, g_ref, o_ref):
    x = x_ref[...]
    ms = jnp.mean(x * x, axis=-1, keepdims=True)
    o_ref[...] = (x * lax.rsqrt(ms + NORM_EPS) * g_ref[...]).astype(o_ref.dtype)


def _rmsnorm(x, g, out_dtype):
    t, d = x.shape
    return pl.pallas_call(
        _rmsnorm_kernel,
        out_shape=jax.ShapeDtypeStruct((t, d), out_dtype),
        grid=(t // ROW_TILE,),
        in_specs=[pl.BlockSpec((ROW_TILE, d), lambda i: (i, 0)),
                  pl.BlockSpec((1, d), lambda i: (0, 0))],
        out_specs=pl.BlockSpec((ROW_TILE, d), lambda i: (i, 0)),
        compiler_params=_params("parallel"),
        name="rmsnorm",
    )(x, g.reshape(1, d))


def _glu_kernel(h_ref, wa_ref, wb_ref, o_ref):
    h = h_ref[...]
    a = _dot(h, wa_ref[...])
    b = _dot(h, wb_ref[...])
    o_ref[...] = a * _sigmoid(b)


def _glu_proj(h, w_in_bf16):
    t, d = h.shape
    tn = HEAD_TILE
    nb = CONV_CHANNELS // tn
    return pl.pallas_call(
        _glu_kernel,
        out_shape=jax.ShapeDtypeStruct((t, CONV_CHANNELS), _F32),
        grid=(t // PROJ_TILE, nb),
        in_specs=[pl.BlockSpec((PROJ_TILE, d), lambda i, j: (i, 0)),
                  pl.BlockSpec((d, tn), lambda i, j: (0, j)),
                  pl.BlockSpec((d, tn), lambda i, j: (0, j + nb))],
        out_specs=pl.BlockSpec((PROJ_TILE, tn), lambda i, j: (i, j)),
        compiler_params=_params("parallel", "arbitrary"),
        name="glu_proj",
    )(h, w_in_bf16, w_in_bf16)


N_Q_SLABS = Q_COLS // HEAD_TILE
N_HEAD_SLABS = N_Q_SLABS + 4
Q_COL_BLOCK = 2 * CONV_CHANNELS // HEAD_TILE
CMP_COL_BLOCK = Q_COL_BLOCK + N_Q_SLABS
KV_COL_BLOCK = CMP_COL_BLOCK + 2


def _rope_tables(seq):
    pos = jnp.arange(seq, dtype=_F32)
    inv = ROPE_THETA ** (-jnp.arange(0, ROT_DIM, 2, dtype=_F32) / ROT_DIM)
    ang = pos[:, None] * inv[None, :]
    cos, sin = jnp.cos(ang), jnp.sin(ang)
    half = ROT_DIM // 2
    ones = jnp.ones((seq, HEAD_DIM - ROT_DIM), _F32)
    zeros_h = jnp.zeros((seq, half), _F32)
    zeros_r = jnp.zeros((seq, HEAD_DIM - ROT_DIM), _F32)
    c = jnp.concatenate([cos, cos, ones], axis=-1)
    s_lo = jnp.concatenate([zeros_h, sin, zeros_r], axis=-1)
    s_hi = jnp.concatenate([-sin, zeros_h, zeros_r], axis=-1)
    q_scale = HEAD_DIM ** -0.5 * LOG2_E
    zeros = jnp.zeros_like(c)
    return (jnp.stack([c * q_scale, c, jnp.ones_like(c)]),
            jnp.stack([s_lo * q_scale, s_lo, zeros]),
            jnp.stack([s_hi * q_scale, s_hi, zeros]))


ROPE_Q, ROPE_K, ROPE_NONE = 0, 1, 2


def _rotate_head(xs, c, slo, shi):
    half = ROT_DIM // 2
    return xs * c + pltpu.roll(xs, half, 1) * slo + pltpu.roll(xs, HEAD_DIM - half, 1) * shi


def _heads_kernel(h_ref, w_ref, c_ref, slo_ref, shi_ref, o_ref, acc_ref):
    s = pl.program_id(0)

    @pl.when(s == 0)
    def _():
        acc_ref[...] = jnp.zeros(acc_ref.shape, acc_ref.dtype)

    for par in range(2):
        @pl.when(s % 2 == par)
        def _(par=par):
            acc_ref[par] = _dot(h_ref[...], w_ref[...])
            c, slo, shi = c_ref[0], slo_ref[0], shi_ref[0]
            for hd in range(GROUP_SIZE):
                r = _rotate_head(acc_ref[1 - par, :, hd * HEAD_DIM:(hd + 1) * HEAD_DIM], c, slo, shi)
                o_ref[0, hd] = r.astype(o_ref.dtype)


def _heads_proj(h, w_in_bf16, tables, seq):
    t, d = h.shape
    tm = PROJ_TILE
    spt = seq // tm
    n_slabs = N_HEAD_SLABS
    last = (t // tm) * n_slabs - 1

    def cur(s):
        s = jnp.minimum(s, last)
        return s // n_slabs, s % n_slabs

    def prev(s):
        s = jnp.maximum(s - 1, 0)
        return s // n_slabs, s % n_slabs

    def variant(j):
        return jnp.where(j < N_Q_SLABS, ROPE_Q, jnp.where(j % 2 == 0, ROPE_K, ROPE_NONE))

    def w_block(s):
        j = cur(s)[1]
        return (0, jnp.where(j < N_Q_SLABS, j + Q_COL_BLOCK, j - N_Q_SLABS + KV_COL_BLOCK))

    tab = pl.BlockSpec((1, tm, HEAD_DIM), lambda s: (variant(prev(s)[1]), prev(s)[0] % spt, 0))
    return pl.pallas_call(
        _heads_kernel,
        out_shape=jax.ShapeDtypeStruct((n_slabs, GROUP_SIZE, t, HEAD_DIM), _BF16),
        grid=(last + 2,),
        in_specs=[pl.BlockSpec((tm, d), lambda s: (cur(s)[0], 0)),
                  pl.BlockSpec((d, HEAD_TILE), w_block),
                  tab, tab, tab],
        out_specs=pl.BlockSpec((1, GROUP_SIZE, tm, HEAD_DIM), lambda s: (prev(s)[1], 0, prev(s)[0], 0)),
        scratch_shapes=[pltpu.VMEM((2, tm, HEAD_TILE), _F32)],
        compiler_params=_params("arbitrary"),
        name="heads_proj",
    )(h, w_in_bf16, *tables)


def _cmp_proj_kernel(h_ref, w_ref, c_ref, slo_ref, shi_ref, o_ref, stage_ref):
    acc = _dot(h_ref[...], w_ref[...])
    c, slo, shi = c_ref[0], slo_ref[0], shi_ref[0]
    for g in range(N_KV_GROUPS):
        stage_ref[g] = _rotate_head(acc[:, g * HEAD_DIM:(g + 1) * HEAD_DIM], c, slo, shi)

    n_chunk = stage_ref.shape[1] // CMP_STRIDE
    for g in range(N_KV_GROUPS):
        for l in range(CMP_STRIDE):
            rows = stage_ref[g, pl.ds(l, n_chunk, stride=CMP_STRIDE), :]
            o_ref[0, g, :, l * HEAD_DIM:(l + 1) * HEAD_DIM] = rows.astype(o_ref.dtype)


def _cmp_proj(h, w_in_bf16, tables, seq):
    t, d = h.shape
    tm = PROJ_TILE
    spt = seq // tm
    tab = pl.BlockSpec((1, tm, HEAD_DIM), lambda i, j: (jnp.where(j == 0, ROPE_K, ROPE_NONE), i % spt, 0))
    return pl.pallas_call(
        _cmp_proj_kernel,
        out_shape=jax.ShapeDtypeStruct((2, N_KV_GROUPS, t // CMP_STRIDE, CMP_STRIDE * HEAD_DIM), _BF16),
        grid=(t // tm, 2),
        in_specs=[pl.BlockSpec((tm, d), lambda i, j: (i, 0)),
                  pl.BlockSpec((d, HEAD_TILE), lambda i, j: (0, j + CMP_COL_BLOCK)),
                  tab, tab, tab],
        out_specs=pl.BlockSpec((1, N_KV_GROUPS, tm // CMP_STRIDE, CMP_STRIDE * HEAD_DIM),
                               lambda i, j: (j, 0, i, 0)),
        scratch_shapes=[pltpu.VMEM((N_KV_GROUPS, tm, HEAD_DIM), _F32)],
        compiler_params=_params("parallel", "arbitrary"),
        name="cmp_proj",
    )(h, w_in_bf16, *tables)


def _gate_kernel(h_ref, w_ref, o_ref):
    o_ref[...] = _sigmoid(_dot(h_ref[...], w_ref[...]))


def _gate_proj(h, w_gate_bf16):
    t, d = h.shape
    n = w_gate_bf16.shape[1]
    return pl.pallas_call(
        _gate_kernel,
        out_shape=jax.ShapeDtypeStruct((t, n), _F32),
        grid=(t // ROW_TILE,),
        in_specs=[pl.BlockSpec((ROW_TILE, d), lambda i: (i, 0)),
                  pl.BlockSpec((d, n), lambda i: (0, 0))],
        out_specs=pl.BlockSpec((ROW_TILE, n), lambda i: (i, 0)),
        compiler_params=_params("parallel"),
        name="gate_proj",
    )(h, w_gate_bf16)


def _compress_kernel(kv_ref, pos_ref, w1_ref, w2_ref, o_ref):
    half = CMP_STRIDE * HEAD_DIM
    c = kv_ref[0, 0].astype(_F32)
    n_chunk = c.shape[0]
    top = (c + pos_ref[0, 0]).astype(_BF16)
    bot = (c + pos_ref[0, 1]).astype(_BF16)
    u = _dot(top, w1_ref[0, :half, :])
    v = _dot(bot, w1_ref[0, half:, :])
    hidden = u + pltpu.roll(v, n_chunk - 1, 0)
    act = jax.nn.gelu(hidden).astype(_BF16)
    o_ref[0, 0, 0] = _dot(act, w2_ref[0]).astype(o_ref.dtype)


def _compress(cmp_flat, pos, w1, w2, batch, seq):
    n_chunk = seq // CMP_STRIDE
    return pl.pallas_call(
        _compress_kernel,
        out_shape=jax.ShapeDtypeStruct((2, batch, N_KV_GROUPS, n_chunk, HEAD_DIM), _BF16),
        grid=(2, batch, N_KV_GROUPS),
        in_specs=[pl.BlockSpec((1, 1, n_chunk, CMP_STRIDE * HEAD_DIM), lambda s, b, g: (s, g, b, 0)),
                  pl.BlockSpec((1, 2, 1, CMP_STRIDE * HEAD_DIM), lambda s, b, g: (s, 0, 0, 0)),
                  pl.BlockSpec((1, CMP_BLOCK * HEAD_DIM, CMP_HIDDEN), lambda s, b, g: (s, 0, 0)),
                  pl.BlockSpec((1, CMP_HIDDEN, HEAD_DIM), lambda s, b, g: (s, 0, 0))],
        out_specs=pl.BlockSpec((1, 1, 1, n_chunk, HEAD_DIM), lambda s, b, g: (s, b, g, 0, 0)),
        compiler_params=_params("parallel", "parallel", "parallel"),
        name="compress",
    )(cmp_flat, pos, w1, w2)


def _conv_kernel(cur_ref, halo_ref, dw_ref, dwb_ref, g_ref, b_ref, o_ref, ext_ref, sh_ref, acc_ref):
    i = pl.program_id(1)
    ts = cur_ref.shape[0]
    ext_ref[:CONV_HALO, :] = jnp.where(i > 0, halo_ref[...], 0.0)
    ext_ref[CONV_HALO:, :] = cur_ref[...]
    n_sh = sh_ref.shape[1]
    for b in range(1, SUBLANES):
        sh_ref[b - 1] = ext_ref[b:b + n_sh, :]
    off = CONV_HALO - (DW_TAPS - 1)

    def chunk(c, carry):
        r0 = pl.multiple_of(c * CONV_ROWS, CONV_ROWS)
        y = None
        for k in range(DW_TAPS):
            a, b = divmod(off + k, SUBLANES)
            src = ext_ref if b == 0 else sh_ref.at[b - 1]
            term = src[pl.ds(r0 + SUBLANES * a, CONV_ROWS), :] * dw_ref[k:k + 1, :]
            y = term if y is None else y + term
        acc_ref[pl.ds(r0, CONV_ROWS), :] = y
        return carry

    lax.fori_loop(0, ts // CONV_ROWS, chunk, 0)
    y = acc_ref[...] + dwb_ref[...]
    mu = jnp.mean(y, axis=-1, keepdims=True)
    yc = y - mu
    var = jnp.mean(yc * yc, axis=-1, keepdims=True)
    z = yc * lax.rsqrt(var + NORM_EPS) * g_ref[...] + b_ref[...]
    o_ref[...] = (z * _sigmoid(z)).astype(o_ref.dtype)


def _conv(glu, dw, dw_b, ln_g, ln_b, batch, seq):
    t, c = glu.shape
    ts = CONV_TILE
    nt = seq // ts
    hb = ts // CONV_HALO
    dw_pad = jnp.concatenate([dw, jnp.zeros((CONV_HALO - DW_TAPS, c), _F32)], axis=0)
    vec = pl.BlockSpec((1, c), lambda b, i: (0, 0))
    return pl.pallas_call(
        _conv_kernel,
        out_shape=jax.ShapeDtypeStruct((t, c), _BF16),
        grid=(batch, nt),
        in_specs=[pl.BlockSpec((ts, c), lambda b, i: (b * nt + i, 0)),
                  pl.BlockSpec((CONV_HALO, c), lambda b, i: (jnp.maximum((b * nt + i) * hb - 1, 0), 0)),
                  pl.BlockSpec((CONV_HALO, c), lambda b, i: (0, 0)),
                  vec, vec, vec],
        out_specs=pl.BlockSpec((ts, c), lambda b, i: (b * nt + i, 0)),
        scratch_shapes=[pltpu.VMEM((ts + CONV_HALO, c), _F32),
                        pltpu.VMEM((SUBLANES - 1, ts + CONV_HALO - SUBLANES, c), _F32),
                        pltpu.VMEM((ts, c), _F32)],
        compiler_params=_params("parallel", "arbitrary"),
        name="conv",
    )(glu, glu, dw_pad, dw_b.reshape(1, c), ln_g.reshape(1, c), ln_b.reshape(1, c))


def _importance_matrix(n_cmp_pad, n_blk):
    r_sel, r_cmp = SEL_BLOCK // CMP_STRIDE, CMP_BLOCK // CMP_STRIDE
    a = np.zeros((n_blk, n_cmp_pad), np.float32)
    for j in range(n_blk):
        for m in range(r_sel):
            for n in range(r_cmp):
                c = r_sel * j + m - n
                if 0 <= c < n_cmp_pad - 1:
                    a[j, c] += 1.0
    return a


def _expand_matrix(n_blk, seq):
    e = np.zeros((LANES, seq), np.float32)
    for j in range(n_blk):
        e[j, j * SEL_BLOCK:(j + 1) * SEL_BLOCK] = 1.0
    return e


def _nsa_step(n_active, q_ref, kc_ref, vc_ref, ks_ref, vs_ref, kw_ref, vw_ref, gate_ref, amat_ref, emat_ref,
              o_ref):
    tq_n = q_ref.shape[2]
    n_cmp_pad = kc_ref.shape[3]
    n_blk = amat_ref.shape[0]
    rows = GROUP_SIZE * tq_n
    start = pl.program_id(2) * tq_n
    q = q_ref[0].reshape(rows, HEAD_DIM)
    tq = start + lax.broadcasted_iota(jnp.int32, (tq_n, 1), 0)

    span = WINDOW + tq_n
    kstart = pl.multiple_of(jnp.maximum(start - WINDOW, 0), tq_n)
    kpos = kstart + lax.broadcasted_iota(jnp.int32, (1, span), 1)
    wbias = jnp.where((kpos <= tq) & (tq - kpos < WINDOW), 0.0, NEG_INF)
    sw = _dot_nt(q, kw_ref[0, 0, pl.ds(kstart, span), :]).reshape(GROUP_SIZE, tq_n, span) + wbias[None]

    cidx = lax.broadcasted_iota(jnp.int32, (1, n_cmp_pad), 1)
    cvalid = (cidx * CMP_STRIDE + (CMP_BLOCK - 1) <= tq) & (cidx < n_cmp_pad - 1)
    cbias = jnp.where(cvalid, 0.0, NEG_INF)
    s = _dot_nt(q, kc_ref[0, 0, 0]).reshape(GROUP_SIZE, tq_n, n_cmp_pad) + cbias[None]
    e_w = jnp.exp2(sw - jnp.max(sw, axis=-1, keepdims=True)).astype(_BF16)
    e = jnp.exp2(s - jnp.max(s, axis=-1, keepdims=True))
    inv_l = jnp.where((tq >= CMP_BLOCK - 1)[None], 1.0 / jnp.sum(e, axis=-1, keepdims=True), 0.0)
    p = e * inv_l
    vw_aug = jnp.concatenate([vw_ref[0, 0, pl.ds(kstart, span), :], jnp.ones((span, HEAD_DIM), _BF16)], axis=1)
    win = _dot(e_w.reshape(rows, span), vw_aug).reshape(GROUP_SIZE, tq_n, 2 * HEAD_DIM)
    o_cmp = _dot(p.reshape(rows, n_cmp_pad).astype(_BF16), vc_ref[0, 0, 0])

    imp = lax.dot_general(amat_ref[...], jnp.sum(p, axis=0), (((1,), (1,)), ((), ())),
                          preferred_element_type=_F32, precision=lax.Precision.HIGHEST)
    tq_row = start + lax.broadcasted_iota(jnp.int32, (1, tq_n), 1)
    blk = lax.broadcasted_iota(jnp.int32, (n_blk, 1), 0)
    cur = tq_row // SEL_BLOCK
    forced = (blk == 0) | ((blk <= cur) & (blk > cur - N_FORCED_LOCAL))
    imp = jnp.where(forced, imp + FORCE_BONUS, imp)
    imp = jnp.where(blk * SEL_BLOCK <= tq_row, imp, NEG_INF)
    rank = jnp.zeros((n_blk, tq_n), _F32)
    for i in range(n_blk):
        row = imp[i:i + 1, :]
        before = (row > imp) | ((row == imp) & (blk > i))
        rank = rank + jnp.where(before, 1.0, 0.0)
    sel_t = jnp.where(rank < float(min(N_SELECT, n_blk)), 1.0, 0.0)
    sel = jnp.concatenate([sel_t, jnp.zeros((LANES - n_blk, tq_n), _F32)], axis=0).T.astype(_BF16)

    ones_c = jnp.ones((SEL_CHUNK, HEAD_DIM), _BF16)

    def scores(c):
        return _dot_nt(q, ks_ref[0, 0, c * SEL_CHUNK:(c + 1) * SEL_CHUNK, :]).reshape(GROUP_SIZE, tq_n, SEL_CHUNK)

    m_run = None
    acc = None
    sc_next = scores(0)
    for c in range(n_active):
        lo = c * SEL_CHUNK
        sc_raw = sc_next
        if c + 1 < n_active:
            sc_next = scores(c + 1)
        chosen = _dot(sel, emat_ref[:, lo:lo + SEL_CHUNK]) > 0.5
        if c == n_active - 1:
            kpos = lo + lax.broadcasted_iota(jnp.int32, (1, SEL_CHUNK), 1)
            chosen = chosen & (kpos <= tq)
        sc = sc_raw + jnp.where(chosen, 0.0, NEG_INF)[None]
        m_c = jnp.max(sc, axis=-1, keepdims=True)
        m_new = m_c if c == 0 else jnp.maximum(m_run, m_c)
        e_c = jnp.exp2(sc - m_new).astype(_BF16)
        v_aug = jnp.concatenate([vs_ref[0, 0, lo:lo + SEL_CHUNK, :], ones_c], axis=1)
        pv = _dot(e_c.reshape(rows, SEL_CHUNK), v_aug).reshape(GROUP_SIZE, tq_n, 2 * HEAD_DIM)
        acc = pv if c == 0 else jnp.exp2(m_run - m_new) * acc + pv
        m_run = m_new

    o_cmp = o_cmp.reshape(GROUP_SIZE, tq_n, HEAD_DIM)
    gates = gate_ref[0]
    for r in range(GROUP_SIZE):
        g_cmp = gates[:, r:r + 1]
        g_slc = gates[:, GROUP_SIZE + r:GROUP_SIZE + r + 1] / acc[r, :, HEAD_DIM:]
        g_win = gates[:, 2 * GROUP_SIZE + r:2 * GROUP_SIZE + r + 1] / win[r, :, HEAD_DIM:]
        o = g_cmp * o_cmp[r] + g_slc * acc[r, :, :HEAD_DIM] + g_win * win[r, :, :HEAD_DIM]
        o_ref[:, r * HEAD_DIM:(r + 1) * HEAD_DIM] = o.astype(o_ref.dtype)


def _nsa_kernel(q_ref, kc_ref, vc_ref, ks_ref, vs_ref, kw_ref, vw_ref, gate_ref, amat_ref, emat_ref, o_ref):
    tq_n = q_ref.shape[2]
    seq = ks_ref.shape[2]
    n_active = (pl.program_id(2) * tq_n + tq_n - 1) // SEL_CHUNK + 1
    for k in range(1, seq // SEL_CHUNK + 1):
        @pl.when(n_active == k)
        def _(k=k):
            _nsa_step(k, q_ref, kc_ref, vc_ref, ks_ref, vs_ref, kw_ref, vw_ref, gate_ref, amat_ref, emat_ref,
                      o_ref)


def _nsa(heads, cmp_kv, gates, batch, seq):
    t = heads.shape[2]
    tq = Q_TILE
    nq = seq // tq
    n_chunk = cmp_kv.shape[3]
    n_blk = seq // SEL_BLOCK
    assert n_chunk == LANES and n_blk <= LANES and seq % SEL_CHUNK == 0 and WINDOW % tq == 0
    amat = jnp.asarray(_importance_matrix(n_chunk, n_blk))
    emat = jnp.asarray(_expand_matrix(n_blk, seq), dtype=_BF16)

    def kv_spec(slab):
        return pl.BlockSpec((1, 1, seq, HEAD_DIM), lambda b, g, i: (slab, g, b, 0))

    def cmp_spec(s):
        return pl.BlockSpec((1, 1, 1, n_chunk, HEAD_DIM), lambda b, g, i: (s, b, g, 0, 0))

    return pl.pallas_call(
        _nsa_kernel,
        out_shape=jax.ShapeDtypeStruct((t, Q_COLS), _BF16),
        grid=(batch, N_KV_GROUPS, nq),
        in_specs=[pl.BlockSpec((1, GROUP_SIZE, tq, HEAD_DIM), lambda b, g, i: (g, 0, b * nq + i, 0)),
                  cmp_spec(0), cmp_spec(1),
                  kv_spec(N_Q_SLABS), kv_spec(N_Q_SLABS + 1),
                  kv_spec(N_Q_SLABS + 2), kv_spec(N_Q_SLABS + 3),
                  pl.BlockSpec((1, tq, 3 * GROUP_SIZE), lambda b, g, i: (g, b * nq + i, 0)),
                  pl.BlockSpec((n_blk, n_chunk), lambda b, g, i: (0, 0)),
                  pl.BlockSpec((LANES, seq), lambda b, g, i: (0, 0))],
        out_specs=pl.BlockSpec((tq, HEAD_TILE), lambda b, g, i: (b * nq + i, g)),
        compiler_params=_params("parallel", "parallel", "arbitrary"),
        name="nsa",
    )(heads, cmp_kv, cmp_kv, heads, heads, heads, heads, gates, amat, emat)


def _mix_kernel(c_ref, o_ref, h_ref, wpc_ref, wpn_ref, wmc_ref, wmn_ref, bc_ref, bn_ref, out_ref):
    h = h_ref[...]
    g_conv = _sigmoid(_dot(h, wmc_ref[...]) + bc_ref[...])
    g_nsa = _sigmoid(_dot(h, wmn_ref[...]) + bn_ref[...])
    y_conv = _dot(c_ref[...], wpc_ref[...])
    y_nsa = _dot(o_ref[...], wpn_ref[...])
    out_ref[...] = (g_conv * y_conv + g_nsa * y_nsa).astype(out_ref.dtype)


def _mix(c, o, h, wpc, wpn, wm, bm):
    t, d = h.shape
    tm, tn = MIX_TILE, MIX_COLS
    nb = d // tn
    bm2 = bm.reshape(1, 2 * d)
    return pl.pallas_call(
        _mix_kernel,
        out_shape=jax.ShapeDtypeStruct((t, d), _BF16),
        grid=(t // tm, nb),
        in_specs=[pl.BlockSpec((tm, CONV_CHANNELS), lambda i, j: (i, 0)),
                  pl.BlockSpec((tm, Q_COLS), lambda i, j: (i, 0)),
                  pl.BlockSpec((tm, d), lambda i, j: (i, 0)),
                  pl.BlockSpec((CONV_CHANNELS, tn), lambda i, j: (0, j)),
                  pl.BlockSpec((Q_COLS, tn), lambda i, j: (0, j)),
                  pl.BlockSpec((d, tn), lambda i, j: (0, j)),
                  pl.BlockSpec((d, tn), lambda i, j: (0, j + nb)),
                  pl.BlockSpec((1, tn), lambda i, j: (0, j)),
                  pl.BlockSpec((1, tn), lambda i, j: (0, j + nb))],
        out_specs=pl.BlockSpec((tm, tn), lambda i, j: (i, j)),
        compiler_params=_params("parallel", "arbitrary"),
        name="mix",
    )(c, o, h, wpc, wpn, wm, wm, bm2, bm2)


def _out_router_kernel(mix_ref, x_ref, wout_ref, g_ref, wrh_ref, wrl_ref, br_ref, x1_ref, h2_ref, route_ref):
    x1 = x_ref[...] + _dot(mix_ref[...], wout_ref[...])
    x1_ref[...] = x1
    ms = jnp.mean(x1 * x1, axis=-1, keepdims=True)
    h2 = x1 * lax.rsqrt(ms + NORM_EPS) * g_ref[...]
    h2_ref[...] = h2
    h_hi = h2.astype(_BF16)
    h_lo = (h2 - h_hi.astype(_F32)).astype(_BF16)
    logits = (_dot(h_hi, wrh_ref[...]) + _dot(h_lo, wrh_ref[...]) + _dot(h_hi, wrl_ref[...])
              + br_ref[...])
    lane = lax.broadcasted_iota(jnp.int32, (1, LANES), 1)
    lane_f = lane.astype(_F32)
    big = float(LANES)
    gl = jnp.where(lane < N_GROUPS, logits, NEG_INF)
    gmax = jnp.max(gl, axis=-1, keepdims=True)
    p_g = 1.0 / jnp.sum(jnp.where(lane < N_GROUPS, jnp.exp(gl - gmax), 0.0), axis=-1, keepdims=True)
    g_sel = jnp.min(jnp.where(gl == gmax, lane_f, big), axis=-1, keepdims=True)
    lo = N_GROUPS + g_sel * EXPERTS_PER_GROUP
    in_grp = (lane_f >= lo) & (lane_f < lo + EXPERTS_PER_GROUP)
    el = jnp.where(in_grp, logits, NEG_INF)
    v0 = jnp.max(el, axis=-1, keepdims=True)
    i0 = jnp.min(jnp.where(el == v0, lane_f, big), axis=-1, keepdims=True)
    el2 = jnp.where(lane_f == i0, NEG_INF, el)
    v1 = jnp.max(el2, axis=-1, keepdims=True)
    i1 = jnp.min(jnp.where(el2 == v1, lane_f, big), axis=-1, keepdims=True)
    t1 = jnp.exp(v1 - v0)
    w0 = p_g / (1.0 + t1)
    w1 = p_g * t1 / (1.0 + t1)
    route = jnp.where(lane == 0, i0 - N_GROUPS, 0.0)
    route = jnp.where(lane == 1, i1 - N_GROUPS, route)
    route = jnp.where(lane == 2, w0, route)
    route = jnp.where(lane == 3, w1, route)
    route_ref[...] = route


def _out_router(mixed, x, w_out, ffn_g, w_route, b_route):
    t, d = x.shape
    tm = OUT_TILE
    row = pl.BlockSpec((tm, d), lambda i: (i, 0))
    w_route_hi = w_route.astype(_BF16)
    w_route_lo = (w_route - w_route_hi.astype(_F32)).astype(_BF16)
    return pl.pallas_call(
        _out_router_kernel,
        out_shape=(jax.ShapeDtypeStruct((t, d), _F32),
                   jax.ShapeDtypeStruct((t, d), _F32),
                   jax.ShapeDtypeStruct((t, LANES), _F32)),
        grid=(t // tm,),
        in_specs=[row, row,
                  pl.BlockSpec((d, d), lambda i: (0, 0), pipeline_mode=pl.Buffered(1)),
                  pl.BlockSpec((1, d), lambda i: (0, 0)),
                  pl.BlockSpec((d, LANES), lambda i: (0, 0)),
                  pl.BlockSpec((d, LANES), lambda i: (0, 0)),
                  pl.BlockSpec((1, LANES), lambda i: (0, 0))],
        out_specs=(row, row, pl.BlockSpec((tm, LANES), lambda i: (i, 0))),
        compiler_params=_params("parallel"),
        name="out_router",
    )(mixed, x, w_out, ffn_g.reshape(1, d), w_route_hi, w_route_lo, b_route)


def _moe_kernel(be_ref, half_ref, tok_ref, nb_ref, h2_hbm, w1_ref, w3_ref, w2_ref, y_ref, xbuf, w1b, w3b, w2b, sem):
    i = pl.program_id(0)
    nb = nb_ref[0]
    slot = i % 2
    bm = xbuf.shape[1]
    hb = bm // 2

    def row_copy(tok, r, sl):
        return pltpu.make_async_copy(h2_hbm.at[pl.ds(tok, 1), :], xbuf.at[sl, pl.ds(r, 1), :], sem.at[sl])

    def rows_wait(sl, n):
        pltpu.make_async_copy(h2_hbm.at[pl.ds(0, n), :], xbuf.at[sl, pl.ds(0, n), :], sem.at[sl]).wait()

    def ffn(x):
        a = _dot(x, w1b[...])
        b = _dot(x, w3b[...])
        act = (a * _sigmoid(a) * b).astype(_BF16)
        return _dot(act, w2b[...])

    @pl.when(i == 0)
    def _():
        def body(r, carry):
            row_copy(tok_ref[r], r, 0).start()
            return carry
        lax.fori_loop(0, jnp.where(half_ref[0] == 1, hb, bm), body, 0)

    for par in range(2):
        for n_rows, is_half in ((bm, 0), (hb, 1)):
            @pl.when((i + 1 < nb) & (slot == par) & (half_ref[jnp.minimum(i + 1, nb - 1)] == is_half))
            def _(par=par, n_rows=n_rows):
                base = (i + 1) * bm
                for r in range(n_rows):
                    row_copy(tok_ref[base + r], r, 1 - par).start()

    @pl.when(i < nb)
    def _():
        half = half_ref[i] == 1

        @pl.when(half)
        def _():
            rows_wait(slot, hb)

        @pl.when(jnp.logical_not(half))
        def _():
            rows_wait(slot, bm)

        changed = (i == 0) | (be_ref[i] != be_ref[jnp.maximum(i - 1, 0)])

        @pl.when(changed)
        def _():
            w1b[...] = w1_ref[0].astype(_BF16)
            w3b[...] = w3_ref[0].astype(_BF16)
            w2b[...] = w2_ref[0].astype(_BF16)

        @pl.when(half)
        def _():
            y_ref[:hb, :] = ffn(xbuf[slot, :hb, :].astype(_BF16))
            y_ref[hb:, :] = jnp.zeros((bm - hb, y_ref.shape[1]), y_ref.dtype)

        @pl.when(jnp.logical_not(half))
        def _():
            y_ref[...] = ffn(xbuf[slot].astype(_BF16))

    @pl.when(i >= nb)
    def _():
        y_ref[...] = jnp.zeros(y_ref.shape, y_ref.dtype)


def _moe(h2, tables, w1, w3, w2):
    t, d = h2.shape
    bm = MOE_ROWS
    block_e, block_half, row_tok, n_used = tables
    n_blocks = block_e.shape[0]
    ff = w1.shape[2]
    grid_spec = pltpu.PrefetchScalarGridSpec(
        num_scalar_prefetch=4,
        grid=(n_blocks,),
        in_specs=[pl.BlockSpec(memory_space=pl.ANY),
                  pl.BlockSpec((1, d, ff), lambda i, be, *_: (be[i], 0, 0)),
                  pl.BlockSpec((1, d, ff), lambda i, be, *_: (be[i], 0, 0)),
                  pl.BlockSpec((1, ff, d), lambda i, be, *_: (be[i], 0, 0))],
        out_specs=pl.BlockSpec((bm, d), lambda i, *_: (i, 0)),
        scratch_shapes=[pltpu.VMEM((2, bm, d), _F32),
                        pltpu.VMEM((d, ff), _BF16),
                        pltpu.VMEM((d, ff), _BF16),
                        pltpu.VMEM((ff, d), _BF16),
                        pltpu.SemaphoreType.DMA((2,))],
    )
    return pl.pallas_call(
        _moe_kernel,
        out_shape=jax.ShapeDtypeStruct((n_blocks * bm, d), _F32),
        grid_spec=grid_spec,
        compiler_params=_params("arbitrary", disable_bounds_checks=True),
        name="moe_experts",
    )(block_e, block_half, row_tok, n_used, h2, w1, w3, w2)


def _final_kernel(pos_ref, y_hbm, x1_ref, route_ref, g_ref, o_ref, ybuf, sem):
    i = pl.program_id(0)
    n = pl.num_programs(0)
    tm = x1_ref.shape[0]

    def row_copy(p, r, k, sl):
        return pltpu.make_async_copy(y_hbm.at[pl.ds(p, 1), :], ybuf.at[sl, k, pl.ds(r, 1), :], sem.at[sl])

    @pl.when(i == 0)
    def _():
        def body(r, carry):
            for k in range(TOP_K):
                row_copy(pos_ref[r * TOP_K + k], r, k, 0).start()
            return carry
        lax.fori_loop(0, tm, body, 0, unroll=DMA_UNROLL)

    def step(par, issue_next):
        if issue_next:
            base = (i + 1) * tm * TOP_K
            for r in range(tm):
                for k in range(TOP_K):
                    row_copy(pos_ref[base + r * TOP_K + k], r, k, 1 - par).start()
        for k in range(TOP_K):
            pltpu.make_async_copy(y_hbm.at[pl.ds(0, tm), :], ybuf.at[par, k], sem.at[par]).wait()
        route = route_ref[...]
        x = x1_ref[...] + route[:, 2:3] * ybuf[par, 0] + route[:, 3:4] * ybuf[par, 1]
        ms = jnp.mean(x * x, axis=-1, keepdims=True)
        o_ref[...] = x * lax.rsqrt(ms + NORM_EPS) * g_ref[...]

    for par in range(2):
        @pl.when((i % 2 == par) & (i + 1 < n))
        def _(par=par):
            step(par, True)

        @pl.when((i % 2 == par) & (i + 1 == n))
        def _(par=par):
            step(par, False)


def _final(pos, y_rows, x1, route, g):
    t, d = x1.shape
    tm = FINAL_TILE
    grid_spec = pltpu.PrefetchScalarGridSpec(
        num_scalar_prefetch=1,
        grid=(t // tm,),
        in_specs=[pl.BlockSpec(memory_space=pl.ANY),
                  pl.BlockSpec((tm, d), lambda i, pos: (i, 0)),
                  pl.BlockSpec((tm, LANES), lambda i, pos: (i, 0)),
                  pl.BlockSpec((1, d), lambda i, pos: (0, 0))],
        out_specs=pl.BlockSpec((tm, d), lambda i, pos: (i, 0)),
        scratch_shapes=[pltpu.VMEM((2, TOP_K, tm, d), _F32),
                        pltpu.SemaphoreType.DMA((2,))],
    )
    return pl.pallas_call(
        _final_kernel,
        out_shape=jax.ShapeDtypeStruct((t, d), _F32),
        grid_spec=grid_spec,
        compiler_params=_params("arbitrary", disable_bounds_checks=True),
        name="combine_norm",
    )(pos, y_rows, x1, route, g.reshape(1, d))


def _routing_tables(expert, n_tokens):
    bm = MOE_ROWS
    n_assign = n_tokens * TOP_K
    n_blocks = -(-(n_assign + N_EXPERTS * (bm - 1)) // bm)
    i32 = jnp.int32
    flat_e = expert.reshape(-1)
    experts = jnp.arange(N_EXPERTS, dtype=i32)
    assign = jnp.arange(n_assign, dtype=i32)
    e_sorted, order = lax.sort((flat_e, assign), num_keys=1)
    start = jnp.sum((e_sorted[:, None] < experts[None, :]).astype(i32), axis=0)
    counts = jnp.concatenate([start[1:], jnp.full((1,), n_assign, i32)]) - start
    padded = (counts + bm - 1) // bm * bm
    pend = jnp.cumsum(padded)
    pstart = pend - padded
    shift = pstart - start
    dshift = shift - jnp.concatenate([jnp.zeros((1,), i32), shift[:-1]])
    shift_sorted = jnp.sum(jnp.where(assign[:, None] >= start[None, :], dshift[None, :], 0), axis=1)
    _, pos = lax.sort((order, assign + shift_sorted), num_keys=1)
    block_row0 = jnp.arange(n_blocks, dtype=i32) * bm
    block_e = jnp.minimum(jnp.sum((pend[None, :] <= block_row0[:, None]).astype(i32), axis=1), N_EXPERTS - 1)
    onehot = block_e[:, None] == experts[None, :]

    def per_block(table):
        return jnp.sum(jnp.where(onehot, table[None, :], 0), axis=1)

    block_start = per_block(start - pstart) + block_row0
    block_end = per_block(start + counts)
    src = block_start[:, None] + jnp.arange(bm, dtype=i32)[None, :]
    valid = src < block_end[:, None]
    tok_sorted = order // TOP_K
    idx = jnp.clip(src, 0, n_assign - 1)
    hb = bm // 2
    looked_up = jnp.concatenate([tok_sorted[idx[:, :hb]], tok_sorted[idx[:, hb:]]], axis=1)
    row_tok = jnp.where(valid, looked_up, 0)
    block_half = (block_end - block_start <= bm // 2).astype(i32)
    n_used = (pend[-1] // bm).astype(i32).reshape(1)
    return (block_e, block_half, row_tok.reshape(-1).astype(i32), n_used), pos.astype(i32)


def kernel(x, attn_norm, w_in, conv_dw, conv_dw_b, conv_ln_g, conv_ln_b, cmp_pos_k, cmp_k_w1, cmp_k_w2, cmp_pos_v, cmp_v_w1, cmp_v_w2, w_proj_conv, w_proj_nsa, w_merge, b_merge, w_out, ffn_norm, w_grp, b_grp, w_exp, b_exp, exp_w1, exp_w3, exp_w2, final_norm):
    batch, seq, d = x.shape
    t = batch * seq
    assert w_in.shape[0] == 1, "the block is specified with a single layer"
    xf = x.reshape(t, d)

    w_in_b = w_in[0].astype(_BF16)
    w_gate = w_in[0][:, 2 * CONV_CHANNELS + HEAD_COLS:].reshape(d, 3, N_KV_GROUPS, GROUP_SIZE)
    w_gate = w_gate.transpose(0, 2, 1, 3).reshape(d, GATE_COLS).astype(_BF16)

    h = _rmsnorm(xf, attn_norm[0], _BF16)
    glu = _glu_proj(h, w_in_b)
    tables = _rope_tables(seq)
    heads = _heads_proj(h, w_in_b, tables, seq)
    cmp_flat = _cmp_proj(h, w_in_b, tables, seq)
    gates = _gate_proj(h, w_gate).reshape(t, N_KV_GROUPS, 3 * GROUP_SIZE).transpose(1, 0, 2)

    half = CMP_STRIDE * HEAD_DIM
    pos = jnp.stack([cmp_pos_k[0], cmp_pos_v[0]]).reshape(2, 2, 1, half)
    cw1 = jnp.stack([cmp_k_w1[0], cmp_v_w1[0]]).astype(_BF16)
    cw2 = jnp.stack([cmp_k_w2[0], cmp_v_w2[0]]).astype(_BF16)
    cmp_kv = _compress(cmp_flat, pos, cw1, cw2, batch, seq)

    c = _conv(glu, conv_dw[0], conv_dw_b[0], conv_ln_g[0], conv_ln_b[0], batch, seq)
    o = _nsa(heads, cmp_kv, gates, batch, seq)
    mixed = _mix(c, o, h, w_proj_conv[0].astype(_BF16), w_proj_nsa[0].astype(_BF16),
                 w_merge[0].astype(_BF16), b_merge[0])

    w_route = jnp.concatenate(
        [w_grp[0], w_exp[0], jnp.zeros((d, LANES - N_GROUPS - N_EXPERTS), _F32)], axis=1)
    b_route = jnp.concatenate(
        [b_grp[0], b_exp[0], jnp.zeros((LANES - N_GROUPS - N_EXPERTS,), _F32)]).reshape(1, LANES)
    x1, h2, route = _out_router(mixed, xf, w_out[0].astype(_BF16), ffn_norm[0], w_route, b_route)

    expert = route[:, :TOP_K].astype(jnp.int32)
    moe_tables, pos_rows = _routing_tables(expert, t)
    y_rows = _moe(h2, moe_tables, exp_w1[0], exp_w3[0], exp_w2[0])
    out = _final(pos_rows, y_rows, x1, route, final_norm)
    return out.reshape(batch, seq, d)
```

```python
import numpy as np
import jax
import jax.numpy as jnp
from jax import lax
from jax.experimental import pallas as pl
from jax.experimental.pallas import tpu as pltpu

D_MODEL = 2048
CONV_CHANNELS = D_MODEL // 2
DW_TAPS = 31
N_HEADS = 16
HEAD_DIM = 128
N_KV_GROUPS = 4
GROUP_SIZE = N_HEADS // N_KV_GROUPS
ROT_DIM = HEAD_DIM // 4
ROPE_THETA = 500000.0
CMP_BLOCK = 32
CMP_STRIDE = 16
CMP_HIDDEN = 256
SEL_BLOCK = 64
N_SELECT = 16
N_FORCED_LOCAL = 2
WINDOW = 512
N_GROUPS = 8
EXPERTS_PER_GROUP = 8
N_EXPERTS = N_GROUPS * EXPERTS_PER_GROUP
TOP_K = 2
EXPERT_FF = 512
Q_COLS = N_HEADS * HEAD_DIM
KV_COLS = N_KV_GROUPS * HEAD_DIM
HEAD_COLS = Q_COLS + 6 * KV_COLS
GATE_COLS = 3 * N_HEADS
NORM_EPS = 1e-6
NEG_INF = -1e30
FORCE_BONUS = 1e4
LOG2_E = 1.4426950408889634

LANES = 128
SUBLANES = 8
VMEM_LIMIT_BYTES = 56 * 1024 * 1024

ROW_TILE = 512
PROJ_TILE = 2048
MIX_TILE = 1024
MIX_COLS = 256
HEAD_TILE = GROUP_SIZE * HEAD_DIM
Q_TILE = 256
SEL_CHUNK = 512
CONV_TILE = 256
CONV_HALO = 32
CONV_ROWS = 16
OUT_TILE = 512
MOE_ROWS = 512
FINAL_TILE = 512
DMA_UNROLL = 8

_BF16 = jnp.bfloat16
_F32 = jnp.float32


def _params(*sem, **kw):
    return pltpu.CompilerParams(dimension_semantics=sem, vmem_limit_bytes=VMEM_LIMIT_BYTES, **kw)


def _dot(a, b):
    return jnp.dot(a, b, preferred_element_type=_F32)


def _dot_nt(a, b):
    return lax.dot_general(a, b, (((1,), (1,)), ((), ())), preferred_element_type=_F32)


def _sigmoid(x):
    return 1.0 / (1.0 + jnp.exp(-x))


def _rmsnorm_kernel(x_ref, g_ref, o_ref):
    x = x_ref[...]
    ms = jnp.mean(x * x, axis=-1, keepdims=True)
    o_ref[...] = (x * lax.rsqrt(ms + NORM_EPS) * g_ref[...]).astype(o_ref.dtype)


def _rmsnorm(x, g, out_dtype):
    t, d = x.shape
    return pl.pallas_call(
        _rmsnorm_kernel,
        out_shape=jax.ShapeDtypeStruct((t, d), out_dtype),
        grid=(t // ROW_TILE,),
        in_specs=[pl.BlockSpec((ROW_TILE, d), lambda i: (i, 0)),
                  pl.BlockSpec((1, d), lambda i: (0, 0))],
        out_specs=pl.BlockSpec((ROW_TILE, d), lambda i: (i, 0)),
        compiler_params=_params("parallel"),
        name="rmsnorm",
    )(x, g.reshape(1, d))


def _glu_kernel(h_ref, wa_ref, wb_ref, o_ref):
    h = h_ref[...]
    a = _dot(h, wa_ref[...])
    b = _dot(h, wb_ref[...])
    o_ref[...] = a * _sigmoid(b)


def _glu_proj(h, w_in_bf16):
    t, d = h.shape
    tn = HEAD_TILE
    nb = CONV_CHANNELS // tn
    return pl.pallas_call(
        _glu_kernel,
        out_shape=jax.ShapeDtypeStruct((t, CONV_CHANNELS), _F32),
        grid=(t // PROJ_TILE, nb),
        in_specs=[pl.BlockSpec((PROJ_TILE, d), lambda i, j: (i, 0)),
                  pl.BlockSpec((d, tn), lambda i, j: (0, j)),
                  pl.BlockSpec((d, tn), lambda i, j: (0, j + nb))],
        out_specs=pl.BlockSpec((PROJ_TILE, tn), lambda i, j: (i, j)),
        compiler_params=_params("parallel", "arbitrary"),
        name="glu_proj",
    )(h, w_in_bf16, w_in_bf16)


N_Q_SLABS = Q_COLS // HEAD_TILE
N_HEAD_SLABS = N_Q_SLABS + 4
Q_COL_BLOCK = 2 * CONV_CHANNELS // HEAD_TILE
CMP_COL_BLOCK = Q_COL_BLOCK + N_Q_SLABS
KV_COL_BLOCK = CMP_COL_BLOCK + 2


def _rope_tables(seq):
    pos = jnp.arange(seq, dtype=_F32)
    inv = ROPE_THETA ** (-jnp.arange(0, ROT_DIM, 2, dtype=_F32) / ROT_DIM)
    ang = pos[:, None] * inv[None, :]
    cos, sin = jnp.cos(ang), jnp.sin(ang)
    half = ROT_DIM // 2
    ones = jnp.ones((seq, HEAD_DIM - ROT_DIM), _F32)
    zeros_h = jnp.zeros((seq, half), _F32)
    zeros_r = jnp.zeros((seq, HEAD_DIM - ROT_DIM), _F32)
    c = jnp.concatenate([cos, cos, ones], axis=-1)
    s_lo = jnp.concatenate([zeros_h, sin, zeros_r], axis=-1)
    s_hi = jnp.concatenate([-sin, zeros_h, zeros_r], axis=-1)
    q_scale = HEAD_DIM ** -0.5 * LOG2_E
    zeros = jnp.zeros_like(c)
    return (jnp.stack([c * q_scale, c, jnp.ones_like(c)]),
            jnp.stack([s_lo * q_scale, s_lo, zeros]),
            jnp.stack([s_hi * q_scale, s_hi, zeros]))


ROPE_Q, ROPE_K, ROPE_NONE = 0, 1, 2


def _rotate_head(xs, c, slo, shi):
    half = ROT_DIM // 2
    return xs * c + pltpu.roll(xs, half, 1) * slo + pltpu.roll(xs, HEAD_DIM - half, 1) * shi


def _heads_kernel(h_ref, w_ref, c_ref, slo_ref, shi_ref, o_ref, acc_ref):
    s = pl.program_id(0)

    @pl.when(s == 0)
    def _():
        acc_ref[...] = jnp.zeros(acc_ref.shape, acc_ref.dtype)

    for par in range(2):
        @pl.when(s % 2 == par)
        def _(par=par):
            acc_ref[par] = _dot(h_ref[...], w_ref[...])
            c, slo, shi = c_ref[0], slo_ref[0], shi_ref[0]
            for hd in range(GROUP_SIZE):
                r = _rotate_head(acc_ref[1 - par, :, hd * HEAD_DIM:(hd + 1) * HEAD_DIM], c, slo, shi)
                o_ref[0, hd] = r.astype(o_ref.dtype)


def _heads_proj(h, w_in_bf16, tables, seq):
    t, d = h.shape
    tm = PROJ_TILE
    spt = seq // tm
    n_slabs = N_HEAD_SLABS
    last = (t // tm) * n_slabs - 1

    def cur(s):
        s = jnp.minimum(s, last)
        return s // n_slabs, s % n_slabs

    def prev(s):
        s = jnp.maximum(s - 1, 0)
        return s // n_slabs, s % n_slabs

    def variant(j):
        return jnp.where(j < N_Q_SLABS, ROPE_Q, jnp.where(j % 2 == 0, ROPE_K, ROPE_NONE))

    def w_block(s):
        j = cur(s)[1]
        return (0, jnp.where(j < N_Q_SLABS, j + Q_COL_BLOCK, j - N_Q_SLABS + KV_COL_BLOCK))

    tab = pl.BlockSpec((1, tm, HEAD_DIM), lambda s: (variant(prev(s)[1]), prev(s)[0] % spt, 0))
    return pl.pallas_call(
        _heads_kernel,
        out_shape=jax.ShapeDtypeStruct((n_slabs, GROUP_SIZE, t, HEAD_DIM), _BF16),
        grid=(last + 2,),
        in_specs=[pl.BlockSpec((tm, d), lambda s: (cur(s)[0], 0)),
                  pl.BlockSpec((d, HEAD_TILE), w_block),
                  tab, tab, tab],
        out_specs=pl.BlockSpec((1, GROUP_SIZE, tm, HEAD_DIM), lambda s: (prev(s)[1], 0, prev(s)[0], 0)),
        scratch_shapes=[pltpu.VMEM((2, tm, HEAD_TILE), _F32)],
        compiler_params=_params("arbitrary"),
        name="heads_proj",
    )(h, w_in_bf16, *tables)


def _cmp_proj_kernel(h_ref, w_ref, c_ref, slo_ref, shi_ref, o_ref, stage_ref):
    acc = _dot(h_ref[...], w_ref[...])
    c, slo, shi = c_ref[0], slo_ref[0], shi_ref[0]
    for g in range(N_KV_GROUPS):
        stage_ref[g] = _rotate_head(acc[:, g * HEAD_DIM:(g + 1) * HEAD_DIM], c, slo, shi)

    n_chunk = stage_ref.shape[1] // CMP_STRIDE
    for g in range(N_KV_GROUPS):
        for l in range(CMP_STRIDE):
            rows = stage_ref[g, pl.ds(l, n_chunk, stride=CMP_STRIDE), :]
            o_ref[0, g, :, l * HEAD_DIM:(l + 1) * HEAD_DIM] = rows.astype(o_ref.dtype)


def _cmp_proj(h, w_in_bf16, tables, seq):
    t, d = h.shape
    tm = PROJ_TILE
    spt = seq // tm
    tab = pl.BlockSpec((1, tm, HEAD_DIM), lambda i, j: (jnp.where(j == 0, ROPE_K, ROPE_NONE), i % spt, 0))
    return pl.pallas_call(
        _cmp_proj_kernel,
        out_shape=jax.ShapeDtypeStruct((2, N_KV_GROUPS, t // CMP_STRIDE, CMP_STRIDE * HEAD_DIM), _BF16),
        grid=(t // tm, 2),
        in_specs=[pl.BlockSpec((tm, d), lambda i, j: (i, 0)),
                  pl.BlockSpec((d, HEAD_TILE), lambda i, j: (0, j + CMP_COL_BLOCK)),
                  tab, tab, tab],
        out_specs=pl.BlockSpec((1, N_KV_GROUPS, tm // CMP_STRIDE, CMP_STRIDE * HEAD_DIM),
                               lambda i, j: (j, 0, i, 0)),
        scratch_shapes=[pltpu.VMEM((N_KV_GROUPS, tm, HEAD_DIM), _F32)],
        compiler_params=_params("parallel", "arbitrary"),
        name="cmp_proj",
    )(h, w_in_bf16, *tables)


def _gate_kernel(h_ref, w_ref, o_ref):
    o_ref[...] = _sigmoid(_dot(h_ref[...], w_ref[...]))


def _gate_proj(h, w_gate_bf16):
    t, d = h.shape
    n = w_gate_bf16.shape[1]
    return pl.pallas_call(
        _gate_kernel,
        out_shape=jax.ShapeDtypeStruct((t, n), _F32),
        grid=(t // ROW_TILE,),
        in_specs=[pl.BlockSpec((ROW_TILE, d), lambda i: (i, 0)),
                  pl.BlockSpec((d, n), lambda i: (0, 0))],
        out_specs=pl.BlockSpec((ROW_TILE, n), lambda i: (i, 0)),
        compiler_params=_params("parallel"),
        name="gate_proj",
    )(h, w_gate_bf16)


def _compress_kernel(kv_ref, pos_ref, w1_ref, w2_ref, o_ref):
    half = CMP_STRIDE * HEAD_DIM
    c = kv_ref[0, 0].astype(_F32)
    n_chunk = c.shape[0]
    top = (c + pos_ref[0, 0]).astype(_BF16)
    bot = (c + pos_ref[0, 1]).astype(_BF16)
    u = _dot(top, w1_ref[0, :half, :])
    v = _dot(bot, w1_ref[0, half:, :])
    hidden = u + pltpu.roll(v, n_chunk - 1, 0)
    act = jax.nn.gelu(hidden).astype(_BF16)
    o_ref[0, 0, 0] = _dot(act, w2_ref[0]).astype(o_ref.dtype)


def _compress(cmp_flat, pos, w1, w2, batch, seq):
    n_chunk = seq // CMP_STRIDE
    return pl.pallas_call(
        _compress_kernel,
        out_shape=jax.ShapeDtypeStruct((2, batch, N_KV_GROUPS, n_chunk, HEAD_DIM), _BF16),
        grid=(2, batch, N_KV_GROUPS),
        in_specs=[pl.BlockSpec((1, 1, n_chunk, CMP_STRIDE * HEAD_DIM), lambda s, b, g: (s, g, b, 0)),
                  pl.BlockSpec((1, 2, 1, CMP_STRIDE * HEAD_DIM), lambda s, b, g: (s, 0, 0, 0)),
                  pl.BlockSpec((1, CMP_BLOCK * HEAD_DIM, CMP_HIDDEN), lambda s, b, g: (s, 0, 0)),
                  pl.BlockSpec((1, CMP_HIDDEN, HEAD_DIM), lambda s, b, g: (s, 0, 0))],
        out_specs=pl.BlockSpec((1, 1, 1, n_chunk, HEAD_DIM), lambda s, b, g: (s, b, g, 0, 0)),
        compiler_params=_params("parallel", "parallel", "parallel"),
        name="compress",
    )(cmp_flat, pos, w1, w2)


def _conv_kernel(cur_ref, halo_ref, dw_ref, dwb_ref, g_ref, b_ref, o_ref, ext_ref, sh_ref, acc_ref):
    i = pl.program_id(1)
    ts = cur_ref.shape[0]
    ext_ref[:CONV_HALO, :] = jnp.where(i > 0, halo_ref[...], 0.0)
    ext_ref[CONV_HALO:, :] = cur_ref[...]
    n_sh = sh_ref.shape[1]
    for b in range(1, SUBLANES):
        sh_ref[b - 1] = ext_ref[b:b + n_sh, :]
    off = CONV_HALO - (DW_TAPS - 1)

    def chunk(c, carry):
        r0 = pl.multiple_of(c * CONV_ROWS, CONV_ROWS)
        y = None
        for k in range(DW_TAPS):
            a, b = divmod(off + k, SUBLANES)
            src = ext_ref if b == 0 else sh_ref.at[b - 1]
            term = src[pl.ds(r0 + SUBLANES * a, CONV_ROWS), :] * dw_ref[k:k + 1, :]
            y = term if y is None else y + term
        acc_ref[pl.ds(r0, CONV_ROWS), :] = y
        return carry

    lax.fori_loop(0, ts // CONV_ROWS, chunk, 0)
    y = acc_ref[...] + dwb_ref[...]
    mu = jnp.mean(y, axis=-1, keepdims=True)
    yc = y - mu
    var = jnp.mean(yc * yc, axis=-1, keepdims=True)
    z = yc * lax.rsqrt(var + NORM_EPS) * g_ref[...] + b_ref[...]
    o_ref[...] = (z * _sigmoid(z)).astype(o_ref.dtype)


def _conv(glu, dw, dw_b, ln_g, ln_b, batch, seq):
    t, c = glu.shape
    ts = CONV_TILE
    nt = seq // ts
    hb = ts // CONV_HALO
    dw_pad = jnp.concatenate([dw, jnp.zeros((CONV_HALO - DW_TAPS, c), _F32)], axis=0)
    vec = pl.BlockSpec((1, c), lambda b, i: (0, 0))
    return pl.pallas_call(
        _conv_kernel,
        out_shape=jax.ShapeDtypeStruct((t, c), _BF16),
        grid=(batch, nt),
        in_specs=[pl.BlockSpec((ts, c), lambda b, i: (b * nt + i, 0)),
                  pl.BlockSpec((CONV_HALO, c), lambda b, i: (jnp.maximum((b * nt + i) * hb - 1, 0), 0)),
                  pl.BlockSpec((CONV_HALO, c), lambda b, i: (0, 0)),
                  vec, vec, vec],
        out_specs=pl.BlockSpec((ts, c), lambda b, i: (b * nt + i, 0)),
        scratch_shapes=[pltpu.VMEM((ts + CONV_HALO, c), _F32),
                        pltpu.VMEM((SUBLANES - 1, ts + CONV_HALO - SUBLANES, c), _F32),
                        pltpu.VMEM((ts, c), _F32)],
        compiler_params=_params("parallel", "arbitrary"),
        name="conv",
    )(glu, glu, dw_pad, dw_b.reshape(1, c), ln_g.reshape(1, c), ln_b.reshape(1, c))


def _importance_matrix(n_cmp_pad, n_blk):
    r_sel, r_cmp = SEL_BLOCK // CMP_STRIDE, CMP_BLOCK // CMP_STRIDE
    a = np.zeros((n_blk, n_cmp_pad), np.float32)
    for j in range(n_blk):
        for m in range(r_sel):
            for n in range(r_cmp):
                c = r_sel * j + m - n
                if 0 <= c < n_cmp_pad - 1:
                    a[j, c] += 1.0
    return a


def _expand_matrix(n_blk, seq):
    e = np.zeros((LANES, seq), np.float32)
    for j in range(n_blk):
        e[j, j * SEL_BLOCK:(j + 1) * SEL_BLOCK] = 1.0
    return e


def _nsa_step(n_active, q_ref, kc_ref, vc_ref, ks_ref, vs_ref, kw_ref, vw_ref, gate_ref, amat_ref, emat_ref,
              o_ref):
    tq_n = q_ref.shape[2]
    n_cmp_pad = kc_ref.shape[3]
    n_blk = amat_ref.shape[0]
    rows = GROUP_SIZE * tq_n
    start = pl.program_id(2) * tq_n
    q = q_ref[0].reshape(rows, HEAD_DIM)
    tq = start + lax.broadcasted_iota(jnp.int32, (tq_n, 1), 0)

    span = WINDOW + tq_n
    kstart = pl.multiple_of(jnp.maximum(start - WINDOW, 0), tq_n)
    kpos = kstart + lax.broadcasted_iota(jnp.int32, (1, span), 1)
    wbias = jnp.where((kpos <= tq) & (tq - kpos < WINDOW), 0.0, NEG_INF)
    sw = _dot_nt(q, kw_ref[0, 0, pl.ds(kstart, span), :]).reshape(GROUP_SIZE, tq_n, span) + wbias[None]

    cidx = lax.broadcasted_iota(jnp.int32, (1, n_cmp_pad), 1)
    cvalid = (cidx * CMP_STRIDE + (CMP_BLOCK - 1) <= tq) & (cidx < n_cmp_pad - 1)
    cbias = jnp.where(cvalid, 0.0, NEG_INF)
    s = _dot_nt(q, kc_ref[0, 0, 0]).reshape(GROUP_SIZE, tq_n, n_cmp_pad) + cbias[None]
    e_w = jnp.exp2(sw - jnp.max(sw, axis=-1, keepdims=True)).astype(_BF16)
    e = jnp.exp2(s - jnp.max(s, axis=-1, keepdims=True))
    inv_l = jnp.where((tq >= CMP_BLOCK - 1)[None], 1.0 / jnp.sum(e, axis=-1, keepdims=True), 0.0)
    p = e * inv_l
    vw_aug = jnp.concatenate([vw_ref[0, 0, pl.ds(kstart, span), :], jnp.ones((span, HEAD_DIM), _BF16)], axis=1)
    win = _dot(e_w.reshape(rows, span), vw_aug).reshape(GROUP_SIZE, tq_n, 2 * HEAD_DIM)
    o_cmp = _dot(p.reshape(rows, n_cmp_pad).astype(_BF16), vc_ref[0, 0, 0])

    imp = lax.dot_general(amat_ref[...], jnp.sum(p, axis=0), (((1,), (1,)), ((), ())),
                          preferred_element_type=_F32, precision=lax.Precision.HIGHEST)
    tq_row = start + lax.broadcasted_iota(jnp.int32, (1, tq_n), 1)
    blk = lax.broadcasted_iota(jnp.int32, (n_blk, 1), 0)
    cur = tq_row // SEL_BLOCK
    forced = (blk == 0) | ((blk <= cur) & (blk > cur - N_FORCED_LOCAL))
    imp = jnp.where(forced, imp + FORCE_BONUS, imp)
    imp = jnp.where(blk * SEL_BLOCK <= tq_row, imp, NEG_INF)
    rank = jnp.zeros((n_blk, tq_n), _F32)
    for i in range(n_blk):
        row = imp[i:i + 1, :]
        before = (row > imp) | ((row == imp) & (blk > i))
        rank = rank + jnp.where(before, 1.0, 0.0)
    sel_t = jnp.where(rank < float(min(N_SELECT, n_blk)), 1.0, 0.0)
    sel = jnp.concatenate([sel_t, jnp.zeros((LANES - n_blk, tq_n), _F32)], axis=0).T.astype(_BF16)

    ones_c = jnp.ones((SEL_CHUNK, HEAD_DIM), _BF16)

    def scores(c):
        return _dot_nt(q, ks_ref[0, 0, c * SEL_CHUNK:(c + 1) * SEL_CHUNK, :]).reshape(GROUP_SIZE, tq_n, SEL_CHUNK)

    m_run = None
    acc = None
    sc_next = scores(0)
    for c in range(n_active):
        lo = c * SEL_CHUNK
        sc_raw = sc_next
        if c + 1 < n_active:
            sc_next = scores(c + 1)
        chosen = _dot(sel, emat_ref[:, lo:lo + SEL_CHUNK]) > 0.5
        if c == n_active - 1:
            kpos = lo + lax.broadcasted_iota(jnp.int32, (1, SEL_CHUNK), 1)
            chosen = chosen & (kpos <= tq)
        sc = sc_raw + jnp.where(chosen, 0.0, NEG_INF)[None]
        m_c = jnp.max(sc, axis=-1, keepdims=True)
        m_new = m_c if c == 0 else jnp.maximum(m_run, m_c)
        e_c = jnp.exp2(sc - m_new).astype(_BF16)
        v_aug = jnp.concatenate([vs_ref[0, 0, lo:lo + SEL_CHUNK, :], ones_c], axis=1)
        pv = _dot(e_c.reshape(rows, SEL_CHUNK), v_aug).reshape(GROUP_SIZE, tq_n, 2 * HEAD_DIM)
        acc = pv if c == 0 else jnp.exp2(m_run - m_new) * acc + pv
        m_run = m_new

    o_cmp = o_cmp.reshape(GROUP_SIZE, tq_n, HEAD_DIM)
    gates = gate_ref[0]
    for r in range(GROUP_SIZE):
        g_cmp = gates[:, r:r + 1]
        g_slc = gates[:, GROUP_SIZE + r:GROUP_SIZE + r + 1] / acc[r, :, HEAD_DIM:]
        g_win = gates[:, 2 * GROUP_SIZE + r:2 * GROUP_SIZE + r + 1] / win[r, :, HEAD_DIM:]
        o = g_cmp * o_cmp[r] + g_slc * acc[r, :, :HEAD_DIM] + g_win * win[r, :, :HEAD_DIM]
        o_ref[:, r * HEAD_DIM:(r + 1) * HEAD_DIM] = o.astype(o_ref.dtype)


def _nsa_kernel(q_ref, kc_ref, vc_ref, ks_ref, vs_ref, kw_ref, vw_ref, gate_ref, amat_ref, emat_ref, o_ref):
    tq_n = q_ref.shape[2]
    seq = ks_ref.shape[2]
    n_active = (pl.program_id(2) * tq_n + tq_n - 1) // SEL_CHUNK + 1
    for k in range(1, seq // SEL_CHUNK + 1):
        @pl.when(n_active == k)
        def _(k=k):
            _nsa_step(k, q_ref, kc_ref, vc_ref, ks_ref, vs_ref, kw_ref, vw_ref, gate_ref, amat_ref, emat_ref,
                      o_ref)


def _nsa(heads, cmp_kv, gates, batch, seq):
    t = heads.shape[2]
    tq = Q_TILE
    nq = seq // tq
    n_chunk = cmp_kv.shape[3]
    n_blk = seq // SEL_BLOCK
    assert n_chunk == LANES and n_blk <= LANES and seq % SEL_CHUNK == 0 and WINDOW % tq == 0
    amat = jnp.asarray(_importance_matrix(n_chunk, n_blk))
    emat = jnp.asarray(_expand_matrix(n_blk, seq), dtype=_BF16)

    def kv_spec(slab):
        return pl.BlockSpec((1, 1, seq, HEAD_DIM), lambda b, g, i: (slab, g, b, 0))

    def cmp_spec(s):
        return pl.BlockSpec((1, 1, 1, n_chunk, HEAD_DIM), lambda b, g, i: (s, b, g, 0, 0))

    return pl.pallas_call(
        _nsa_kernel,
        out_shape=jax.ShapeDtypeStruct((t, Q_COLS), _BF16),
        grid=(batch, N_KV_GROUPS, nq),
        in_specs=[pl.BlockSpec((1, GROUP_SIZE, tq, HEAD_DIM), lambda b, g, i: (g, 0, b * nq + i, 0)),
                  cmp_spec(0), cmp_spec(1),
                  kv_spec(N_Q_SLABS), kv_spec(N_Q_SLABS + 1),
                  kv_spec(N_Q_SLABS + 2), kv_spec(N_Q_SLABS + 3),
                  pl.BlockSpec((1, tq, 3 * GROUP_SIZE), lambda b, g, i: (g, b * nq + i, 0)),
                  pl.BlockSpec((n_blk, n_chunk), lambda b, g, i: (0, 0)),
                  pl.BlockSpec((LANES, seq), lambda b, g, i: (0, 0))],
        out_specs=pl.BlockSpec((tq, HEAD_TILE), lambda b, g, i: (b * nq + i, g)),
        compiler_params=_params("parallel", "parallel", "arbitrary"),
        name="nsa",
    )(heads, cmp_kv, cmp_kv, heads, heads, heads, heads, gates, amat, emat)


def _mix_kernel(c_ref, o_ref, h_ref, wpc_ref, wpn_ref, wmc_ref, wmn_ref, bc_ref, bn_ref, out_ref):
    h = h_ref[...]
    g_conv = _sigmoid(_dot(h, wmc_ref[...]) + bc_ref[...])
    g_nsa = _sigmoid(_dot(h, wmn_ref[...]) + bn_ref[...])
    y_conv = _dot(c_ref[...], wpc_ref[...])
    y_nsa = _dot(o_ref[...], wpn_ref[...])
    out_ref[...] = (g_conv * y_conv + g_nsa * y_nsa).astype(out_ref.dtype)


def _mix(c, o, h, wpc, wpn, wm, bm):
    t, d = h.shape
    tm, tn = MIX_TILE, MIX_COLS
    nb = d // tn
    bm2 = bm.reshape(1, 2 * d)
    return pl.pallas_call(
        _mix_kernel,
        out_shape=jax.ShapeDtypeStruct((t, d), _BF16),
        grid=(t // tm, nb),
        in_specs=[pl.BlockSpec((tm, CONV_CHANNELS), lambda i, j: (i, 0)),
                  pl.BlockSpec((tm, Q_COLS), lambda i, j: (i, 0)),
                  pl.BlockSpec((tm, d), lambda i, j: (i, 0)),
                  pl.BlockSpec((CONV_CHANNELS, tn), lambda i, j: (0, j)),
                  pl.BlockSpec((Q_COLS, tn), lambda i, j: (0, j)),
                  pl.BlockSpec((d, tn), lambda i, j: (0, j)),
                  pl.BlockSpec((d, tn), lambda i, j: (0, j + nb)),
                  pl.BlockSpec((1, tn), lambda i, j: (0, j)),
                  pl.BlockSpec((1, tn), lambda i, j: (0, j + nb))],
        out_specs=pl.BlockSpec((tm, tn), lambda i, j: (i, j)),
        compiler_params=_params("parallel", "arbitrary"),
        name="mix",
    )(c, o, h, wpc, wpn, wm, wm, bm2, bm2)


def _out_router_kernel(mix_ref, x_ref, wout_ref, g_ref, wrh_ref, wrl_ref, br_ref, x1_ref, h2_ref, route_ref):
    x1 = x_ref[...] + _dot(mix_ref[...], wout_ref[...])
    x1_ref[...] = x1
    ms = jnp.mean(x1 * x1, axis=-1, keepdims=True)
    h2 = x1 * lax.rsqrt(ms + NORM_EPS) * g_ref[...]
    h2_ref[...] = h2
    h_hi = h2.astype(_BF16)
    h_lo = (h2 - h_hi.astype(_F32)).astype(_BF16)
    logits = (_dot(h_hi, wrh_ref[...]) + _dot(h_lo, wrh_ref[...]) + _dot(h_hi, wrl_ref[...])
              + br_ref[...])
    lane = lax.broadcasted_iota(jnp.int32, (1, LANES), 1)
    lane_f = lane.astype(_F32)
    big = float(LANES)
    gl = jnp.where(lane < N_GROUPS, logits, NEG_INF)
    gmax = jnp.max(gl, axis=-1, keepdims=True)
    p_g = 1.0 / jnp.sum(jnp.where(lane < N_GROUPS, jnp.exp(gl - gmax), 0.0), axis=-1, keepdims=True)
    g_sel = jnp.min(jnp.where(gl == gmax, lane_f, big), axis=-1, keepdims=True)
    lo = N_GROUPS + g_sel * EXPERTS_PER_GROUP
    in_grp = (lane_f >= lo) & (lane_f < lo + EXPERTS_PER_GROUP)
    el = jnp.where(in_grp, logits, NEG_INF)
    v0 = jnp.max(el, axis=-1, keepdims=True)
    i0 = jnp.min(jnp.where(el == v0, lane_f, big), axis=-1, keepdims=True)
    el2 = jnp.where(lane_f == i0, NEG_INF, el)
    v1 = jnp.max(el2, axis=-1, keepdims=True)
    i1 = jnp.min(jnp.where(el2 == v1, lane_f, big), axis=-1, keepdims=True)
    t1 = jnp.exp(v1 - v0)
    w0 = p_g / (1.0 + t1)
    w1 = p_g * t1 / (1.0 + t1)
    route = jnp.where(lane == 0, i0 - N_GROUPS, 0.0)
    route = jnp.where(lane == 1, i1 - N_GROUPS, route)
    route = jnp.where(lane == 2, w0, route)
    route = jnp.where(lane == 3, w1, route)
    route_ref[...] = route


def _out_router(mixed, x, w_out, ffn_g, w_route, b_route):
    t, d = x.shape
    tm = OUT_TILE
    row = pl.BlockSpec((tm, d), lambda i: (i, 0))
    w_route_hi = w_route.astype(_BF16)
    w_route_lo = (w_route - w_route_hi.astype(_F32)).astype(_BF16)
    return pl.pallas_call(
        _out_router_kernel,
        out_shape=(jax.ShapeDtypeStruct((t, d), _F32),
                   jax.ShapeDtypeStruct((t, d), _F32),
                   jax.ShapeDtypeStruct((t, LANES), _F32)),
        grid=(t // tm,),
        in_specs=[row, row,
                  pl.BlockSpec((d, d), lambda i: (0, 0), pipeline_mode=pl.Buffered(1)),
                  pl.BlockSpec((1, d), lambda i: (0, 0)),
                  pl.BlockSpec((d, LANES), lambda i: (0, 0)),
                  pl.BlockSpec((d, LANES), lambda i: (0, 0)),
                  pl.BlockSpec((1, LANES), lambda i: (0, 0))],
        out_specs=(row, row, pl.BlockSpec((tm, LANES), lambda i: (i, 0))),
        compiler_params=_params("parallel"),
        name="out_router",
    )(mixed, x, w_out, ffn_g.reshape(1, d), w_route_hi, w_route_lo, b_route)


def _moe_kernel(be_ref, half_ref, tok_ref, nb_ref, h2_hbm, w1_ref, w3_ref, w2_ref, y_ref, xbuf, w1b, w3b, w2b, sem):
    i = pl.program_id(0)
    nb = nb_ref[0]
    slot = i % 2
    bm = xbuf.shape[1]
    hb = bm // 2

    def row_copy(tok, r, sl):
        return pltpu.make_async_copy(h2_hbm.at[pl.ds(tok, 1), :], xbuf.at[sl, pl.ds(r, 1), :], sem.at[sl])

    def rows_wait(sl, n):
        pltpu.make_async_copy(h2_hbm.at[pl.ds(0, n), :], xbuf.at[sl, pl.ds(0, n), :], sem.at[sl]).wait()

    def ffn(x):
        a = _dot(x, w1b[...])
        b = _dot(x, w3b[...])
        act = (a * _sigmoid(a) * b).astype(_BF16)
        return _dot(act, w2b[...])

    @pl.when(i == 0)
    def _():
        def body(r, carry):
            row_copy(tok_ref[r], r, 0).start()
            return carry
        lax.fori_loop(0, jnp.where(half_ref[0] == 1, hb, bm), body, 0)

    for par in range(2):
        for n_rows, is_half in ((bm, 0), (hb, 1)):
            @pl.when((i + 1 < nb) & (slot == par) & (half_ref[jnp.minimum(i + 1, nb - 1)] == is_half))
            def _(par=par, n_rows=n_rows):
                base = (i + 1) * bm
                for r in range(n_rows):
                    row_copy(tok_ref[base + r], r, 1 - par).start()

    @pl.when(i < nb)
    def _():
        half = half_ref[i] == 1

        @pl.when(half)
        def _():
            rows_wait(slot, hb)

        @pl.when(jnp.logical_not(half))
        def _():
            rows_wait(slot, bm)

        changed = (i == 0) | (be_ref[i] != be_ref[jnp.maximum(i - 1, 0)])

        @pl.when(changed)
        def _():
            w1b[...] = w1_ref[0].astype(_BF16)
            w3b[...] = w3_ref[0].astype(_BF16)
            w2b[...] = w2_ref[0].astype(_BF16)

        @pl.when(half)
        def _():
            y_ref[:hb, :] = ffn(xbuf[slot, :hb, :].astype(_BF16))
            y_ref[hb:, :] = jnp.zeros((bm - hb, y_ref.shape[1]), y_ref.dtype)

        @pl.when(jnp.logical_not(half))
        def _():
            y_ref[...] = ffn(xbuf[slot].astype(_BF16))

    @pl.when(i >= nb)
    def _():
        y_ref[...] = jnp.zeros(y_ref.shape, y_ref.dtype)


def _moe(h2, tables, w1, w3, w2):
    t, d = h2.shape
    bm = MOE_ROWS
    block_e, block_half, row_tok, n_used = tables
    n_blocks = block_e.shape[0]
    ff = w1.shape[2]
    grid_spec = pltpu.PrefetchScalarGridSpec(
        num_scalar_prefetch=4,
        grid=(n_blocks,),
        in_specs=[pl.BlockSpec(memory_space=pl.ANY),
                  pl.BlockSpec((1, d, ff), lambda i, be, *_: (be[i], 0, 0)),
                  pl.BlockSpec((1, d, ff), lambda i, be, *_: (be[i], 0, 0)),
                  pl.BlockSpec((1, ff, d), lambda i, be, *_: (be[i], 0, 0))],
        out_specs=pl.BlockSpec((bm, d), lambda i, *_: (i, 0)),
        scratch_shapes=[pltpu.VMEM((2, bm, d), _F32),
                        pltpu.VMEM((d, ff), _BF16),
                        pltpu.VMEM((d, ff), _BF16),
                        pltpu.VMEM((ff, d), _BF16),
                        pltpu.SemaphoreType.DMA((2,))],
    )
    return pl.pallas_call(
        _moe_kernel,
        out_shape=jax.ShapeDtypeStruct((n_blocks * bm, d), _F32),
        grid_spec=grid_spec,
        compiler_params=_params("arbitrary", disable_bounds_checks=True),
        name="moe_experts",
    )(block_e, block_half, row_tok, n_used, h2, w1, w3, w2)


def _final_kernel(pos_ref, y_hbm, x1_ref, route_ref, g_ref, o_ref, ybuf, sem):
    i = pl.program_id(0)
    n = pl.num_programs(0)
    tm = x1_ref.shape[0]

    def row_copy(p, r, k, sl):
        return pltpu.make_async_copy(y_hbm.at[pl.ds(p, 1), :], ybuf.at[sl, k, pl.ds(r, 1), :], sem.at[sl])

    @pl.when(i == 0)
    def _():
        def body(r, carry):
            for k in range(TOP_K):
                row_copy(pos_ref[r * TOP_K + k], r, k, 0).start()
            return carry
        lax.fori_loop(0, tm, body, 0, unroll=DMA_UNROLL)

    def step(par, issue_next):
        if issue_next:
            base = (i + 1) * tm * TOP_K
            for r in range(tm):
                for k in range(TOP_K):
                    row_copy(pos_ref[base + r * TOP_K + k], r, k, 1 - par).start()
        for k in range(TOP_K):
            pltpu.make_async_copy(y_hbm.at[pl.ds(0, tm), :], ybuf.at[par, k], sem.at[par]).wait()
        route = route_ref[...]
        x = x1_ref[...] + route[:, 2:3] * ybuf[par, 0] + route[:, 3:4] * ybuf[par, 1]
        ms = jnp.mean(x * x, axis=-1, keepdims=True)
        o_ref[...] = x * lax.rsqrt(ms + NORM_EPS) * g_ref[...]

    for par in range(2):
        @pl.when((i % 2 == par) & (i + 1 < n))
        def _(par=par):
            step(par, True)

        @pl.when((i % 2 == par) & (i + 1 == n))
        def _(par=par):
            step(par, False)


def _final(pos, y_rows, x1, route, g):
    t, d = x1.shape
    tm = FINAL_TILE
    grid_spec = pltpu.PrefetchScalarGridSpec(
        num_scalar_prefetch=1,
        grid=(t // tm,),
        in_specs=[pl.BlockSpec(memory_space=pl.ANY),
                  pl.BlockSpec((tm, d), lambda i, pos: (i, 0)),
                  pl.BlockSpec((tm, LANES), lambda i, pos: (i, 0)),
                  pl.BlockSpec((1, d), lambda i, pos: (0, 0))],
        out_specs=pl.BlockSpec((tm, d), lambda i, pos: (i, 0)),
        scratch_shapes=[pltpu.VMEM((2, TOP_K, tm, d), _F32),
                        pltpu.SemaphoreType.DMA((2,))],
    )
    return pl.pallas_call(
        _final_kernel,
        out_shape=jax.ShapeDtypeStruct((t, d), _F32),
        grid_spec=grid_spec,
        compiler_params=_params("arbitrary", disable_bounds_checks=True),
        name="combine_norm",
    )(pos, y_rows, x1, route, g.reshape(1, d))


def _routing_tables(expert, n_tokens):
    bm = MOE_ROWS
    n_assign = n_tokens * TOP_K
    n_blocks = -(-(n_assign + N_EXPERTS * (bm - 1)) // bm)
    i32 = jnp.int32
    flat_e = expert.reshape(-1)
    experts = jnp.arange(N_EXPERTS, dtype=i32)
    assign = jnp.arange(n_assign, dtype=i32)
    e_sorted, order = lax.sort((flat_e, assign), num_keys=1)
    start = jnp.sum((e_sorted[:, None] < experts[None, :]).astype(i32), axis=0)
    counts = jnp.concatenate([start[1:], jnp.full((1,), n_assign, i32)]) - start
    padded = (counts + bm - 1) // bm * bm
    pend = jnp.cumsum(padded)
    pstart = pend - padded
    shift = pstart - start
    dshift = shift - jnp.concatenate([jnp.zeros((1,), i32), shift[:-1]])
    shift_sorted = jnp.sum(jnp.where(assign[:, None] >= start[None, :], dshift[None, :], 0), axis=1)
    _, pos = lax.sort((order, assign + shift_sorted), num_keys=1)
    block_row0 = jnp.arange(n_blocks, dtype=i32) * bm
    block_e = jnp.minimum(jnp.sum((pend[None, :] <= block_row0[:, None]).astype(i32), axis=1), N_EXPERTS - 1)
    onehot = block_e[:, None] == experts[None, :]

    def per_block(table):
        return jnp.sum(jnp.where(onehot, table[None, :], 0), axis=1)

    block_start = per_block(start - pstart) + block_row0
    block_end = per_block(start + counts)
    src = block_start[:, None] + jnp.arange(bm, dtype=i32)[None, :]
    valid = src < block_end[:, None]
    tok_sorted = order // TOP_K
    idx = jnp.clip(src, 0, n_assign - 1)
    hb = bm // 2
    looked_up = jnp.concatenate([tok_sorted[idx[:, :hb]], tok_sorted[idx[:, hb:]]], axis=1)
    row_tok = jnp.where(valid, looked_up, 0)
    block_half = (block_end - block_start <= bm // 2).astype(i32)
    n_used = (pend[-1] // bm).astype(i32).reshape(1)
    return (block_e, block_half, row_tok.reshape(-1).astype(i32), n_used), pos.astype(i32)


def kernel(x, attn_norm, w_in, conv_dw, conv_dw_b, conv_ln_g, conv_ln_b, cmp_pos_k, cmp_k_w1, cmp_k_w2, cmp_pos_v, cmp_v_w1, cmp_v_w2, w_proj_conv, w_proj_nsa, w_merge, b_merge, w_out, ffn_norm, w_grp, b_grp, w_exp, b_exp, exp_w1, exp_w3, exp_w2, final_norm):
    batch, seq, d = x.shape
    t = batch * seq
    assert w_in.shape[0] == 1, "the block is specified with a single layer"
    xf = x.reshape(t, d)

    w_in_b = w_in[0].astype(_BF16)
    w_gate = w_in[0][:, 2 * CONV_CHANNELS + HEAD_COLS:].reshape(d, 3, N_KV_GROUPS, GROUP_SIZE)
    w_gate = w_gate.transpose(0, 2, 1, 3).reshape(d, GATE_COLS).astype(_BF16)

    h = _rmsnorm(xf, attn_norm[0], _BF16)
    glu = _glu_proj(h, w_in_b)
    tables = _rope_tables(seq)
    heads = _heads_proj(h, w_in_b, tables, seq)
    cmp_flat = _cmp_proj(h, w_in_b, tables, seq)
    gates = _gate_proj(h, w_gate).reshape(t, N_KV_GROUPS, 3 * GROUP_SIZE).transpose(1, 0, 2)

    half = CMP_STRIDE * HEAD_DIM
    pos = jnp.stack([cmp_pos_k[0], cmp_pos_v[0]]).reshape(2, 2, 1, half)
    cw1 = jnp.stack([cmp_k_w1[0], cmp_v_w1[0]]).astype(_BF16)
    cw2 = jnp.stack([cmp_k_w2[0], cmp_v_w2[0]]).astype(_BF16)
    cmp_kv = _compress(cmp_flat, pos, cw1, cw2, batch, seq)

    c = _conv(glu, conv_dw[0], conv_dw_b[0], conv_ln_g[0], conv_ln_b[0], batch, seq)
    o = _nsa(heads, cmp_kv, gates, batch, seq)
    mixed = _mix(c, o, h, w_proj_conv[0].astype(_BF16), w_proj_nsa[0].astype(_BF16),
                 w_merge[0].astype(_BF16), b_merge[0])

    w_route = jnp.concatenate(
        [w_grp[0], w_exp[0], jnp.zeros((d, LANES - N_GROUPS - N_EXPERTS), _F32)], axis=1)
    b_route = jnp.concatenate(
        [b_grp[0], b_exp[0], jnp.zeros((LANES - N_GROUPS - N_EXPERTS,), _F32)]).reshape(1, LANES)
    x1, h2, route = _out_router(mixed, xf, w_out[0].astype(_BF16), ffn_norm[0], w_route, b_route)

    expert = route[:, :TOP_K].astype(jnp.int32)
    moe_tables, pos_rows = _routing_tables(expert, t)
    y_rows = _moe(h2, moe_tables, exp_w1[0], exp_w3[0], exp_w2[0])
    out = _final(pos_rows, y_rows, x1, route, final_norm)
    return out.reshape(batch, seq, d)
```
